```python
import jax, jax.numpy as jnp
from jax import lax
import numpy as np


D_MODEL = 1024
BATCH = 8
SEQ = 2048
DEPTH = 2

MIX = D_MODEL
HEAD_DIM = 64
ATT_WIDTH = MIX // 2
N_Q_HEADS = ATT_WIDTH // HEAD_DIM
N_KV_HEADS = N_Q_HEADS // 4
GQA_GROUP = N_Q_HEADS // N_KV_HEADS
WINDOW = 128
BLOCK = 128
CV_CH = MIX // 4
CONV_K = 31
RET_WIDTH = MIX // 4
RET_DK = 64
RET_DV = 64
RET_HEADS = RET_WIDTH // RET_DV
N_META = 16
PAD = BLOCK - N_META
D_FF = ((8 * D_MODEL // 3 + 127) // 128) * 128
FFN_CONV_K = 3
RMS_EPS = 1e-6
LN_EPS = 1e-5
SPLIT_SIZES = (N_Q_HEADS * HEAD_DIM, N_KV_HEADS * HEAD_DIM, N_KV_HEADS * HEAD_DIM,
               CV_CH, CV_CH,
               RET_HEADS * RET_DK, RET_HEADS * RET_DK, RET_HEADS * RET_DV, RET_WIDTH)
IN_WIDTH = sum(SPLIT_SIZES)

kernel_name = 'hymba_swa_conformer_retention_convffn'


def rmsnorm(x, g):
    xf = x.astype(jnp.float32)
    y = xf * lax.rsqrt(jnp.mean(xf * xf, axis=-1, keepdims=True) + RMS_EPS)
    return (y * g.astype(jnp.float32)).astype(x.dtype)


def layernorm(x, g, b):
    xf = x.astype(jnp.float32)
    mu = jnp.mean(xf, axis=-1, keepdims=True)
    var = jnp.mean(jnp.square(xf - mu), axis=-1, keepdims=True)
    y = (xf - mu) * lax.rsqrt(var + LN_EPS)
    return (y * g.astype(jnp.float32) + b.astype(jnp.float32)).astype(x.dtype)


def causal_dwconv(x, w, b):
    ksz, ch = w.shape
    y = lax.conv_general_dilated(
        x, w.astype(x.dtype)[:, None, :], window_strides=(1,), padding=[(ksz - 1, 0)],
        dimension_numbers=('NWC', 'WIO', 'NWC'), feature_group_count=ch)
    return y + b.astype(x.dtype)


def split_cols(z):
    out, start = [], 0
    for w in SPLIT_SIZES:
        out.append(z[..., start:start + w])
        start += w
    return out


def sliding_window_sink_attention(q, k, v, sinks, slopes):
    B, L = q.shape[0], q.shape[1]
    nb = L // BLOCK
    qb = q.reshape(B, nb, BLOCK, N_KV_HEADS, GQA_GROUP, HEAD_DIM)

    def band(z):
        zb = z.reshape(B, nb, BLOCK, N_KV_HEADS, HEAD_DIM)
        prev = jnp.concatenate([jnp.zeros_like(zb[:, :1]), zb[:, :-1]], axis=1)
        return jnp.concatenate([prev, zb], axis=2)

    def meta(z):
        return jnp.broadcast_to(z[:, None, PAD:BLOCK], (B, nb, N_META, N_KV_HEADS, HEAD_DIM))

    keys = jnp.concatenate([meta(k), band(k)], axis=2)
    vals = jnp.concatenate([meta(v), band(v)], axis=2)

    blk = jnp.arange(nb)[:, None]
    t = blk * BLOCK + jnp.arange(BLOCK)[None, :]
    s_band = (blk - 1) * BLOCK + jnp.arange(2 * BLOCK)[None, :]
    s_meta = PAD + jnp.arange(N_META)
    d_band = t[:, :, None] - s_band[:, None, :]
    d_meta = t[:, :, None] - s_meta[None, None, :]
    ok_band = (d_band >= 0) & (d_band < WINDOW) & (s_band[:, None, :] >= BLOCK)
    ok_meta = d_meta >= 0
    ok = jnp.concatenate([ok_meta, ok_band], axis=-1)
    dist = jnp.concatenate([jnp.minimum(d_meta, WINDOW), d_band], axis=-1).astype(jnp.float32)

    scale = HEAD_DIM ** -0.5
    logits = jnp.einsum('bnihgd,bnjhd->bnhgij', qb, keys).astype(jnp.float32) * scale
    sl = slopes.reshape(N_KV_HEADS, GQA_GROUP)[None, None, :, :, None, None]
    logits = logits - sl * dist[None, :, None, None, :, :]
    logits = jnp.where(ok[None, :, None, None], logits, -jnp.inf)
    sink = jnp.broadcast_to(
        sinks.astype(jnp.float32).reshape(N_KV_HEADS, GQA_GROUP)[None, None, :, :, None, None],
        logits.shape[:-1] + (1,))
    probs = jax.nn.softmax(jnp.concatenate([logits, sink], axis=-1), axis=-1)[..., :-1]
    out = jnp.einsum('bnhgij,bnjhd->bnihgd', probs.astype(v.dtype), vals)
    return out.reshape(B, L, N_Q_HEADS * HEAD_DIM)


def conformer_conv(a, b, w_dw, b_dw, ln_g, ln_b, w_pw):
    u = a * jax.nn.sigmoid(b)
    u = causal_dwconv(u, w_dw, b_dw)
    u = layernorm(u, ln_g, ln_b)
    u = jax.nn.silu(u)
    return u @ w_pw.astype(u.dtype)


def chunkwise_retention(q, k, v, g, gn_g, valid):
    B, L = q.shape[0], q.shape[1]
    nc = L // BLOCK
    qf = q.astype(jnp.float32).reshape(B, nc, BLOCK, RET_HEADS, RET_DK)
    kf = (k.astype(jnp.float32) * (RET_DK ** -0.5) * valid[:, :, None, None].astype(jnp.float32))
    kf = kf.reshape(B, nc, BLOCK, RET_HEADS, RET_DK)
    vf = v.astype(jnp.float32).reshape(B, nc, BLOCK, RET_HEADS, RET_DV)

    log_gamma = jnp.log1p(-jnp.exp2(-5.0 - jnp.arange(RET_HEADS, dtype=jnp.float32)))
    idx = jnp.arange(BLOCK, dtype=jnp.float32)
    diff = idx[:, None] - idx[None, :]
    decay = jnp.where(diff[None] >= 0, jnp.exp(jnp.maximum(diff, 0.0)[None] * log_gamma[:, None, None]), 0.0)
    zeta = jnp.exp((BLOCK - 1 - idx)[None, :] * log_gamma[:, None])
    xi = jnp.exp((idx + 1.0)[None, :] * log_gamma[:, None])
    chunk_decay = jnp.exp(BLOCK * log_gamma)

    scores = jnp.einsum('bnihd,bnjhd->bnhij', qf, kf) * decay[None, None]
    y = jnp.einsum('bnhij,bnjhe->bnihe', scores, vf)
    kv = jnp.einsum('bnjhd,bnjhe,hj->bnhde', kf, vf, zeta)

    def step(state, kv_n):
        return chunk_decay[None, :, None, None] * state + kv_n, state

    init = jnp.zeros((B, RET_HEADS, RET_DK, RET_DV), jnp.float32)
    _, r_prev = lax.scan(step, init, jnp.moveaxis(kv, 1, 0))
    r_prev = jnp.moveaxis(r_prev, 0, 1)
    y = y + jnp.einsum('bnihd,bnhde,hi->bnihe', qf, r_prev, xi)

    y = y.reshape(B, L, RET_HEADS, RET_DV)
    mu = jnp.mean(y, axis=-1, keepdims=True)
    var = jnp.mean(jnp.square(y - mu), axis=-1, keepdims=True)
    y = ((y - mu) * lax.rsqrt(var + LN_EPS)).reshape(B, L, RET_WIDTH) * gn_g.astype(jnp.float32)
    return jax.nn.silu(g) * y.astype(g.dtype)


def setup_inputs(seed: int = 0) -> dict:
    key = jax.random.key(seed)
    ks = jax.random.split(key, 22)
    f32 = jnp.float32

    def nrm(k, shape, scale):
        return jax.random.normal(k, shape, f32) * scale

    return {
        'x': nrm(ks[0], (BATCH, SEQ, D_MODEL), 1.0),
        'meta': nrm(ks[1], (N_META, D_MODEL), 1.0),
        'norm_mix_g': 1.0 + nrm(ks[2], (DEPTH, D_MODEL), 0.02),
        'w_in': nrm(ks[3], (DEPTH, D_MODEL, IN_WIDTH), D_MODEL ** -0.5),
        'q_norm_g': 1.0 + nrm(ks[4], (DEPTH, HEAD_DIM), 0.02),
        'k_norm_g': 1.0 + nrm(ks[5], (DEPTH, HEAD_DIM), 0.02),
        'attn_sinks': nrm(ks[6], (DEPTH, N_Q_HEADS), 0.5),
        'attn_out_g': 1.0 + nrm(ks[7], (DEPTH, ATT_WIDTH), 0.02),
        'cv_dw_w': nrm(ks[8], (DEPTH, CONV_K, CV_CH), CONV_K ** -0.5),
        'cv_dw_b': nrm(ks[9], (DEPTH, CV_CH), 0.02),
        'cv_ln_g': 1.0 + nrm(ks[10], (DEPTH, CV_CH), 0.02),
        'cv_ln_b': nrm(ks[11], (DEPTH, CV_CH), 0.02),
        'cv_pw': nrm(ks[12], (DEPTH, CV_CH, CV_CH), CV_CH ** -0.5),
        'cv_out_g': 1.0 + nrm(ks[13], (DEPTH, CV_CH), 0.02),
        'ret_gn_g': 1.0 + nrm(ks[14], (DEPTH, RET_WIDTH), 0.02),
        'w_out': nrm(ks[15], (DEPTH, MIX, D_MODEL), MIX ** -0.5),
        'norm_ffn_g': 1.0 + nrm(ks[16], (DEPTH, D_MODEL), 0.02),
        'ffn_up': nrm(ks[17], (DEPTH, D_MODEL, 2 * D_FF), D_MODEL ** -0.5),
        'ffn_dw_w': nrm(ks[18], (DEPTH, FFN_CONV_K, 2 * D_FF), FFN_CONV_K ** -0.5),
        'ffn_dw_b': nrm(ks[19], (DEPTH, 2 * D_FF), 0.02),
        'ffn_down': nrm(ks[20], (DEPTH, D_FF, D_MODEL), D_FF ** -0.5),
    }


def reference(x, meta, norm_mix_g, w_in, q_norm_g, k_norm_g, attn_sinks, attn_out_g,
              cv_dw_w, cv_dw_b, cv_ln_g, cv_ln_b, cv_pw, cv_out_g, ret_gn_g, w_out,
              norm_ffn_g, ffn_up, ffn_dw_w, ffn_dw_b, ffn_down):
    B, S, D = x.shape
    L = S + BLOCK
    dt = x.dtype
    h = jnp.concatenate([
        jnp.zeros((B, PAD, D), dt),
        jnp.broadcast_to(meta.astype(dt)[None], (B, N_META, D)),
        x], axis=1)
    valid = (jnp.arange(L) >= PAD).astype(dt)[None, :, None]
    slopes = jnp.exp2(-8.0 * jnp.arange(1, N_Q_HEADS + 1, dtype=jnp.float32) / N_Q_HEADS)

    for l in range(DEPTH):
        u = rmsnorm(h, norm_mix_g[l])
        proj = u @ w_in[l].astype(dt)
        q, k, v, ca, cb, rq, rk, rv, rg = split_cols(proj)
        q = rmsnorm(q.reshape(B, L, N_Q_HEADS, HEAD_DIM), q_norm_g[l])
        k = rmsnorm(k.reshape(B, L, N_KV_HEADS, HEAD_DIM), k_norm_g[l])
        v = v.reshape(B, L, N_KV_HEADS, HEAD_DIM)
        y_att = rmsnorm(sliding_window_sink_attention(q, k, v, attn_sinks[l], slopes), attn_out_g[l])
        y_cv = rmsnorm(conformer_conv(ca, cb, cv_dw_w[l], cv_dw_b[l], cv_ln_g[l], cv_ln_b[l], cv_pw[l]),
                       cv_out_g[l])
        y_ret = chunkwise_retention(rq.reshape(B, L, RET_HEADS, RET_DK),
                                    rk.reshape(B, L, RET_HEADS, RET_DK),
                                    rv.reshape(B, L, RET_HEADS, RET_DV),
                                    rg, ret_gn_g[l], valid[..., 0])
        y = jnp.concatenate([y_att, y_cv, y_ret], axis=-1) @ w_out[l].astype(dt)
        h = h + valid * y
        f = rmsnorm(h, norm_ffn_g[l]) @ ffn_up[l].astype(dt)
        f = causal_dwconv(f, ffn_dw_w[l], ffn_dw_b[l])
        fg, fu = f[..., :D_FF], f[..., D_FF:]
        h = h + valid * ((jax.nn.silu(fg) * fu) @ ffn_down[l].astype(dt))

    return h[:, BLOCK:, :]
```

```python
import functools

import jax
import jax.numpy as jnp
from jax import lax
from jax.experimental import pallas as pl
from jax.experimental.pallas import tpu as pltpu

F32 = jnp.float32
BF16 = jnp.bfloat16

HEAD_DIM = 64
BLOCK = 128
WINDOW = 128
N_META = 16
PAD = BLOCK - N_META
CONV_K = 31
FFN_CONV_K = 3
RET_DK = 64
RET_HEADS = 4
RMS_EPS = 1e-6
LN_EPS = 1e-5
NEG = -1e30

CONV_TAIL = 32
FFN_TAIL = 8
VMEM_LIMIT = 56 * 1024 * 1024


def _dot(a, b):
    return jnp.dot(a, b, preferred_element_type=F32)


def _dot_nt(a, b):
    return lax.dot_general(a, b, (((1,), (1,)), ((), ())), preferred_element_type=F32)


def _dot_tn(a, b):
    return lax.dot_general(a, b, (((0,), (0,)), ((), ())), preferred_element_type=F32)


def _sigmoid(x):
    return 1.0 / (1.0 + jnp.exp(-x))


def _rms(x, g):
    return x * lax.rsqrt(jnp.mean(x * x, axis=-1, keepdims=True) + RMS_EPS) * g


def _segmean(x, ones_bd):
    hi = x.astype(BF16)
    lo = (x - hi.astype(F32)).astype(BF16)
    return (_dot(hi, ones_bd) + _dot(lo, ones_bd)) * (1.0 / HEAD_DIM)


def _inproj_kernel(h_ref, g_ref, w_ref, o_ref):
    u = _rms(h_ref[...], g_ref[...])
    o_ref[...] = _dot(u.astype(BF16), w_ref[...])


def _inproj(h2, g, w, tm):
    rows, d = h2.shape
    n = w.shape[1]
    return pl.pallas_call(
        _inproj_kernel,
        out_shape=jax.ShapeDtypeStruct((rows, n), F32),
        grid=(rows // tm,),
        in_specs=[
            pl.BlockSpec((tm, d), lambda i: (i, 0)),
            pl.BlockSpec((1, d), lambda i: (0, 0)),
            pl.BlockSpec((d, n), lambda i: (0, 0)),
        ],
        out_specs=pl.BlockSpec((tm, n), lambda i: (i, 0)),
        compiler_params=pltpu.CompilerParams(
            dimension_semantics=("arbitrary",), vmem_limit_bytes=VMEM_LIMIT),
        name="in_proj",
    )(h2, g, w)


def _head_blocks(x, width):
    heads = width // HEAD_DIM
    seg = lax.shift_right_logical(lax.broadcasted_iota(jnp.int32, x.shape, 1), 6)
    return jnp.concatenate([jnp.where(seg == j, x, 0.0) for j in range(heads)], axis=0)


def _mixer_kernel(sink_ref, proj_ref, h_ref, qg_ref, kg_ref, ones_ref, aog_ref,
                  dww_ref, dwb_ref, lng_ref, lnb_ref, pw_ref, cvg_ref,
                  dec_ref, zeta_ref, xi_ref, cdec_ref, gng_ref, wout_ref,
                  o_ref,
                  kbp_ref, vbp_ref, kbm_ref, vbm_ref, ubuf_ref, state_ref):
    n = pl.program_id(1)
    wr = lax.rem(n, 2)
    rd = 1 - wr

    @pl.when(n == 0)
    def _init():
        kbp_ref[...] = jnp.zeros_like(kbp_ref)
        vbp_ref[...] = jnp.zeros_like(vbp_ref)
        ubuf_ref[0:CONV_TAIL, :] = jnp.zeros((CONV_TAIL, ubuf_ref.shape[1]), F32)
        state_ref[...] = jnp.zeros_like(state_ref)

    ones_bd = ones_ref[...]
    row = lax.broadcasted_iota(jnp.int32, (BLOCK, BLOCK), 0)
    col = lax.broadcasted_iota(jnp.int32, (BLOCK, BLOCK), 1)
    row1 = lax.broadcasted_iota(jnp.int32, (BLOCK, 1), 0)
    tok_valid = (n * BLOCK + row1) >= PAD

    q = proj_ref[:, 0:512]
    k = proj_ref[:, 512:640]
    v = proj_ref[:, 640:768]
    qn = jnp.concatenate(
        [q[:, 0:256] * lax.rsqrt(_segmean(q[:, 0:256] * q[:, 0:256], ones_bd) + RMS_EPS),
         q[:, 256:512] * lax.rsqrt(_segmean(q[:, 256:512] * q[:, 256:512], ones_bd) + RMS_EPS)],
        axis=1) * qg_ref[...]
    qs = (qn * (HEAD_DIM ** -0.5)).astype(BF16)
    kn = k * lax.rsqrt(_segmean(k * k, ones_bd[0:128, 0:128]) + RMS_EPS) * kg_ref[...]

    low = col < HEAD_DIM
    k_sw = pltpu.roll(kn, HEAD_DIM, axis=1)
    v_sw = pltpu.roll(v, HEAD_DIM, axis=1)
    kk = (jnp.where(low, kn, k_sw), jnp.where(low, k_sw, kn))
    vv = (jnp.where(low, v, v_sw), jnp.where(low, v_sw, v))

    d_prev = (row - col + BLOCK).astype(F32)
    ok_prev = jnp.logical_and(col > row, n >= 2)
    d_cur = (row - col).astype(F32)
    ok_cur = jnp.logical_and(col <= row, n >= 1)
    m_idx = jnp.bitwise_and(col, N_META - 1)
    m_head = lax.shift_right_logical(col, 4)
    d_meta_i = n * BLOCK + row - PAD - m_idx
    ok_meta = jnp.logical_and(d_meta_i >= 0, col < 4 * N_META)
    d_meta = jnp.minimum(d_meta_i, WINDOW).astype(F32)

    att = []
    for g in range(2):
        krep = jnp.concatenate([kk[g], kk[g]], axis=1)
        vrep = jnp.concatenate([vv[g], vv[g]], axis=1)
        kb_cur = _head_blocks(krep, 256).astype(BF16)
        vb_cur = _head_blocks(vrep, 256).astype(BF16)

        @pl.when(n == 0)
        def _meta():
            z = jnp.zeros((BLOCK - 4 * N_META, 256), F32)
            kbm_ref[g] = jnp.concatenate(
                [_head_blocks(krep[PAD:BLOCK], 256), z], axis=0).astype(BF16)
            vbm_ref[g] = jnp.concatenate(
                [_head_blocks(vrep[PAD:BLOCK], 256), z], axis=0).astype(BF16)

        qg = qs[:, 256 * g:256 * (g + 1)]
        s_prev = _dot_nt(qg, kbp_ref[2 * rd + g])
        s_cur = _dot_nt(qg, kb_cur)
        s_meta = _dot_nt(qg, kbm_ref[g])

        slopes = [2.0 ** (-(4 * g + j + 1)) for j in range(4)]
        sl_meta = jnp.where(m_head == 0, slopes[0],
                            jnp.where(m_head == 1, slopes[1],
                                      jnp.where(m_head == 2, slopes[2], slopes[3])))
        sm = jnp.where(ok_meta, s_meta - sl_meta * d_meta, NEG)

        sp, sc, mx = [], [], []
        for j in range(4):
            spj = jnp.where(ok_prev, s_prev[:, 128 * j:128 * (j + 1)] - slopes[j] * d_prev, NEG)
            scj = jnp.where(ok_cur, s_cur[:, 128 * j:128 * (j + 1)] - slopes[j] * d_cur, NEG)
            t = jnp.maximum(jnp.maximum(spj, scj), jnp.where(m_head == j, sm, NEG))
            mj = jnp.maximum(jnp.max(t, axis=-1, keepdims=True), sink_ref[4 * g + j])
            sp.append(spj)
            sc.append(scj)
            mx.append(mj)
        m_meta = jnp.where(m_head == 0, mx[0],
                           jnp.where(m_head == 1, mx[1],
                                     jnp.where(m_head == 2, mx[2], mx[3])))
        pm = jnp.exp(sm - m_meta)
        pp, pc, linv = [], [], []
        for j in range(4):
            ppj = jnp.exp(sp[j] - mx[j])
            pcj = jnp.exp(sc[j] - mx[j])
            lj = jnp.sum(ppj + pcj + jnp.where(m_head == j, pm, 0.0), axis=-1, keepdims=True)
            lj = lj + jnp.exp(sink_ref[4 * g + j] - mx[j])
            pp.append(ppj.astype(BF16))
            pc.append(pcj.astype(BF16))
            linv.append(1.0 / lj)
        o = (_dot(jnp.concatenate(pp, axis=1), vbp_ref[2 * rd + g])
             + _dot(jnp.concatenate(pc, axis=1), vb_cur)
             + _dot(pm.astype(BF16), vbm_ref[g]))
        seg = lax.shift_right_logical(lax.broadcasted_iota(jnp.int32, (BLOCK, 256), 1), 6)
        lfull = jnp.where(seg == 0, linv[0],
                          jnp.where(seg == 1, linv[1],
                                    jnp.where(seg == 2, linv[2], linv[3])))
        att.append(o * lfull)
        kbp_ref[2 * wr + g] = kb_cur
        vbp_ref[2 * wr + g] = vb_cur
    y_att = _rms(jnp.concatenate(att, axis=1), aog_ref[...])

    ca = proj_ref[:, 768:1024]
    cb = proj_ref[:, 1024:1280]
    ubuf_ref[CONV_TAIL:CONV_TAIL + BLOCK, :] = ca * _sigmoid(cb)
    base = CONV_TAIL - (CONV_K - 1)
    acc = dww_ref[0:1, :] * ubuf_ref[base:base + BLOCK, :]
    for t in range(1, CONV_K):
        acc = acc + dww_ref[t:t + 1, :] * ubuf_ref[base + t:base + t + BLOCK, :]
    acc = acc + dwb_ref[...]
    ubuf_ref[0:CONV_TAIL, :] = ubuf_ref[BLOCK:BLOCK + CONV_TAIL, :]
    mu = jnp.mean(acc, axis=-1, keepdims=True)
    dc = acc - mu
    var = jnp.mean(dc * dc, axis=-1, keepdims=True)
    cn = dc * lax.rsqrt(var + LN_EPS) * lng_ref[...] + lnb_ref[...]
    cs = cn * _sigmoid(cn)
    y_cv = _rms(_dot(cs.astype(BF16), pw_ref[...]), cvg_ref[...])

    rq = proj_ref[:, 1280:1536]
    rk = jnp.where(tok_valid, proj_ref[:, 1536:1792], 0.0)
    rv = proj_ref[:, 1792:2048]
    rg = proj_ref[:, 2048:2304]
    kb_r = _head_blocks(rk, 256).astype(BF16)
    vb_r = _head_blocks(rv, 256).astype(BF16)
    scores = _dot_nt(rq.astype(BF16), kb_r) * dec_ref[...]
    state = state_ref[...]
    y_r = (_dot(scores.astype(BF16), vb_r)
           + _dot((rq * xi_ref[...]).astype(BF16), state.astype(BF16)))
    kv = _dot_tn(rk.astype(BF16), (rv * zeta_ref[...]).astype(BF16))
    r2 = lax.shift_right_logical(lax.broadcasted_iota(jnp.int32, (256, 256), 0), 6)
    c2 = lax.shift_right_logical(lax.broadcasted_iota(jnp.int32, (256, 256), 1), 6)
    state_ref[...] = cdec_ref[...] * state + jnp.where(r2 == c2, kv, 0.0)
    mu_r = _segmean(y_r, ones_bd)
    dr = y_r - mu_r
    var_r = _segmean(dr * dr, ones_bd)
    yn = dr * lax.rsqrt(var_r + LN_EPS) * gng_ref[...]
    y_ret = rg * _sigmoid(rg) * yn

    ycat = jnp.concatenate([y_att, y_cv, y_ret], axis=1).astype(BF16)
    y = _dot(ycat, wout_ref[...])
    o_ref[...] = h_ref[...] + jnp.where(tok_valid, y, 0.0)


def _mixer(proj, h2, sinks, qg, kg, ones_bd, aog, dww, dwb, lng, lnb, pw, cvg,
           dec, zeta, xi, cdec, gng, wout, batch, nblk):
    rows, d = h2.shape
    nin = proj.shape[1]

    def rowblk(b, n):
        return (b * nblk + n, 0)

    def const(b, n):
        return (0, 0)

    def cspec(a):
        return pl.BlockSpec(a.shape, const)

    return pl.pallas_call(
        _mixer_kernel,
        out_shape=jax.ShapeDtypeStruct((rows, d), F32),
        grid=(batch, nblk),
        in_specs=[
            pl.BlockSpec(memory_space=pltpu.SMEM),
            pl.BlockSpec((BLOCK, nin), rowblk),
            pl.BlockSpec((BLOCK, d), rowblk),
            cspec(qg), cspec(kg), cspec(ones_bd), cspec(aog),
            cspec(dww), cspec(dwb), cspec(lng), cspec(lnb), cspec(pw), cspec(cvg),
            cspec(dec), cspec(zeta), cspec(xi), cspec(cdec), cspec(gng), cspec(wout),
        ],
        out_specs=pl.BlockSpec((BLOCK, d), rowblk),
        scratch_shapes=[
            pltpu.VMEM((4, 4 * BLOCK, 256), BF16),
            pltpu.VMEM((4, 4 * BLOCK, 256), BF16),
            pltpu.VMEM((2, BLOCK, 256), BF16),
            pltpu.VMEM((2, BLOCK, 256), BF16),
            pltpu.VMEM((CONV_TAIL + BLOCK, 256), F32),
            pltpu.VMEM((256, 256), F32),
        ],
        compiler_params=pltpu.CompilerParams(
            dimension_semantics=("arbitrary", "arbitrary"), vmem_limit_bytes=VMEM_LIMIT),
        name="mixer",
    )(sinks, proj, h2, qg, kg, ones_bd, aog, dww, dwb, lng, lnb, pw, cvg,
      dec, zeta, xi, cdec, gng, wout)


def _ffn_kernel(h_ref, g_ref, up_ref, dww_ref, dwb_ref, down_ref, o_ref,
                fbuf_ref, carry_ref, acc_ref, *, tm, dff, cw, pad_rows):
    t = pl.program_id(1)
    nchunk = dff // cw

    @pl.when(t == 0)
    def _init():
        carry_ref[...] = jnp.zeros_like(carry_ref)

    h = h_ref[...]
    u = _rms(h, g_ref[...]).astype(BF16)
    for c in range(nchunk):
        for part, off in ((0, c * cw), (1, dff + c * cw)):
            f = _dot(u, up_ref[:, off:off + cw])
            lo = part * cw
            fbuf_ref[0:FFN_TAIL, lo:lo + cw] = carry_ref[c, :, lo:lo + cw]
            fbuf_ref[FFN_TAIL:FFN_TAIL + tm, lo:lo + cw] = f
        carry_ref[c] = fbuf_ref[tm:tm + FFN_TAIL, :]
        halves = []
        for part, off in ((0, c * cw), (1, dff + c * cw)):
            lo = part * cw
            fc = dwb_ref[:, off:off + cw]
            for kk in range(FFN_CONV_K):
                s = FFN_TAIL - (FFN_CONV_K - 1) + kk
                fc = fc + dww_ref[kk:kk + 1, off:off + cw] * fbuf_ref[s:s + tm, lo:lo + cw]
            halves.append(fc)
        act = (halves[0] * _sigmoid(halves[0]) * halves[1]).astype(BF16)
        contrib = _dot(act, down_ref[c * cw:(c + 1) * cw, :])
        if c == 0:
            acc_ref[...] = contrib
        else:
            acc_ref[...] += contrib
    row = t * tm + lax.broadcasted_iota(jnp.int32, (tm, 1), 0)
    o_ref[...] = h + jnp.where(row >= pad_rows, acc_ref[...], 0.0)


def _ffn(h2, g, up, dww, dwb, down, batch, tiles, tm, cw):
    rows, d = h2.shape
    dff = down.shape[0]

    def rowblk(b, t):
        return (b * tiles + t, 0)

    def const(b, t):
        return (0, 0)

    kern = functools.partial(_ffn_kernel, tm=tm, dff=dff, cw=cw, pad_rows=PAD)
    return pl.pallas_call(
        kern,
        out_shape=jax.ShapeDtypeStruct((rows, d), F32),
        grid=(batch, tiles),
        in_specs=[
            pl.BlockSpec((tm, d), rowblk),
            pl.BlockSpec(g.shape, const),
            pl.BlockSpec(up.shape, const),
            pl.BlockSpec(dww.shape, const),
            pl.BlockSpec(dwb.shape, const),
            pl.BlockSpec(down.shape, const),
        ],
        out_specs=pl.BlockSpec((tm, d), rowblk),
        scratch_shapes=[
            pltpu.VMEM((FFN_TAIL + tm, 2 * cw), F32),
            pltpu.VMEM((dff // cw, FFN_TAIL, 2 * cw), F32),
            pltpu.VMEM((tm, d), F32),
        ],
        compiler_params=pltpu.CompilerParams(
            dimension_semantics=("arbitrary", "arbitrary"), vmem_limit_bytes=VMEM_LIMIT),
        name="ffn",
    )(h2, g, up, dww, dwb, down)


def _retention_tables():
    log_gamma = jnp.log1p(-jnp.exp2(-5.0 - jnp.arange(RET_HEADS, dtype=F32)))
    idx = jnp.arange(BLOCK, dtype=F32)
    diff = idx[:, None] - idx[None, :]
    decay = jnp.where(diff[None] >= 0,
                      jnp.exp(jnp.maximum(diff, 0.0)[None] * log_gamma[:, None, None]), 0.0)
    zeta = jnp.exp((BLOCK - 1 - idx)[None, :] * log_gamma[:, None])
    xi = jnp.exp((idx + 1.0)[None, :] * log_gamma[:, None])
    chunk_decay = jnp.exp(BLOCK * log_gamma)
    scale = RET_DK ** -0.5
    dec_t = jnp.transpose(decay, (1, 0, 2)).reshape(BLOCK, RET_HEADS * BLOCK) * scale
    zeta_t = jnp.repeat(zeta.T, RET_DK, axis=1) * scale
    xi_t = jnp.repeat(xi.T, RET_DK, axis=1)
    cdec_t = jnp.repeat(chunk_decay, RET_DK)[None, :]
    return dec_t, zeta_t, xi_t, cdec_t


def kernel(x, meta, norm_mix_g, w_in, q_norm_g, k_norm_g, attn_sinks, attn_out_g, cv_dw_w, cv_dw_b, cv_ln_g, cv_ln_b, cv_pw, cv_out_g, ret_gn_g, w_out, norm_ffn_g, ffn_up, ffn_dw_w, ffn_dw_b, ffn_down):
    batch, seq, d = x.shape
    depth = w_in.shape[0]
    length = seq + BLOCK
    nblk = length // BLOCK
    rows = batch * length
    h = jnp.concatenate([
        jnp.zeros((batch, PAD, d), x.dtype),
        jnp.broadcast_to(meta.astype(x.dtype)[None], (batch, N_META, d)),
        x], axis=1).reshape(rows, d)

    dec_t, zeta_t, xi_t, cdec_t = _retention_tables()
    seg = jnp.arange(256) // HEAD_DIM
    ones_bd = (seg[:, None] == seg[None, :]).astype(BF16)

    ffn_tiles = 4
    ffn_tm = length // ffn_tiles
    for l in range(depth):
        proj = _inproj(h, norm_mix_g[l][None, :], w_in[l].astype(BF16), tm=512)
        h = _mixer(
            proj, h, attn_sinks[l],
            jnp.tile(q_norm_g[l], 8)[None, :], jnp.tile(k_norm_g[l], 2)[None, :], ones_bd,
            attn_out_g[l][None, :], cv_dw_w[l], cv_dw_b[l][None, :], cv_ln_g[l][None, :],
            cv_ln_b[l][None, :], cv_pw[l].astype(BF16), cv_out_g[l][None, :],
            dec_t, zeta_t, xi_t, cdec_t, ret_gn_g[l][None, :], w_out[l].astype(BF16),
            batch, nblk)
        h = _ffn(h, norm_ffn_g[l][None, :], ffn_up[l].astype(BF16), ffn_dw_w[l],
                 ffn_dw_b[l][None, :], ffn_down[l].astype(BF16), batch, ffn_tiles, ffn_tm, cw=256)
    return h.reshape(batch, length, d)[:, BLOCK:, :]
```

```python
import functools

import jax
import jax.numpy as jnp
from jax import lax
from jax.experimental import pallas as pl
from jax.experimental.pallas import tpu as pltpu

F32 = jnp.float32
BF16 = jnp.bfloat16

HEAD_DIM = 64
BLOCK = 128
WINDOW = 128
N_META = 16
PAD = BLOCK - N_META
CONV_K = 31
FFN_CONV_K = 3
RET_DK = 64
RET_HEADS = 4
RMS_EPS = 1e-6
LN_EPS = 1e-5
NEG = -1e30

SUBLANES = 8
CONV_TAIL = 32
FFN_TAIL = SUBLANES
VMEM_LIMIT = 56 * 1024 * 1024


def _dot(a, b):
    return jnp.dot(a, b, preferred_element_type=F32)


def _sigmoid(x):
    return 1.0 / (1.0 + jnp.exp(-x))


def _rms(x, g):
    return x * lax.rsqrt(jnp.mean(x * x, axis=-1, keepdims=True) + RMS_EPS) * g


def _segmean(x, ones_bd):
    hi = x.astype(BF16)
    lo = (x - hi.astype(F32)).astype(BF16)
    return (_dot(hi, ones_bd) + _dot(lo, ones_bd)) * (1.0 / HEAD_DIM)


def _inproj_kernel(h_ref, g_ref, w_ref, o_ref):
    u = _rms(h_ref[...], g_ref[...])
    o_ref[...] = _dot(u.astype(BF16), w_ref[...])


def _inproj(h2, g, w, tm):
    rows, d = h2.shape
    n = w.shape[1]
    return pl.pallas_call(
        _inproj_kernel,
        out_shape=jax.ShapeDtypeStruct((rows, n), F32),
        grid=(rows // tm,),
        in_specs=[
            pl.BlockSpec((tm, d), lambda i: (i, 0)),
            pl.BlockSpec((1, d), lambda i: (0, 0)),
            pl.BlockSpec((d, n), lambda i: (0, 0)),
        ],
        out_specs=pl.BlockSpec((tm, n), lambda i: (i, 0)),
        compiler_params=pltpu.CompilerParams(
            dimension_semantics=("arbitrary",), vmem_limit_bytes=VMEM_LIMIT),
        name="in_proj",
    )(h2, g, w)


def _row_blocks(x, width):
    heads = width // HEAD_DIM
    seg = lax.shift_right_logical(lax.broadcasted_iota(jnp.int32, x.shape, 1), 6)
    return jnp.concatenate([jnp.where(seg == j, x, 0.0) for j in range(heads)], axis=0)


def _diag_blocks(xt, heads):
    z = jnp.zeros_like(xt[0])
    return jnp.concatenate(
        [jnp.concatenate([xt[j] if i == j else z for i in range(heads)], axis=1)
         for j in range(heads)], axis=0)


def _mixer_kernel(sink_ref, proj_ref, qg_ref, kg_ref, ones_ref, aog_ref,
                  dww_ref, dwb_ref, lng_ref, lnb_ref, pw_ref, cvg_ref,
                  dec_ref, zeta_ref, xi_ref, cdec_ref, gng_ref,
                  o_ref,
                  kbp_ref, vbp_ref, kbm_ref, vbm_ref, ubuf_ref, state_ref):
    n = pl.program_id(1)
    wr = lax.rem(n, 2)
    rd = 1 - wr

    @pl.when(n == 0)
    def _init():
        kbp_ref[...] = jnp.zeros_like(kbp_ref)
        vbp_ref[...] = jnp.zeros_like(vbp_ref)
        ubuf_ref[1, BLOCK:BLOCK + CONV_TAIL, :] = jnp.zeros((CONV_TAIL, ubuf_ref.shape[2]), F32)
        state_ref[1] = jnp.zeros(state_ref.shape[1:], F32)

    ones_bd = ones_ref[...]
    row = lax.broadcasted_iota(jnp.int32, (BLOCK, BLOCK), 0)
    col = lax.broadcasted_iota(jnp.int32, (BLOCK, BLOCK), 1)
    row1 = lax.broadcasted_iota(jnp.int32, (BLOCK, 1), 0)
    tok_valid = (n * BLOCK + row1) >= PAD

    q = proj_ref[:, 0:512]
    k = proj_ref[:, 512:640]
    v = proj_ref[:, 640:768]
    qn = jnp.concatenate(
        [q[:, 0:256] * lax.rsqrt(_segmean(q[:, 0:256] * q[:, 0:256], ones_bd) + RMS_EPS),
         q[:, 256:512] * lax.rsqrt(_segmean(q[:, 256:512] * q[:, 256:512], ones_bd) + RMS_EPS)],
        axis=1) * qg_ref[...]
    qs = (qn * (HEAD_DIM ** -0.5)).astype(BF16)
    kn = k * lax.rsqrt(_segmean(k * k, ones_bd[0:128, 0:128]) + RMS_EPS) * kg_ref[...]
    knt = kn.T

    low = col < HEAD_DIM
    v_sw = pltpu.roll(v, HEAD_DIM, axis=1)
    vv = (jnp.where(low, v, v_sw), jnp.where(low, v_sw, v))

    d_prev = (row - col + BLOCK).astype(F32)
    ok_prev = jnp.logical_and(col > row, n >= 2)
    d_cur = (row - col).astype(F32)
    ok_cur = jnp.logical_and(col <= row, n >= 1)
    m_idx = jnp.bitwise_and(col, N_META - 1)
    m_head = lax.shift_right_logical(col, 4)
    d_meta_i = n * BLOCK + row - PAD - m_idx
    ok_meta = jnp.logical_and(d_meta_i >= 0, col < 4 * N_META)
    d_meta = jnp.minimum(d_meta_i, WINDOW).astype(F32)

    att = []
    for g in range(2):
        kt = knt[HEAD_DIM * g:HEAD_DIM * (g + 1), :]
        vrep = jnp.concatenate([vv[g], vv[g]], axis=1)
        kb_cur = _diag_blocks([kt] * 4, 4).astype(BF16)
        vb_cur = _row_blocks(vrep, 256).astype(BF16)

        @pl.when(n == 0)
        def _meta():
            tok = lax.broadcasted_iota(jnp.int32, kt.shape, 1)
            ktm = jnp.where(tok >= PAD, kt, 0.0)
            kbm_ref[g] = jnp.concatenate(
                [pltpu.roll(ktm, N_META * (j + 1), axis=1) for j in range(4)], axis=0).astype(BF16)
            z = jnp.zeros((BLOCK - 4 * N_META, 256), F32)
            vbm_ref[g] = jnp.concatenate(
                [_row_blocks(vrep[PAD:BLOCK], 256), z], axis=0).astype(BF16)

        qg = qs[:, 256 * g:256 * (g + 1)]
        s_prev = _dot(qg, kbp_ref[2 * rd + g])
        s_cur = _dot(qg, kb_cur)
        s_meta = _dot(qg, kbm_ref[g])

        slopes = [2.0 ** (-(4 * g + j + 1)) for j in range(4)]
        sl_meta = jnp.where(m_head == 0, slopes[0],
                            jnp.where(m_head == 1, slopes[1],
                                      jnp.where(m_head == 2, slopes[2], slopes[3])))
        sm = jnp.where(ok_meta, s_meta - sl_meta * d_meta, NEG)

        sp, sc, mx = [], [], []
        for j in range(4):
            spj = jnp.where(ok_prev, s_prev[:, 128 * j:128 * (j + 1)] - slopes[j] * d_prev, NEG)
            scj = jnp.where(ok_cur, s_cur[:, 128 * j:128 * (j + 1)] - slopes[j] * d_cur, NEG)
            t = jnp.maximum(jnp.maximum(spj, scj), jnp.where(m_head == j, sm, NEG))
            mj = jnp.maximum(jnp.max(t, axis=-1, keepdims=True), sink_ref[4 * g + j])
            sp.append(spj)
            sc.append(scj)
            mx.append(mj)
        m_meta = jnp.where(m_head == 0, mx[0],
                           jnp.where(m_head == 1, mx[1],
                                     jnp.where(m_head == 2, mx[2], mx[3])))
        pm = jnp.exp(sm - m_meta)
        pp, pc, linv = [], [], []
        for j in range(4):
            ppj = jnp.exp(sp[j] - mx[j])
            pcj = jnp.exp(sc[j] - mx[j])
            lj = jnp.sum(ppj + pcj + jnp.where(m_head == j, pm, 0.0), axis=-1, keepdims=True)
            lj = lj + jnp.exp(sink_ref[4 * g + j] - mx[j])
            pp.append(ppj.astype(BF16))
            pc.append(pcj.astype(BF16))
            linv.append(1.0 / lj)
        o = (_dot(jnp.concatenate(pp, axis=1), vbp_ref[2 * rd + g])
             + _dot(jnp.concatenate(pc, axis=1), vb_cur)
             + _dot(pm.astype(BF16), vbm_ref[g]))
        seg = lax.shift_right_logical(lax.broadcasted_iota(jnp.int32, (BLOCK, 256), 1), 6)
        lfull = jnp.where(seg == 0, linv[0],
                          jnp.where(seg == 1, linv[1],
                                    jnp.where(seg == 2, linv[2], linv[3])))
        att.append(o * lfull)
        kbp_ref[2 * wr + g] = kb_cur
        vbp_ref[2 * wr + g] = vb_cur
    y_att = _rms(jnp.concatenate(att, axis=1), aog_ref[...])

    ca = proj_ref[:, 768:1024]
    cb = proj_ref[:, 1024:1280]
    ubuf_ref[wr, 0:CONV_TAIL, :] = ubuf_ref[rd, BLOCK:BLOCK + CONV_TAIL, :]
    ubuf_ref[wr, CONV_TAIL:CONV_TAIL + BLOCK, :] = ca * _sigmoid(cb)
    base = CONV_TAIL - (CONV_K - 1)
    ub = ubuf_ref[wr]
    nrow = CONV_TAIL + BLOCK
    acc = None
    for r in range(SUBLANES):
        taps = [t for t in range(CONV_K) if (base + t) % SUBLANES == r]
        xr = ub if r == 0 else pltpu.roll(ub, nrow - r, axis=0)
        for t in taps:
            a = base + t - r
            term = dww_ref[t:t + 1, :] * xr[a:a + BLOCK, :]
            acc = term if acc is None else acc + term
    acc = acc + dwb_ref[...]
    mu = jnp.mean(acc, axis=-1, keepdims=True)
    dc = acc - mu
    var = jnp.mean(dc * dc, axis=-1, keepdims=True)
    cn = dc * lax.rsqrt(var + LN_EPS) * lng_ref[...] + lnb_ref[...]
    cs = cn * _sigmoid(cn)
    y_cv = _rms(_dot(cs.astype(BF16), pw_ref[...]), cvg_ref[...])

    rq = proj_ref[:, 1280:1536]
    rk = jnp.where(tok_valid, proj_ref[:, 1536:1792], 0.0)
    rv = proj_ref[:, 1792:2048]
    rg = proj_ref[:, 2048:2304]
    rkt = rk.T
    kb_r = _diag_blocks([rkt[HEAD_DIM * hh:HEAD_DIM * (hh + 1), :] for hh in range(RET_HEADS)],
                        RET_HEADS).astype(BF16)
    vb_r = _row_blocks(rv, 256).astype(BF16)
    scores = _dot(rq.astype(BF16), kb_r) * dec_ref[...]
    state = state_ref[rd]
    y_r = (_dot(scores.astype(BF16), vb_r)
           + _dot((rq * xi_ref[...]).astype(BF16), state.astype(BF16)))
    kv = _dot(rkt.astype(BF16), (rv * zeta_ref[...]).astype(BF16))
    r2 = lax.shift_right_logical(lax.broadcasted_iota(jnp.int32, (256, 256), 0), 6)
    c2 = lax.shift_right_logical(lax.broadcasted_iota(jnp.int32, (256, 256), 1), 6)
    state_ref[wr] = cdec_ref[...] * state + jnp.where(r2 == c2, kv, 0.0)
    mu_r = _segmean(y_r, ones_bd)
    dr = y_r - mu_r
    var_r = _segmean(dr * dr, ones_bd)
    yn = dr * lax.rsqrt(var_r + LN_EPS) * gng_ref[...]
    y_ret = rg * _sigmoid(rg) * yn

    o_ref[...] = jnp.concatenate([y_att, y_cv, y_ret], axis=1).astype(BF16)


def _mixer(proj, sinks, qg, kg, ones_bd, aog, dww, dwb, lng, lnb, pw, cvg,
           dec, zeta, xi, cdec, gng, batch, nblk):
    rows, nin = proj.shape
    d = aog.shape[1] + cvg.shape[1] + gng.shape[1]

    def rowblk(b, n):
        return (b * nblk + n, 0)

    def const(b, n):
        return (0, 0)

    def cspec(a):
        return pl.BlockSpec(a.shape, const)

    return pl.pallas_call(
        _mixer_kernel,
        out_shape=jax.ShapeDtypeStruct((rows, d), BF16),
        grid=(batch, nblk),
        in_specs=[
            pl.BlockSpec(memory_space=pltpu.SMEM),
            pl.BlockSpec((BLOCK, nin), rowblk),
            cspec(qg), cspec(kg), cspec(ones_bd), cspec(aog),
            cspec(dww), cspec(dwb), cspec(lng), cspec(lnb), cspec(pw), cspec(cvg),
            cspec(dec), cspec(zeta), cspec(xi), cspec(cdec), cspec(gng),
        ],
        out_specs=pl.BlockSpec((BLOCK, d), rowblk),
        scratch_shapes=[
            pltpu.VMEM((4, 256, 4 * BLOCK), BF16),
            pltpu.VMEM((4, 4 * BLOCK, 256), BF16),
            pltpu.VMEM((2, 256, BLOCK), BF16),
            pltpu.VMEM((2, BLOCK, 256), BF16),
            pltpu.VMEM((2, CONV_TAIL + BLOCK, 256), F32),
            pltpu.VMEM((2, 256, 256), F32),
        ],
        compiler_params=pltpu.CompilerParams(
            dimension_semantics=("arbitrary", "arbitrary"), vmem_limit_bytes=VMEM_LIMIT),
        name="mixer",
    )(sinks, proj, qg, kg, ones_bd, aog, dww, dwb, lng, lnb, pw, cvg,
      dec, zeta, xi, cdec, gng)


def _ffn_kernel(h_ref, y_ref, wout_ref, g_ref, up_ref, dww_ref, dwb_ref, down_ref, o_ref,
                carry_ref, act_ref, *, tm, dff, cw, pad_rows):
    t = pl.program_id(1)
    nchunk = dff // cw

    wr = lax.rem(t, 2)
    rd = 1 - wr

    @pl.when(t == 0)
    def _init():
        carry_ref[1] = jnp.zeros(carry_ref.shape[1:], F32)

    valid = (t * tm + lax.broadcasted_iota(jnp.int32, (tm, 1), 0)) >= pad_rows
    h = h_ref[...] + jnp.where(valid, _dot(y_ref[...], wout_ref[...]), 0.0)
    u = _rms(h, g_ref[...]).astype(BF16)
    for c in range(nchunk):
        halves = []
        for off in (c * cw, dff + c * cw):
            f = _dot(u, up_ref[:, off:off + cw])
            fe = jnp.concatenate([carry_ref[rd, :, off:off + cw], f], axis=0)
            carry_ref[wr, :, off:off + cw] = f[tm - FFN_TAIL:tm, :]
            fc = dwb_ref[:, off:off + cw] + dww_ref[FFN_CONV_K - 1:FFN_CONV_K, off:off + cw] * f
            for s in range(1, FFN_CONV_K):
                kk = FFN_CONV_K - 1 - s
                shifted = pltpu.roll(fe, s, axis=0)[FFN_TAIL:FFN_TAIL + tm, :]
                fc = fc + dww_ref[kk:kk + 1, off:off + cw] * shifted
            halves.append(fc)
        act_ref[:, c * cw:(c + 1) * cw] = (halves[0] * _sigmoid(halves[0]) * halves[1]).astype(BF16)
    o_ref[...] = h + jnp.where(valid, _dot(act_ref[...], down_ref[...]), 0.0)


def _ffn(h2, ycat, wout, g, up, dww, dwb, down, batch, tiles, tm, cw):
    rows, d = h2.shape
    dff = down.shape[0]

    def rowblk(b, t):
        return (b * tiles + t, 0)

    def const(b, t):
        return (0, 0)

    def wspec(a):
        return pl.BlockSpec(a.shape, const, pipeline_mode=pl.Buffered(1))

    kern = functools.partial(_ffn_kernel, tm=tm, dff=dff, cw=cw, pad_rows=PAD)
    return pl.pallas_call(
        kern,
        out_shape=jax.ShapeDtypeStruct((rows, d), F32),
        grid=(batch, tiles),
        in_specs=[
            pl.BlockSpec((tm, d), rowblk),
            pl.BlockSpec((tm, d), rowblk),
            wspec(wout), wspec(g), wspec(up), wspec(dww), wspec(dwb), wspec(down),
        ],
        out_specs=pl.BlockSpec((tm, d), rowblk),
        scratch_shapes=[
            pltpu.VMEM((2, FFN_TAIL, 2 * dff), F32),
            pltpu.VMEM((tm, dff), BF16),
        ],
        compiler_params=pltpu.CompilerParams(
            dimension_semantics=("arbitrary", "arbitrary"), vmem_limit_bytes=VMEM_LIMIT),
        name="ffn",
    )(h2, ycat, wout, g, up, dww, dwb, down)


def _retention_tables():
    log_gamma = jnp.log1p(-jnp.exp2(-5.0 - jnp.arange(RET_HEADS, dtype=F32)))
    idx = jnp.arange(BLOCK, dtype=F32)
    diff = idx[:, None] - idx[None, :]
    decay = jnp.where(diff[None] >= 0,
                      jnp.exp(jnp.maximum(diff, 0.0)[None] * log_gamma[:, None, None]), 0.0)
    zeta = jnp.exp((BLOCK - 1 - idx)[None, :] * log_gamma[:, None])
    xi = jnp.exp((idx + 1.0)[None, :] * log_gamma[:, None])
    chunk_decay = jnp.exp(BLOCK * log_gamma)
    scale = RET_DK ** -0.5
    dec_t = jnp.transpose(decay, (1, 0, 2)).reshape(BLOCK, RET_HEADS * BLOCK) * scale
    zeta_t = jnp.repeat(zeta.T, RET_DK, axis=1) * scale
    xi_t = jnp.repeat(xi.T, RET_DK, axis=1)
    cdec_t = jnp.repeat(chunk_decay, RET_DK)[None, :]
    return dec_t, zeta_t, xi_t, cdec_t


def kernel(x, meta, norm_mix_g, w_in, q_norm_g, k_norm_g, attn_sinks, attn_out_g, cv_dw_w, cv_dw_b, cv_ln_g, cv_ln_b, cv_pw, cv_out_g, ret_gn_g, w_out, norm_ffn_g, ffn_up, ffn_dw_w, ffn_dw_b, ffn_down):
    batch, seq, d = x.shape
    depth = w_in.shape[0]
    length = seq + BLOCK
    nblk = length // BLOCK
    rows = batch * length
    h = jnp.concatenate([
        jnp.zeros((batch, PAD, d), x.dtype),
        jnp.broadcast_to(meta.astype(x.dtype)[None], (batch, N_META, d)),
        x], axis=1).reshape(rows, d)

    dec_t, zeta_t, xi_t, cdec_t = _retention_tables()
    seg = jnp.arange(256) // HEAD_DIM
    ones_bd = (seg[:, None] == seg[None, :]).astype(BF16)

    ffn_tiles = 4
    ffn_tm = length // ffn_tiles
    for l in range(depth):
        proj = _inproj(h, norm_mix_g[l][None, :], w_in[l].astype(BF16), tm=512)
        ycat = _mixer(
            proj, attn_sinks[l],
            jnp.tile(q_norm_g[l], 8)[None, :], jnp.tile(k_norm_g[l], 2)[None, :], ones_bd,
            attn_out_g[l][None, :], cv_dw_w[l], cv_dw_b[l][None, :], cv_ln_g[l][None, :],
            cv_ln_b[l][None, :], cv_pw[l].astype(BF16), cv_out_g[l][None, :],
            dec_t, zeta_t, xi_t, cdec_t, ret_gn_g[l][None, :], batch, nblk)
        h = _ffn(h, ycat, w_out[l].astype(BF16), norm_ffn_g[l][None, :], ffn_up[l].astype(BF16),
                 ffn_dw_w[l], ffn_dw_b[l][None, :], ffn_down[l].astype(BF16),
                 batch, ffn_tiles, ffn_tm, cw=256)
    return h.reshape(batch, length, d)[:, BLOCK:, :]
```

```python
import functools

import jax
import jax.numpy as jnp
from jax import lax
from jax.experimental import pallas as pl
from jax.experimental.pallas import tpu as pltpu

F32 = jnp.float32
BF16 = jnp.bfloat16

HEAD_DIM = 64
BLOCK = 128
WINDOW = 128
N_META = 16
PAD = BLOCK - N_META
N_Q_HEADS = 8
GQA_GROUP = 4
CONV_K = 31
FFN_CONV_K = 3
RET_DK = 64
RET_HEADS = 4
RMS_EPS = 1e-6
LN_EPS = 1e-5
NEG = -1e30

Q0, K0, V0, CA0, CB0, RQ0, RK0, RV0, RG0, IN_END = 0, 512, 640, 768, 1024, 1280, 1536, 1792, 2048, 2304
PB_Q, PB_V, PB_RQ, PB_RV, PB_RVZ, PB_END = 0, 512, 768, 1024, 1280, 1536

SUBLANES = 8
CONV_TAIL = 32
FFN_TAIL = SUBLANES
VMEM_LIMIT = 56 * 1024 * 1024


def _dot(a, b):
    return jnp.dot(a, b, preferred_element_type=F32)


def _sigmoid(x):
    return 1.0 / (1.0 + jnp.exp(-x))


def _rms(x, g):
    return x * lax.rsqrt(jnp.mean(x * x, axis=-1, keepdims=True) + RMS_EPS) * g


def _segmean(x, ones_bd):
    hi = x.astype(BF16)
    lo = (x - hi.astype(F32)).astype(BF16)
    return (_dot(hi, ones_bd) + _dot(lo, ones_bd)) * (1.0 / HEAD_DIM)


def _inproj_kernel(h_ref, g_ref, w_ref, qg_ref, kg_ref, ones_ref, zeta_ref,
                   pb_ref, pf_ref, pt_ref, p0_ref, p1_ref, *, tm, nblk):
    i = pl.program_id(0)

    @pl.when(i == 0)
    def _init():
        p1_ref[...] = jnp.zeros_like(p1_ref)

    @pl.when(lax.rem(i, 2) == 0)
    def _even():
        _inproj_step(h_ref, g_ref, w_ref, qg_ref, kg_ref, ones_ref, zeta_ref,
                     pb_ref, pf_ref, pt_ref, p0_ref, p1_ref, tm=tm, nblk=nblk)

    @pl.when(lax.rem(i, 2) == 1)
    def _odd():
        _inproj_step(h_ref, g_ref, w_ref, qg_ref, kg_ref, ones_ref, zeta_ref,
                     pb_ref, pf_ref, pt_ref, p1_ref, p0_ref, tm=tm, nblk=nblk)


def _inproj_step(h_ref, g_ref, w_ref, qg_ref, kg_ref, ones_ref, zeta_ref,
                 pb_ref, pf_ref, pt_ref, pw_ref, pr_ref, *, tm, nblk):
    u = _rms(h_ref[...], g_ref[...]).astype(BF16)
    pw_ref[...] = _dot(u, w_ref[...])
    ones_bd = ones_ref[...]
    tile = jnp.maximum(pl.program_id(0) - 1, 0)

    def proj(lo, hi):
        return pr_ref[:, lo:hi]

    for half in range(2):
        lo = Q0 + 256 * half
        q = proj(lo, lo + 256)
        qn = q * lax.rsqrt(_segmean(q * q, ones_bd) + RMS_EPS) * qg_ref[:, 256 * half:256 * (half + 1)]
        pb_ref[:, PB_Q + 256 * half:PB_Q + 256 * (half + 1)] = (qn * (HEAD_DIM ** -0.5)).astype(BF16)

    kv = proj(K0, CA0)
    k = kv[:, 0:128]
    v = kv[:, 128:256]
    kn = k * lax.rsqrt(_segmean(k * k, ones_bd[0:128, 0:128]) + RMS_EPS) * kg_ref[...]
    pt_ref[0:128, :] = kn.T.astype(BF16)
    low = lax.broadcasted_iota(jnp.int32, v.shape, 1) < HEAD_DIM
    v_sw = pltpu.roll(v, HEAD_DIM, axis=1)
    pb_ref[:, PB_V:PB_V + 128] = jnp.where(low, v, v_sw).astype(BF16)
    pb_ref[:, PB_V + 128:PB_V + 256] = jnp.where(low, v_sw, v).astype(BF16)

    cab = proj(CA0, RQ0)
    pf_ref[:, 0:256] = cab[:, 0:256] * _sigmoid(cab[:, 256:512])

    pb_ref[:, PB_RQ:PB_RQ + 256] = proj(RQ0, RK0).astype(BF16)
    row = lax.broadcasted_iota(jnp.int32, (tm, 1), 0)
    sub = lax.shift_right_logical(row, 7)
    pos = [lax.rem(tile * (tm // BLOCK) + r, nblk) for r in range(tm // BLOCK)]
    blk_pos = jnp.full((tm, 1), pos[-1], jnp.int32)
    for r in range(tm // BLOCK - 2, -1, -1):
        blk_pos = jnp.where(sub == r, pos[r], blk_pos)
    pad_row = jnp.logical_and(blk_pos == 0, jnp.bitwise_and(row, BLOCK - 1) < PAD)
    rk = jnp.where(pad_row, 0.0, proj(RK0, RV0))
    pt_ref[128:384, :] = rk.T.astype(BF16)
    rv = proj(RV0, RG0)
    pb_ref[:, PB_RV:PB_RV + 256] = rv.astype(BF16)
    pb_ref[:, PB_RVZ:PB_RVZ + 256] = (rv * zeta_ref[...]).astype(BF16)
    rg = proj(RG0, IN_END)
    pf_ref[:, 256:512] = rg * _sigmoid(rg)


def _inproj(h2, g, w, qg, kg, ones_bd, zeta_tile, tm, nblk):
    rows, d = h2.shape
    ntiles = rows // tm
    kern = functools.partial(_inproj_kernel, tm=tm, nblk=nblk)

    def const(i):
        return (0, 0)

    def cspec(a):
        return pl.BlockSpec(a.shape, const)

    return pl.pallas_call(
        kern,
        out_shape=(jax.ShapeDtypeStruct((rows, PB_END), BF16),
                   jax.ShapeDtypeStruct((rows, 512), F32),
                   jax.ShapeDtypeStruct((384, rows), BF16)),
        grid=(ntiles + 1,),
        in_specs=[
            pl.BlockSpec((tm, d), lambda i: (jnp.minimum(i, ntiles - 1), 0)),
            cspec(g), cspec(w), cspec(qg), cspec(kg), cspec(ones_bd), cspec(zeta_tile),
        ],
        out_specs=(pl.BlockSpec((tm, PB_END), lambda i: (jnp.maximum(i - 1, 0), 0)),
                   pl.BlockSpec((tm, 512), lambda i: (jnp.maximum(i - 1, 0), 0)),
                   pl.BlockSpec((384, tm), lambda i: (0, jnp.maximum(i - 1, 0)))),
        scratch_shapes=[pltpu.VMEM((tm, w.shape[1]), F32)] * 2,
        compiler_params=pltpu.CompilerParams(
            dimension_semantics=("arbitrary",), vmem_limit_bytes=VMEM_LIMIT),
        name="in_proj",
    )(h2, g, w, qg, kg, ones_bd, zeta_tile)


def _row_blocks(x, hm_ref):
    r = x.shape[0]
    return jnp.concatenate([x * hm_ref[j, 0:r, :] for j in range(4)], axis=0)


def _diag_blocks(xt, heads):
    z = jnp.zeros_like(xt[0])
    return jnp.concatenate(
        [jnp.concatenate([xt[j] if i == j else z for i in range(heads)], axis=1)
         for j in range(heads)], axis=0)


def _mixer_kernel(sink_ref, pb_ref, pf_ref, pt_ref, bias_ref, hm_ref, ones_ref, aog_ref,
                  dww_ref, dwb_ref, lng_ref, lnb_ref, pw_ref, cvg_ref,
                  dec_ref, xi_ref, cdec_ref, gng_ref,
                  o_ref,
                  kbp_ref, vbp_ref, kbm_ref, vbm_ref, ubuf_ref, state_ref):
    n = pl.program_id(1)
    wr = lax.rem(n, 2)
    rd = 1 - wr

    @pl.when(n == 0)
    def _init():
        kbp_ref[...] = jnp.zeros_like(kbp_ref)
        vbp_ref[...] = jnp.zeros_like(vbp_ref)
        ubuf_ref[1, BLOCK:BLOCK + CONV_TAIL, :] = jnp.zeros((CONV_TAIL, ubuf_ref.shape[2]), F32)
        state_ref[1] = jnp.zeros(state_ref.shape[1:], F32)
        for g in range(2):
            ktf = pt_ref[HEAD_DIM * g:HEAD_DIM * (g + 1), :].astype(F32)
            tok = lax.broadcasted_iota(jnp.int32, ktf.shape, 1)
            ktm = jnp.where(tok >= PAD, ktf, 0.0)
            kbm_ref[g] = jnp.concatenate(
                [pltpu.roll(ktm, N_META * (j + 1), axis=1) for j in range(4)], axis=0).astype(BF16)
            vm = pb_ref[PAD:BLOCK, PB_V + 128 * g:PB_V + 128 * (g + 1)]
            z = jnp.zeros((BLOCK - 4 * N_META, 256), BF16)
            vbm_ref[g] = jnp.concatenate(
                [_row_blocks(jnp.concatenate([vm, vm], axis=1), hm_ref), z], axis=0)

    ones_bd = ones_ref[...]
    m_head = lax.shift_right_logical(lax.broadcasted_iota(jnp.int32, (BLOCK, BLOCK), 1), 4)

    att = []
    for g in range(2):
        kt = pt_ref[HEAD_DIM * g:HEAD_DIM * (g + 1), :]
        vg = pb_ref[:, PB_V + 128 * g:PB_V + 128 * (g + 1)]
        vrep = jnp.concatenate([vg, vg], axis=1)
        kb_cur = _diag_blocks([kt] * 4, 4)
        vb_cur = _row_blocks(vrep, hm_ref)

        qg = pb_ref[:, PB_Q + 256 * g:PB_Q + 256 * (g + 1)]
        gb = 1152 * g
        s_prev = _dot(qg, kbp_ref[2 * rd + g]) + bias_ref[0, :, gb:gb + 512]
        s_cur = _dot(qg, kb_cur) + bias_ref[0, :, gb + 512:gb + 1024]
        sm = _dot(qg, kbm_ref[g]) + bias_ref[0, :, gb + 1024:gb + 1152]

        mx = []
        for j in range(4):
            t = jnp.maximum(jnp.maximum(s_prev[:, 128 * j:128 * (j + 1)], s_cur[:, 128 * j:128 * (j + 1)]),
                            jnp.where(m_head == j, sm, NEG))
            mx.append(jnp.maximum(jnp.max(t, axis=-1, keepdims=True), sink_ref[4 * g + j]))
        m_meta = jnp.where(m_head == 0, mx[0],
                           jnp.where(m_head == 1, mx[1],
                                     jnp.where(m_head == 2, mx[2], mx[3])))
        pm = jnp.exp(sm - m_meta)
        pp, pc, linv = [], [], []
        for j in range(4):
            ppj = jnp.exp(s_prev[:, 128 * j:128 * (j + 1)] - mx[j])
            pcj = jnp.exp(s_cur[:, 128 * j:128 * (j + 1)] - mx[j])
            lj = jnp.sum(ppj + pcj + jnp.where(m_head == j, pm, 0.0), axis=-1, keepdims=True)
            lj = lj + jnp.exp(sink_ref[4 * g + j] - mx[j])
            pp.append(ppj.astype(BF16))
            pc.append(pcj.astype(BF16))
            linv.append(1.0 / lj)
        o = (_dot(jnp.concatenate(pp, axis=1), vbp_ref[2 * rd + g])
             + _dot(jnp.concatenate(pc, axis=1), vb_cur)
             + _dot(pm.astype(BF16), vbm_ref[g]))
        seg = lax.shift_right_logical(lax.broadcasted_iota(jnp.int32, (BLOCK, 256), 1), 6)
        lfull = jnp.where(seg == 0, linv[0],
                          jnp.where(seg == 1, linv[1],
                                    jnp.where(seg == 2, linv[2], linv[3])))
        att.append(o * lfull)
        kbp_ref[2 * wr + g] = kb_cur
        vbp_ref[2 * wr + g] = vb_cur
    y_att = _rms(jnp.concatenate(att, axis=1), aog_ref[...])

    ubuf_ref[wr, 0:CONV_TAIL, :] = ubuf_ref[rd, BLOCK:BLOCK + CONV_TAIL, :]
    ubuf_ref[wr, CONV_TAIL:CONV_TAIL + BLOCK, :] = pf_ref[:, 0:256]
    base = CONV_TAIL - (CONV_K - 1)
    ub = ubuf_ref[wr]
    nrow = CONV_TAIL + BLOCK
    acc = None
    for r in range(SUBLANES):
        taps = [t for t in range(CONV_K) if (base + t) % SUBLANES == r]
        xr = ub if r == 0 else pltpu.roll(ub, nrow - r, axis=0)
        for t in taps:
            a = base + t - r
            term = dww_ref[t:t + 1, :] * xr[a:a + BLOCK, :]
            acc = term if acc is None else acc + term
    acc = acc + dwb_ref[...]
    mu = jnp.mean(acc, axis=-1, keepdims=True)
    dc = acc - mu
    var = jnp.mean(dc * dc, axis=-1, keepdims=True)
    cn = dc * lax.rsqrt(var + LN_EPS) * lng_ref[...] + lnb_ref[...]
    cs = cn * _sigmoid(cn)
    y_cv = _rms(_dot(cs.astype(BF16), pw_ref[...]), cvg_ref[...])

    rq = pb_ref[:, PB_RQ:PB_RQ + 256]
    rkt = pt_ref[128:384, :]
    kb_r = _diag_blocks([rkt[HEAD_DIM * hh:HEAD_DIM * (hh + 1), :] for hh in range(RET_HEADS)],
                        RET_HEADS)
    vb_r = _row_blocks(pb_ref[:, PB_RV:PB_RV + 256], hm_ref)
    scores = _dot(rq, kb_r) * dec_ref[...]
    state = state_ref[rd]
    y_r = _dot(scores.astype(BF16), vb_r) + _dot(rq, state.astype(BF16)) * xi_ref[...]
    kv = _dot(rkt, pb_ref[:, PB_RVZ:PB_RVZ + 256])
    r2 = lax.shift_right_logical(lax.broadcasted_iota(jnp.int32, (256, 256), 0), 6)
    c2 = lax.shift_right_logical(lax.broadcasted_iota(jnp.int32, (256, 256), 1), 6)
    state_ref[wr] = cdec_ref[...] * state + jnp.where(r2 == c2, kv, 0.0)
    mu_r = _segmean(y_r, ones_bd)
    dr = y_r - mu_r
    var_r = _segmean(dr * dr, ones_bd)
    yn = dr * lax.rsqrt(var_r + LN_EPS) * gng_ref[...]
    y_ret = pf_ref[:, 256:512] * yn

    o_ref[...] = jnp.concatenate([y_att, y_cv, y_ret], axis=1).astype(BF16)


def _mixer(pb, pf, pt, sinks, bias, hm, ones_bd, aog, dww, dwb, lng, lnb, pw, cvg,
           dec, xi, cdec, gng, batch, nblk):
    rows = pb.shape[0]
    d = aog.shape[1] + cvg.shape[1] + gng.shape[1]

    def rowblk(b, n):
        return (b * nblk + n, 0)

    def colblk(b, n):
        return (0, b * nblk + n)

    def const(b, n):
        return (0,) * 2

    def cspec(a):
        return pl.BlockSpec(a.shape, lambda b, n: (0,) * a.ndim)

    return pl.pallas_call(
        _mixer_kernel,
        out_shape=jax.ShapeDtypeStruct((rows, d), BF16),
        grid=(batch, nblk),
        in_specs=[
            pl.BlockSpec(memory_space=pltpu.SMEM),
            pl.BlockSpec((BLOCK, pb.shape[1]), rowblk),
            pl.BlockSpec((BLOCK, pf.shape[1]), rowblk),
            pl.BlockSpec((pt.shape[0], BLOCK), colblk),
            pl.BlockSpec((1,) + bias.shape[1:], lambda b, n: (jnp.minimum(n, 2), 0, 0)),
            cspec(hm), cspec(ones_bd), cspec(aog),
            cspec(dww), cspec(dwb), cspec(lng), cspec(lnb), cspec(pw), cspec(cvg),
            cspec(dec), cspec(xi), cspec(cdec), cspec(gng),
        ],
        out_specs=pl.BlockSpec((BLOCK, d), rowblk),
        scratch_shapes=[
            pltpu.VMEM((4, 256, 4 * BLOCK), BF16),
            pltpu.VMEM((4, 4 * BLOCK, 256), BF16),
            pltpu.VMEM((2, 256, BLOCK), BF16),
            pltpu.VMEM((2, BLOCK, 256), BF16),
            pltpu.VMEM((2, CONV_TAIL + BLOCK, 256), F32),
            pltpu.VMEM((2, 256, 256), F32),
        ],
        compiler_params=pltpu.CompilerParams(
            dimension_semantics=("arbitrary", "arbitrary"), vmem_limit_bytes=VMEM_LIMIT),
        name="mixer",
    )(sinks, pb, pf, pt, bias, hm, ones_bd, aog, dww, dwb, lng, lnb, pw, cvg,
      dec, xi, cdec, gng)


def _ffn_kernel(h_ref, y_ref, wout_ref, g_ref, up_ref, dww_ref, dwb_ref, down_ref, o_ref,
                carry_ref, act_ref, *, tm, dff, cw, pad_rows):
    t = pl.program_id(1)
    nchunk = dff // cw
    wr = lax.rem(t, 2)
    rd = 1 - wr

    @pl.when(t == 0)
    def _init():
        carry_ref[1] = jnp.zeros(carry_ref.shape[1:], F32)

    valid = (t * tm + lax.broadcasted_iota(jnp.int32, (tm, 1), 0)) >= pad_rows
    h = h_ref[...] + jnp.where(valid, _dot(y_ref[...], wout_ref[...]), 0.0)
    u = _rms(h, g_ref[...]).astype(BF16)
    for c in range(nchunk):
        halves = []
        for off in (c * cw, dff + c * cw):
            f = _dot(u, up_ref[:, off:off + cw])
            fe = jnp.concatenate([carry_ref[rd, :, off:off + cw], f], axis=0)
            carry_ref[wr, :, off:off + cw] = f[tm - FFN_TAIL:tm, :]
            fc = dwb_ref[:, off:off + cw] + dww_ref[FFN_CONV_K - 1:FFN_CONV_K, off:off + cw] * f
            for s in range(1, FFN_CONV_K):
                kk = FFN_CONV_K - 1 - s
                shifted = pltpu.roll(fe, s, axis=0)[FFN_TAIL:FFN_TAIL + tm, :]
                fc = fc + dww_ref[kk:kk + 1, off:off + cw] * shifted
            halves.append(fc)
        act_ref[:, c * cw:(c + 1) * cw] = (halves[0] * _sigmoid(halves[0]) * halves[1]).astype(BF16)
    o_ref[...] = h + jnp.where(valid, _dot(act_ref[...], down_ref[...]), 0.0)


def _ffn(h2, ycat, wout, g, up, dww, dwb, down, batch, tiles, tm, cw):
    rows, d = h2.shape
    dff = down.shape[0]

    def rowblk(b, t):
        return (b * tiles + t, 0)

    def const(b, t):
        return (0, 0)

    def wspec(a):
        return pl.BlockSpec(a.shape, const, pipeline_mode=pl.Buffered(1))

    kern = functools.partial(_ffn_kernel, tm=tm, dff=dff, cw=cw, pad_rows=PAD)
    return pl.pallas_call(
        kern,
        out_shape=jax.ShapeDtypeStruct((rows, d), F32),
        grid=(batch, tiles),
        in_specs=[
            pl.BlockSpec((tm, d), rowblk),
            pl.BlockSpec((tm, d), rowblk),
            wspec(wout), wspec(g), wspec(up), wspec(dww), wspec(dwb), wspec(down),
        ],
        out_specs=pl.BlockSpec((tm, d), rowblk),
        scratch_shapes=[
            pltpu.VMEM((2, FFN_TAIL, 2 * dff), F32),
            pltpu.VMEM((tm, dff), BF16),
        ],
        compiler_params=pltpu.CompilerParams(
            dimension_semantics=("arbitrary", "arbitrary"), vmem_limit_bytes=VMEM_LIMIT),
        name="ffn",
    )(h2, ycat, wout, g, up, dww, dwb, down)


def _retention_tables():
    log_gamma = jnp.log1p(-jnp.exp2(-5.0 - jnp.arange(RET_HEADS, dtype=F32)))
    idx = jnp.arange(BLOCK, dtype=F32)
    diff = idx[:, None] - idx[None, :]
    decay = jnp.where(diff[None] >= 0,
                      jnp.exp(jnp.maximum(diff, 0.0)[None] * log_gamma[:, None, None]), 0.0)
    zeta = jnp.exp((BLOCK - 1 - idx)[None, :] * log_gamma[:, None])
    xi = jnp.exp((idx + 1.0)[None, :] * log_gamma[:, None])
    chunk_decay = jnp.exp(BLOCK * log_gamma)
    scale = RET_DK ** -0.5
    dec_t = jnp.transpose(decay, (1, 0, 2)).reshape(BLOCK, RET_HEADS * BLOCK) * scale
    zeta_t = jnp.repeat(zeta.T, RET_DK, axis=1) * scale
    xi_t = jnp.repeat(xi.T, RET_DK, axis=1)
    cdec_t = jnp.repeat(chunk_decay, RET_DK)[None, :]
    return dec_t, zeta_t, xi_t, cdec_t


def _attention_bias():
    slopes = jnp.exp2(-8.0 * jnp.arange(1, N_Q_HEADS + 1, dtype=F32) / N_Q_HEADS)
    i = jnp.arange(BLOCK)[:, None]
    c = jnp.arange(BLOCK)[None, :]
    out = []
    for n in range(3):
        d_prev = (i - c + BLOCK).astype(F32)
        ok_prev = jnp.logical_and(c > i, n >= 2)
        d_cur = (i - c).astype(F32)
        ok_cur = jnp.logical_and(c <= i, n >= 1)
        d_meta_i = n * BLOCK + i - PAD - (c % N_META)
        ok_meta = jnp.logical_and(d_meta_i >= 0, c < GQA_GROUP * N_META)
        d_meta = jnp.minimum(d_meta_i, WINDOW).astype(F32)
        groups = []
        for g in range(N_Q_HEADS // GQA_GROUP):
            sl = slopes[GQA_GROUP * g:GQA_GROUP * (g + 1)]
            prev = [jnp.where(ok_prev, -sl[j] * d_prev, NEG) for j in range(GQA_GROUP)]
            cur = [jnp.where(ok_cur, -sl[j] * d_cur, NEG) for j in range(GQA_GROUP)]
            sl_meta = sl[jnp.minimum(c // N_META, GQA_GROUP - 1)]
            meta = jnp.where(ok_meta, -sl_meta * d_meta, NEG)
            groups.append(jnp.concatenate(prev + cur + [meta], axis=1))
        out.append(jnp.concatenate(groups, axis=1))
    return jnp.stack(out)


def kernel(x, meta, norm_mix_g, w_in, q_norm_g, k_norm_g, attn_sinks, attn_out_g, cv_dw_w, cv_dw_b, cv_ln_g, cv_ln_b, cv_pw, cv_out_g, ret_gn_g, w_out, norm_ffn_g, ffn_up, ffn_dw_w, ffn_dw_b, ffn_down):
    batch, seq, d = x.shape
    depth = w_in.shape[0]
    length = seq + BLOCK
    nblk = length // BLOCK
    rows = batch * length
    h = jnp.concatenate([
        jnp.zeros((batch, PAD, d), x.dtype),
        jnp.broadcast_to(meta.astype(x.dtype)[None], (batch, N_META, d)),
        x], axis=1).reshape(rows, d)

    dec_t, zeta_t, xi_t, cdec_t = _retention_tables()
    bias = _attention_bias()
    seg = jnp.arange(256) // HEAD_DIM
    ones_bd = (seg[:, None] == seg[None, :]).astype(BF16)
    hm = jnp.broadcast_to((seg[None, :] == jnp.arange(4)[:, None])[:, None, :], (4, BLOCK, 256)).astype(BF16)

    in_tm = 512
    zeta_tile = jnp.tile(zeta_t, (in_tm // BLOCK, 1))
    ffn_tiles = 4
    ffn_tm = length // ffn_tiles
    for l in range(depth):
        pb, pf, pt = _inproj(h, norm_mix_g[l][None, :], w_in[l].astype(BF16),
                             jnp.tile(q_norm_g[l], 8)[None, :], jnp.tile(k_norm_g[l], 2)[None, :],
                             ones_bd, zeta_tile, in_tm, nblk)
        ycat = _mixer(
            pb, pf, pt, attn_sinks[l], bias, hm, ones_bd,
            attn_out_g[l][None, :], cv_dw_w[l], cv_dw_b[l][None, :], cv_ln_g[l][None, :],
            cv_ln_b[l][None, :], cv_pw[l].astype(BF16), cv_out_g[l][None, :],
            dec_t, xi_t, cdec_t, ret_gn_g[l][None, :], batch, nblk)
        h = _ffn(h, ycat, w_out[l].astype(BF16), norm_ffn_g[l][None, :], ffn_up[l].astype(BF16),
                 ffn_dw_w[l], ffn_dw_b[l][None, :], ffn_down[l].astype(BF16),
                 batch, ffn_tiles, ffn_tm, cw=256)
    return h.reshape(batch, length, d)[:, BLOCK:, :]
```

```python
import functools

import jax
import jax.numpy as jnp
from jax import lax
from jax.experimental import pallas as pl
from jax.experimental.pallas import tpu as pltpu

F32 = jnp.float32
BF16 = jnp.bfloat16

HEAD_DIM = 64
BLOCK = 128
WINDOW = 128
N_META = 16
PAD = BLOCK - N_META
N_Q_HEADS = 8
GQA_GROUP = 4
CONV_K = 31
FFN_CONV_K = 3
RET_DK = 64
RET_HEADS = 4
RMS_EPS = 1e-6
LN_EPS = 1e-5
NEG = -1e30

Q0, K0, V0, CA0, CB0, RQ0, RK0, RV0, RG0, IN_END = 0, 512, 640, 768, 1024, 1280, 1536, 1792, 2048, 2304
PB_Q, PB_V, PB_RQ, PB_RV, PB_RVZ, PB_END = 0, 512, 768, 1024, 1280, 1536

SUBLANES = 8
CONV_TAIL = 32
FFN_TAIL = SUBLANES
VMEM_LIMIT = 56 * 1024 * 1024


def _dot(a, b):
    return jnp.dot(a, b, preferred_element_type=F32)


def _sigmoid(x):
    return 1.0 / (1.0 + jnp.exp(-x))


def _rms(x, g):
    return x * lax.rsqrt(jnp.mean(x * x, axis=-1, keepdims=True) + RMS_EPS) * g


def _segmean(x, ones_bd):
    hi = x.astype(BF16)
    lo = (x - hi.astype(F32)).astype(BF16)
    return (_dot(hi, ones_bd) + _dot(lo, ones_bd)) * (1.0 / HEAD_DIM)


def _layer_spec(a, l):
    nd = a.ndim - 1
    return pl.BlockSpec((None,) + a.shape[1:], lambda *_: (l,) + (0,) * nd)


def _const_spec(a):
    return pl.BlockSpec(a.shape, lambda *_: (0,) * a.ndim)


def _inproj_kernel(h_ref, g_ref, w_ref, qg_ref, kg_ref, ones_ref, zeta_ref,
                   pb_ref, pf_ref, pt_ref, p0_ref, p1_ref, *, tm, pad_rows):
    i = pl.program_id(0)

    @pl.when(i == 0)
    def _init():
        p1_ref[...] = jnp.zeros_like(p1_ref)

    @pl.when(lax.rem(i, 2) == 0)
    def _even():
        _inproj_step(h_ref, g_ref, w_ref, qg_ref, kg_ref, ones_ref, zeta_ref,
                     pb_ref, pf_ref, pt_ref, p0_ref, p1_ref, tm=tm, pad_rows=pad_rows)

    @pl.when(lax.rem(i, 2) == 1)
    def _odd():
        _inproj_step(h_ref, g_ref, w_ref, qg_ref, kg_ref, ones_ref, zeta_ref,
                     pb_ref, pf_ref, pt_ref, p1_ref, p0_ref, tm=tm, pad_rows=pad_rows)


def _inproj_step(h_ref, g_ref, w_ref, qg_ref, kg_ref, ones_ref, zeta_ref,
                 pb_ref, pf_ref, pt_ref, pw_ref, pr_ref, *, tm, pad_rows):
    u = _rms(h_ref[...], g_ref[...]).astype(BF16)
    pw_ref[...] = _dot(u, w_ref[...])
    ones_bd = ones_ref[...]

    def proj(lo, hi):
        return pr_ref[:, lo:hi]

    for half in range(2):
        lo = Q0 + 256 * half
        q = proj(lo, lo + 256)
        qn = q * lax.rsqrt(_segmean(q * q, ones_bd) + RMS_EPS) * qg_ref[:, 256 * half:256 * (half + 1)]
        pb_ref[:, PB_Q + 256 * half:PB_Q + 256 * (half + 1)] = (qn * (HEAD_DIM ** -0.5)).astype(BF16)

    kv = proj(K0, CA0)
    k = kv[:, 0:128]
    v = kv[:, 128:256]
    kn = k * lax.rsqrt(_segmean(k * k, ones_bd[0:128, 0:128]) + RMS_EPS) * kg_ref[...]
    pt_ref[0:128, :] = kn.T.astype(BF16)
    low = lax.broadcasted_iota(jnp.int32, v.shape, 1) < HEAD_DIM
    v_sw = pltpu.roll(v, HEAD_DIM, axis=1)
    pb_ref[:, PB_V:PB_V + 128] = jnp.where(low, v, v_sw).astype(BF16)
    pb_ref[:, PB_V + 128:PB_V + 256] = jnp.where(low, v_sw, v).astype(BF16)

    cab = proj(CA0, RQ0)
    pf_ref[:, 0:256] = cab[:, 0:256] * _sigmoid(cab[:, 256:512])

    pb_ref[:, PB_RQ:PB_RQ + 256] = proj(RQ0, RK0).astype(BF16)
    rk = proj(RK0, RV0)
    if pad_rows:
        row = lax.broadcasted_iota(jnp.int32, (tm, 1), 0)
        rk = jnp.where(row < pad_rows, 0.0, rk)
    pt_ref[128:384, :] = rk.T.astype(BF16)
    rv = proj(RV0, RG0)
    pb_ref[:, PB_RV:PB_RV + 256] = rv.astype(BF16)
    pb_ref[:, PB_RVZ:PB_RVZ + 256] = (rv * zeta_ref[...]).astype(BF16)
    rg = proj(RG0, IN_END)
    pf_ref[:, 256:512] = rg * _sigmoid(rg)


def _inproj(h2, l, g, w, qg, kg, ones_bd, zeta_tile, tm, pad_rows):
    rows, d = h2.shape
    ntiles = rows // tm
    nout = w.shape[2]
    kern = functools.partial(_inproj_kernel, tm=tm, pad_rows=pad_rows)
    return pl.pallas_call(
        kern,
        out_shape=(jax.ShapeDtypeStruct((rows, PB_END), BF16),
                   jax.ShapeDtypeStruct((rows, 512), F32),
                   jax.ShapeDtypeStruct((384, rows), BF16)),
        grid=(ntiles + 1,),
        in_specs=[
            pl.BlockSpec((tm, d), lambda i: (jnp.minimum(i, ntiles - 1), 0)),
            _layer_spec(g, l), _layer_spec(w, l), _layer_spec(qg, l), _layer_spec(kg, l),
            _const_spec(ones_bd), _const_spec(zeta_tile),
        ],
        out_specs=(pl.BlockSpec((tm, PB_END), lambda i: (jnp.maximum(i - 1, 0), 0)),
                   pl.BlockSpec((tm, 512), lambda i: (jnp.maximum(i - 1, 0), 0)),
                   pl.BlockSpec((384, tm), lambda i: (0, jnp.maximum(i - 1, 0)))),
        scratch_shapes=[pltpu.VMEM((tm, nout), F32)] * 2,
        compiler_params=pltpu.CompilerParams(
            dimension_semantics=("arbitrary",), vmem_limit_bytes=VMEM_LIMIT),
        name="in_proj",
    )(h2, g, w, qg, kg, ones_bd, zeta_tile)


def _row_blocks(x, hm_ref):
    r = x.shape[0]
    return jnp.concatenate([x * hm_ref[j, 0:r, :] for j in range(4)], axis=0)


def _diag_blocks(xt, heads):
    z = jnp.zeros_like(xt[0])
    return jnp.concatenate(
        [jnp.concatenate([xt[j] if i == j else z for i in range(heads)], axis=1)
         for j in range(heads)], axis=0)


def _mixer_kernel(sink_ref, pb_ref, pf_ref, pt_ref, bias_ref, hm_ref, ones_ref, aog_ref,
                  dww_ref, dwb_ref, lng_ref, lnb_ref, pw_ref, cvg_ref,
                  dec_ref, xi_ref, cdec_ref, gng_ref,
                  kbm_in_ref, vbm_in_ref, tail_in_ref, state_in_ref,
                  o_ref, kbm_out_ref, vbm_out_ref, tail_out_ref, state_out_ref,
                  kbp_ref, vbp_ref, kbm_ref, vbm_ref, ubuf_ref, state_ref, *, layer, first_block):
    n = pl.program_id(1)

    @pl.when(n == 0)
    def _init():
        kbp_ref[...] = jnp.zeros_like(kbp_ref)
        vbp_ref[...] = jnp.zeros_like(vbp_ref)
        if first_block:
            ubuf_ref[1, BLOCK:BLOCK + CONV_TAIL, :] = jnp.zeros((CONV_TAIL, ubuf_ref.shape[2]), F32)
            state_ref[1] = jnp.zeros(state_ref.shape[1:], F32)
            for g in range(2):
                ktf = pt_ref[HEAD_DIM * g:HEAD_DIM * (g + 1), :].astype(F32)
                tok = lax.broadcasted_iota(jnp.int32, ktf.shape, 1)
                ktm = jnp.where(tok >= PAD, ktf, 0.0)
                kbm_ref[g] = jnp.concatenate(
                    [pltpu.roll(ktm, N_META * (j + 1), axis=1) for j in range(4)], axis=0).astype(BF16)
                vm = pb_ref[PAD:BLOCK, PB_V + 128 * g:PB_V + 128 * (g + 1)]
                z = jnp.zeros((BLOCK - 4 * N_META, 256), BF16)
                vbm_ref[g] = jnp.concatenate(
                    [_row_blocks(jnp.concatenate([vm, vm], axis=1), hm_ref), z], axis=0)
        else:
            ubuf_ref[1, BLOCK:BLOCK + CONV_TAIL, :] = tail_in_ref[...]
            state_ref[1] = state_in_ref[...]
            kbm_ref[...] = kbm_in_ref[...]
            vbm_ref[...] = vbm_in_ref[...]

    for parity in range(2):
        @pl.when(lax.rem(n, 2) == parity)
        def _step():
            _mixer_step(sink_ref, pb_ref, pf_ref, pt_ref, bias_ref, hm_ref, ones_ref, aog_ref,
                        dww_ref, dwb_ref, lng_ref, lnb_ref, pw_ref, cvg_ref,
                        dec_ref, xi_ref, cdec_ref, gng_ref,
                        o_ref, kbm_out_ref, vbm_out_ref, tail_out_ref, state_out_ref,
                        kbp_ref, vbp_ref, kbm_ref, vbm_ref, ubuf_ref, state_ref,
                        layer=layer, wr=parity, rd=1 - parity)


def _mixer_step(sink_ref, pb_ref, pf_ref, pt_ref, bias_ref, hm_ref, ones_ref, aog_ref,
                dww_ref, dwb_ref, lng_ref, lnb_ref, pw_ref, cvg_ref,
                dec_ref, xi_ref, cdec_ref, gng_ref,
                o_ref, kbm_out_ref, vbm_out_ref, tail_out_ref, state_out_ref,
                kbp_ref, vbp_ref, kbm_ref, vbm_ref, ubuf_ref, state_ref, *, layer, wr, rd):
    n = pl.program_id(1)
    ones_bd = ones_ref[...]
    m_head = lax.shift_right_logical(lax.broadcasted_iota(jnp.int32, (BLOCK, BLOCK), 1), 4)

    att = []
    for g in range(2):
        kt = pt_ref[HEAD_DIM * g:HEAD_DIM * (g + 1), :]
        vg = pb_ref[:, PB_V + 128 * g:PB_V + 128 * (g + 1)]
        vrep = jnp.concatenate([vg, vg], axis=1)
        kb_cur = _diag_blocks([kt] * 4, 4)
        vb_cur = _row_blocks(vrep, hm_ref)

        qg = pb_ref[:, PB_Q + 256 * g:PB_Q + 256 * (g + 1)]
        gb = 1152 * g
        s_prev = _dot(qg, kbp_ref[2 * rd + g]) + bias_ref[0, :, gb:gb + 512]
        s_cur = _dot(qg, kb_cur) + bias_ref[0, :, gb + 512:gb + 1024]
        sm = _dot(qg, kbm_ref[g]) + bias_ref[0, :, gb + 1024:gb + 1152]

        mx = []
        for j in range(4):
            t = jnp.maximum(jnp.maximum(s_prev[:, 128 * j:128 * (j + 1)], s_cur[:, 128 * j:128 * (j + 1)]),
                            jnp.where(m_head == j, sm, NEG))
            mx.append(jnp.maximum(jnp.max(t, axis=-1, keepdims=True), sink_ref[layer, 4 * g + j]))
        m_meta = jnp.where(m_head == 0, mx[0],
                           jnp.where(m_head == 1, mx[1],
                                     jnp.where(m_head == 2, mx[2], mx[3])))
        pm = jnp.exp(sm - m_meta)
        pp, pc, linv = [], [], []
        for j in range(4):
            ppj = jnp.exp(s_prev[:, 128 * j:128 * (j + 1)] - mx[j])
            pcj = jnp.exp(s_cur[:, 128 * j:128 * (j + 1)] - mx[j])
            lj = jnp.sum(ppj + pcj + jnp.where(m_head == j, pm, 0.0), axis=-1, keepdims=True)
            lj = lj + jnp.exp(sink_ref[layer, 4 * g + j] - mx[j])
            pp.append(ppj.astype(BF16))
            pc.append(pcj.astype(BF16))
            linv.append(1.0 / lj)
        o = (_dot(jnp.concatenate(pp, axis=1), vbp_ref[2 * rd + g])
             + _dot(jnp.concatenate(pc, axis=1), vb_cur)
             + _dot(pm.astype(BF16), vbm_ref[g]))
        seg = lax.shift_right_logical(lax.broadcasted_iota(jnp.int32, (BLOCK, 256), 1), 6)
        lfull = jnp.where(seg == 0, linv[0],
                          jnp.where(seg == 1, linv[1],
                                    jnp.where(seg == 2, linv[2], linv[3])))
        att.append(o * lfull)
        kbp_ref[2 * wr + g] = kb_cur
        vbp_ref[2 * wr + g] = vb_cur
    y_att = _rms(jnp.concatenate(att, axis=1), aog_ref[...])

    ubuf_ref[wr, 0:CONV_TAIL, :] = ubuf_ref[rd, BLOCK:BLOCK + CONV_TAIL, :]
    ubuf_ref[wr, CONV_TAIL:CONV_TAIL + BLOCK, :] = pf_ref[:, 0:256]
    base = CONV_TAIL - (CONV_K - 1)
    ub = ubuf_ref[wr]
    nrow = CONV_TAIL + BLOCK
    acc = None
    for r in range(SUBLANES):
        taps = [t for t in range(CONV_K) if (base + t) % SUBLANES == r]
        xr = ub if r == 0 else pltpu.roll(ub, nrow - r, axis=0)
        for t in taps:
            a = base + t - r
            term = dww_ref[t:t + 1, :] * xr[a:a + BLOCK, :]
            acc = term if acc is None else acc + term
    acc = acc + dwb_ref[...]
    mu = jnp.mean(acc, axis=-1, keepdims=True)
    dc = acc - mu
    var = jnp.mean(dc * dc, axis=-1, keepdims=True)
    cn = dc * lax.rsqrt(var + LN_EPS) * lng_ref[...] + lnb_ref[...]
    cs = cn * _sigmoid(cn)
    y_cv = _rms(_dot(cs.astype(BF16), pw_ref[...]), cvg_ref[...])

    rq = pb_ref[:, PB_RQ:PB_RQ + 256]
    rkt = pt_ref[128:384, :]
    kb_r = _diag_blocks([rkt[HEAD_DIM * hh:HEAD_DIM * (hh + 1), :] for hh in range(RET_HEADS)],
                        RET_HEADS)
    vb_r = _row_blocks(pb_ref[:, PB_RV:PB_RV + 256], hm_ref)
    scores = _dot(rq, kb_r) * dec_ref[...]
    state = state_ref[rd]
    y_r = _dot(scores.astype(BF16), vb_r) + _dot(rq, state.astype(BF16)) * xi_ref[...]
    kv = _dot(rkt, pb_ref[:, PB_RVZ:PB_RVZ + 256])
    r2 = lax.shift_right_logical(lax.broadcasted_iota(jnp.int32, (256, 256), 0), 6)
    c2 = lax.shift_right_logical(lax.broadcasted_iota(jnp.int32, (256, 256), 1), 6)
    state_ref[wr] = cdec_ref[...] * state + jnp.where(r2 == c2, kv, 0.0)
    mu_r = _segmean(y_r, ones_bd)
    dr = y_r - mu_r
    var_r = _segmean(dr * dr, ones_bd)
    yn = dr * lax.rsqrt(var_r + LN_EPS) * gng_ref[...]
    y_ret = pf_ref[:, 256:512] * yn

    o_ref[...] = jnp.concatenate([y_att, y_cv, y_ret], axis=1).astype(BF16)

    @pl.when(n == pl.num_programs(1) - 1)
    def _hand_over():
        kbm_out_ref[...] = kbm_ref[...]
        vbm_out_ref[...] = vbm_ref[...]
        tail_out_ref[...] = ubuf_ref[wr, BLOCK:BLOCK + CONV_TAIL, :]
        state_out_ref[...] = state_ref[wr]


def _mixer(pb, pf, pt, l, sinks, bias, hm, ones_bd, aog, dww, dwb, lng, lnb, pw, cvg,
           dec, xi, cdec, gng, state_in, batch, nblk, first_block):
    rows = pb.shape[0]
    d = aog.shape[2] + cvg.shape[2] + gng.shape[2]
    pos0 = 0 if first_block else 1

    def rowblk(b, n):
        return (b * nblk + n, 0)

    def colblk(b, n):
        return (0, b * nblk + n)

    def lspec(a):
        return _layer_spec(a, l)

    kern = functools.partial(_mixer_kernel, layer=l, first_block=first_block)
    state_shapes = tuple(jax.ShapeDtypeStruct(a.shape, a.dtype) for a in state_in)
    return pl.pallas_call(
        kern,
        out_shape=(jax.ShapeDtypeStruct((rows, d), BF16),) + state_shapes,
        grid=(batch, nblk),
        in_specs=[
            pl.BlockSpec(memory_space=pltpu.SMEM),
            pl.BlockSpec((BLOCK, pb.shape[1]), rowblk),
            pl.BlockSpec((BLOCK, pf.shape[1]), rowblk),
            pl.BlockSpec((pt.shape[0], BLOCK), colblk),
            pl.BlockSpec((1,) + bias.shape[1:], lambda b, n: (jnp.minimum(n + pos0, 2), 0, 0)),
            _const_spec(hm), _const_spec(ones_bd), lspec(aog),
            lspec(dww), lspec(dwb), lspec(lng), lspec(lnb), lspec(pw), lspec(cvg),
            _const_spec(dec), _const_spec(xi), _const_spec(cdec), lspec(gng),
        ] + [_const_spec(a) for a in state_in],
        out_specs=(pl.BlockSpec((BLOCK, d), rowblk),) + tuple(_const_spec(a) for a in state_in),
        scratch_shapes=[
            pltpu.VMEM((4, 256, 4 * BLOCK), BF16),
            pltpu.VMEM((4, 4 * BLOCK, 256), BF16),
            pltpu.VMEM((2, 256, BLOCK), BF16),
            pltpu.VMEM((2, BLOCK, 256), BF16),
            pltpu.VMEM((2, CONV_TAIL + BLOCK, 256), F32),
            pltpu.VMEM((2, 256, 256), F32),
        ],
        compiler_params=pltpu.CompilerParams(
            dimension_semantics=("arbitrary", "arbitrary"), vmem_limit_bytes=VMEM_LIMIT),
        name="mixer",
    )(sinks, pb, pf, pt, bias, hm, ones_bd, aog, dww, dwb, lng, lnb, pw, cvg,
      dec, xi, cdec, gng, *state_in)


def _mixer_state_zeros():
    return (jnp.zeros((2, 256, BLOCK), BF16), jnp.zeros((2, BLOCK, 256), BF16),
            jnp.zeros((CONV_TAIL, 256), F32), jnp.zeros((256, 256), F32))


def _ffn_kernel(h_ref, y_ref, wout_ref, g_ref, up_ref, dww_ref, dwb_ref, down_ref, carry_in_ref,
                o_ref, carry_out_ref, carry_ref, act_ref, *, tm, dff, cw, pad_rows):
    t = pl.program_id(1)
    nchunk = dff // cw
    wr = lax.rem(t, 2)
    rd = 1 - wr

    @pl.when(t == 0)
    def _init():
        carry_ref[1] = carry_in_ref[...]

    def keep(y):
        if not pad_rows:
            return y
        row = t * tm + lax.broadcasted_iota(jnp.int32, (tm, 1), 0)
        return jnp.where(row >= pad_rows, y, 0.0)

    h = h_ref[...] + keep(_dot(y_ref[...], wout_ref[...]))
    u = _rms(h, g_ref[...]).astype(BF16)
    for c in range(nchunk):
        halves = []
        for off in (c * cw, dff + c * cw):
            f = _dot(u, up_ref[:, off:off + cw])
            fe = jnp.concatenate([carry_ref[rd, :, off:off + cw], f], axis=0)
            carry_ref[wr, :, off:off + cw] = f[tm - FFN_TAIL:tm, :]
            fc = dwb_ref[:, off:off + cw] + dww_ref[FFN_CONV_K - 1:FFN_CONV_K, off:off + cw] * f
            for s in range(1, FFN_CONV_K):
                kk = FFN_CONV_K - 1 - s
                shifted = pltpu.roll(fe, s, axis=0)[FFN_TAIL:FFN_TAIL + tm, :]
                fc = fc + dww_ref[kk:kk + 1, off:off + cw] * shifted
            halves.append(fc)
        act_ref[:, c * cw:(c + 1) * cw] = (halves[0] * _sigmoid(halves[0]) * halves[1]).astype(BF16)
    o_ref[...] = h + keep(_dot(act_ref[...], down_ref[...]))

    @pl.when(t == pl.num_programs(1) - 1)
    def _hand_over():
        carry_out_ref[...] = carry_ref[wr]


def _ffn(h2, ycat, l, wout, g, up, dww, dwb, down, carry_in, batch, tiles, tm, cw, pad_rows):
    rows, d = h2.shape
    dff = down.shape[1]

    def rowblk(b, t):
        return (b * tiles + t, 0)

    def wspec(a):
        nd = a.ndim - 1
        return pl.BlockSpec((None,) + a.shape[1:], lambda b, t: (l,) + (0,) * nd,
                            pipeline_mode=pl.Buffered(1))

    kern = functools.partial(_ffn_kernel, tm=tm, dff=dff, cw=cw, pad_rows=pad_rows)
    return pl.pallas_call(
        kern,
        out_shape=(jax.ShapeDtypeStruct((rows, d), F32),
                   jax.ShapeDtypeStruct(carry_in.shape, F32)),
        grid=(batch, tiles),
        in_specs=[
            pl.BlockSpec((tm, d), rowblk),
            pl.BlockSpec((tm, d), rowblk),
            wspec(wout), wspec(g), wspec(up), wspec(dww), wspec(dwb), wspec(down),
            _const_spec(carry_in),
        ],
        out_specs=(pl.BlockSpec((tm, d), rowblk), _const_spec(carry_in)),
        scratch_shapes=[
            pltpu.VMEM((2, FFN_TAIL, 2 * dff), F32),
            pltpu.VMEM((tm, dff), BF16),
        ],
        compiler_params=pltpu.CompilerParams(
            dimension_semantics=("arbitrary", "arbitrary"), vmem_limit_bytes=VMEM_LIMIT),
        name="ffn",
    )(h2, ycat, wout, g, up, dww, dwb, down, carry_in)


def _retention_tables():
    log_gamma = jnp.log1p(-jnp.exp2(-5.0 - jnp.arange(RET_HEADS, dtype=F32)))
    idx = jnp.arange(BLOCK, dtype=F32)
    diff = idx[:, None] - idx[None, :]
    decay = jnp.where(diff[None] >= 0,
                      jnp.exp(jnp.maximum(diff, 0.0)[None] * log_gamma[:, None, None]), 0.0)
    zeta = jnp.exp((BLOCK - 1 - idx)[None, :] * log_gamma[:, None])
    xi = jnp.exp((idx + 1.0)[None, :] * log_gamma[:, None])
    chunk_decay = jnp.exp(BLOCK * log_gamma)
    scale = RET_DK ** -0.5
    dec_t = jnp.transpose(decay, (1, 0, 2)).reshape(BLOCK, RET_HEADS * BLOCK) * scale
    zeta_t = jnp.repeat(zeta.T, RET_DK, axis=1) * scale
    xi_t = jnp.repeat(xi.T, RET_DK, axis=1)
    cdec_t = jnp.repeat(chunk_decay, RET_DK)[None, :]
    return dec_t, zeta_t, xi_t, cdec_t


def _attention_bias():
    slopes = jnp.exp2(-8.0 * jnp.arange(1, N_Q_HEADS + 1, dtype=F32) / N_Q_HEADS)
    i = jnp.arange(BLOCK)[:, None]
    c = jnp.arange(BLOCK)[None, :]
    out = []
    for n in range(3):
        d_prev = (i - c + BLOCK).astype(F32)
        ok_prev = jnp.logical_and(c > i, n >= 2)
        d_cur = (i - c).astype(F32)
        ok_cur = jnp.logical_and(c <= i, n >= 1)
        d_meta_i = n * BLOCK + i - PAD - (c % N_META)
        ok_meta = jnp.logical_and(d_meta_i >= 0, c < GQA_GROUP * N_META)
        d_meta = jnp.minimum(d_meta_i, WINDOW).astype(F32)
        groups = []
        for g in range(N_Q_HEADS // GQA_GROUP):
            sl = slopes[GQA_GROUP * g:GQA_GROUP * (g + 1)]
            prev = [jnp.where(ok_prev, -sl[j] * d_prev, NEG) for j in range(GQA_GROUP)]
            cur = [jnp.where(ok_cur, -sl[j] * d_cur, NEG) for j in range(GQA_GROUP)]
            sl_meta = sl[jnp.minimum(c // N_META, GQA_GROUP - 1)]
            meta = jnp.where(ok_meta, -sl_meta * d_meta, NEG)
            groups.append(jnp.concatenate(prev + cur + [meta], axis=1))
        out.append(jnp.concatenate(groups, axis=1))
    return jnp.stack(out)


def kernel(x, meta, norm_mix_g, w_in, q_norm_g, k_norm_g, attn_sinks, attn_out_g, cv_dw_w, cv_dw_b, cv_ln_g, cv_ln_b, cv_pw, cv_out_g, ret_gn_g, w_out, norm_ffn_g, ffn_up, ffn_dw_w, ffn_dw_b, ffn_down):
    batch, seq, d = x.shape
    depth = w_in.shape[0]
    dff = ffn_down.shape[1]

    dec_t, zeta_t, xi_t, cdec_t = _retention_tables()
    bias = _attention_bias()
    seg = jnp.arange(256) // HEAD_DIM
    ones_bd = (seg[:, None] == seg[None, :]).astype(BF16)
    hm = jnp.broadcast_to((seg[None, :] == jnp.arange(4)[:, None])[:, None, :], (4, BLOCK, 256)).astype(BF16)

    def vec(a):
        return a.reshape(depth, 1, a.shape[-1])

    w_in_b, w_out_b, up_b, down_b, pw_b = (a.astype(BF16) for a in (w_in, w_out, ffn_up, ffn_down, cv_pw))
    mix_g, ffn_g = vec(norm_mix_g), vec(norm_ffn_g)
    qg, kg = vec(jnp.tile(q_norm_g, (1, N_Q_HEADS))), vec(jnp.tile(k_norm_g, (1, N_Q_HEADS // GQA_GROUP)))
    aog, dwb, lng, lnb, cvg, gng, fdwb = (vec(a) for a in (
        attn_out_g, cv_dw_b, cv_ln_g, cv_ln_b, cv_out_g, ret_gn_g, ffn_dw_b))

    h0 = jnp.concatenate([jnp.zeros((PAD, d), x.dtype), meta.astype(x.dtype)], axis=0)
    hr = x.reshape(batch * seq, d)
    nblk = seq // BLOCK
    in_tm = 512
    ffn_tm = 512
    zeta_tile = jnp.tile(zeta_t, (in_tm // BLOCK, 1))

    def layer(h2, l, state_in, carry_in, nb, blocks, tm_in, tm_ffn, zt, first):
        pad_rows = PAD if first else 0
        pb, pf, pt = _inproj(h2, l, mix_g, w_in_b, qg, kg, ones_bd, zt, tm_in, pad_rows)
        ycat, *state = _mixer(pb, pf, pt, l, attn_sinks, bias, hm, ones_bd, aog, cv_dw_w, dwb, lng, lnb,
                              pw_b, cvg, dec_t, xi_t, cdec_t, gng, state_in, nb, blocks, first)
        h2, carry = _ffn(h2, ycat, l, w_out_b, ffn_g, up_b, ffn_dw_w, fdwb, down_b, carry_in,
                         nb, blocks * BLOCK // tm_ffn, tm_ffn, 256, pad_rows)
        return h2, tuple(state), carry

    for l in range(depth):
        h0, state0, carry0 = layer(h0, l, _mixer_state_zeros(), jnp.zeros((FFN_TAIL, 2 * dff), F32),
                                   1, 1, BLOCK, BLOCK, zeta_t, True)
        hr, _, _ = layer(hr, l, state0, carry0, batch, nblk, in_tm, ffn_tm, zeta_tile, False)
    return hr.reshape(batch, seq, d)
```

```python
import functools

import jax
import jax.numpy as jnp
from jax import lax
from jax.experimental import pallas as pl
from jax.experimental.pallas import tpu as pltpu

F32 = jnp.float32
BF16 = jnp.bfloat16

HEAD_DIM = 64
BLOCK = 128
WINDOW = 128
N_META = 16
PAD = BLOCK - N_META
N_Q_HEADS = 8
GQA_GROUP = 4
CONV_K = 31
FFN_CONV_K = 3
RET_DK = 64
RET_HEADS = 4
RMS_EPS = 1e-6
LN_EPS = 1e-5
NEG = -1e30

Q0, K0, V0, CA0, CB0, RQ0, RK0, RV0, RG0, IN_END = 0, 512, 640, 768, 1024, 1280, 1536, 1792, 2048, 2304
PB_Q, PB_V, PB_RQ, PB_RV, PB_RVZ, PB_END = 0, 512, 768, 1024, 1280, 1536

SUBLANES = 8
CONV_TAIL = 32
FFN_TAIL = SUBLANES
VMEM_LIMIT = 56 * 1024 * 1024


def _dot(a, b):
    return jnp.dot(a, b, preferred_element_type=F32)


def _sigmoid(x):
    return 1.0 / (1.0 + jnp.exp(-x))


def _rms(x, g):
    return x * lax.rsqrt(jnp.mean(x * x, axis=-1, keepdims=True) + RMS_EPS) * g


def _head_mean(x):
    low = lax.broadcasted_iota(jnp.int32, (x.shape[0], 2 * HEAD_DIM), 1) < HEAD_DIM
    cols = []
    for c in range(x.shape[1] // (2 * HEAD_DIM)):
        xs = x[:, 2 * HEAD_DIM * c:2 * HEAD_DIM * (c + 1)]
        lo = jnp.where(low, xs, 0.0)
        s_lo = jnp.sum(lo, axis=-1, keepdims=True)
        s_hi = jnp.sum(xs - lo, axis=-1, keepdims=True)
        cols.append(jnp.where(low, s_lo, s_hi))
    return jnp.concatenate(cols, axis=1) * (1.0 / HEAD_DIM)


def _layer_spec(a, l):
    nd = a.ndim - 1
    return pl.BlockSpec((None,) + a.shape[1:], lambda *_: (l,) + (0,) * nd)


def _const_spec(a):
    return pl.BlockSpec(a.shape, lambda *_: (0,) * a.ndim)


def _inproj_kernel(h_ref, g_ref, w_ref, qg_ref, kg_ref, zeta_ref,
                   pb_ref, pf_ref, pt_ref, p0_ref, p1_ref, *, tm, pad_rows):
    i = pl.program_id(0)

    @pl.when(i == 0)
    def _init():
        p1_ref[...] = jnp.zeros_like(p1_ref)

    @pl.when(lax.rem(i, 2) == 0)
    def _even():
        _inproj_step(h_ref, g_ref, w_ref, qg_ref, kg_ref, zeta_ref,
                     pb_ref, pf_ref, pt_ref, p0_ref, p1_ref, tm=tm, pad_rows=pad_rows)

    @pl.when(lax.rem(i, 2) == 1)
    def _odd():
        _inproj_step(h_ref, g_ref, w_ref, qg_ref, kg_ref, zeta_ref,
                     pb_ref, pf_ref, pt_ref, p1_ref, p0_ref, tm=tm, pad_rows=pad_rows)


def _inproj_step(h_ref, g_ref, w_ref, qg_ref, kg_ref, zeta_ref,
                 pb_ref, pf_ref, pt_ref, pw_ref, pr_ref, *, tm, pad_rows):
    u = _rms(h_ref[...], g_ref[...]).astype(BF16)
    pw_ref[...] = _dot(u, w_ref[...])

    def proj(lo, hi):
        return pr_ref[:, lo:hi]

    for half in range(2):
        lo = Q0 + 256 * half
        q = proj(lo, lo + 256)
        qn = q * lax.rsqrt(_head_mean(q * q) + RMS_EPS) * qg_ref[:, 256 * half:256 * (half + 1)]
        pb_ref[:, PB_Q + 256 * half:PB_Q + 256 * (half + 1)] = (qn * (HEAD_DIM ** -0.5)).astype(BF16)

    kv = proj(K0, CA0)
    k = kv[:, 0:128]
    v = kv[:, 128:256]
    kn = k * lax.rsqrt(_head_mean(k * k) + RMS_EPS) * kg_ref[...]
    pt_ref[0:128, :] = kn.T.astype(BF16)
    low = lax.broadcasted_iota(jnp.int32, v.shape, 1) < HEAD_DIM
    v_sw = pltpu.roll(v, HEAD_DIM, axis=1)
    pb_ref[:, PB_V:PB_V + 128] = jnp.where(low, v, v_sw).astype(BF16)
    pb_ref[:, PB_V + 128:PB_V + 256] = jnp.where(low, v_sw, v).astype(BF16)

    cab = proj(CA0, RQ0)
    pf_ref[:, 0:256] = cab[:, 0:256] * _sigmoid(cab[:, 256:512])

    pb_ref[:, PB_RQ:PB_RQ + 256] = proj(RQ0, RK0).astype(BF16)
    rk = proj(RK0, RV0)
    if pad_rows:
        row = lax.broadcasted_iota(jnp.int32, (tm, 1), 0)
        rk = jnp.where(row < pad_rows, 0.0, rk)
    pt_ref[128:384, :] = rk.T.astype(BF16)
    rv = proj(RV0, RG0)
    pb_ref[:, PB_RV:PB_RV + 256] = rv.astype(BF16)
    pb_ref[:, PB_RVZ:PB_RVZ + 256] = (rv * zeta_ref[...]).astype(BF16)
    rg = proj(RG0, IN_END)
    pf_ref[:, 256:512] = rg * _sigmoid(rg)


def _inproj(h2, l, g, w, qg, kg, zeta_tile, tm, pad_rows):
    rows, d = h2.shape
    ntiles = rows // tm
    nout = w.shape[2]
    kern = functools.partial(_inproj_kernel, tm=tm, pad_rows=pad_rows)
    return pl.pallas_call(
        kern,
        out_shape=(jax.ShapeDtypeStruct((rows, PB_END), BF16),
                   jax.ShapeDtypeStruct((rows, 512), F32),
                   jax.ShapeDtypeStruct((384, rows), BF16)),
        grid=(ntiles + 1,),
        in_specs=[
            pl.BlockSpec((tm, d), lambda i: (jnp.minimum(i, ntiles - 1), 0)),
            _layer_spec(g, l), _layer_spec(w, l), _layer_spec(qg, l), _layer_spec(kg, l),
            _const_spec(zeta_tile),
        ],
        out_specs=(pl.BlockSpec((tm, PB_END), lambda i: (jnp.maximum(i - 1, 0), 0)),
                   pl.BlockSpec((tm, 512), lambda i: (jnp.maximum(i - 1, 0), 0)),
                   pl.BlockSpec((384, tm), lambda i: (0, jnp.maximum(i - 1, 0)))),
        scratch_shapes=[pltpu.VMEM((tm, nout), F32)] * 2,
        compiler_params=pltpu.CompilerParams(
            dimension_semantics=("arbitrary",), vmem_limit_bytes=VMEM_LIMIT),
        name="in_proj",
    )(h2, g, w, qg, kg, zeta_tile)


def _row_blocks(x, hm_ref):
    r = x.shape[0]
    return jnp.concatenate([x * hm_ref[j, 0:r, :] for j in range(4)], axis=0)


def _diag_blocks(xt, heads):
    z = jnp.zeros_like(xt[0])
    return jnp.concatenate(
        [jnp.concatenate([xt[j] if i == j else z for i in range(heads)], axis=1)
         for j in range(heads)], axis=0)


def _mixer_kernel(sink_ref, pb_ref, pf_ref, pt_ref, bias_ref, hm_ref, aog_ref,
                  dww_ref, dwb_ref, lng_ref, lnb_ref, pw_ref, cvg_ref,
                  dec_ref, xi_ref, cdec_ref, gng_ref,
                  kbm_in_ref, vbm_in_ref, tail_in_ref, state_in_ref,
                  o_ref, kbm_out_ref, vbm_out_ref, tail_out_ref, state_out_ref,
                  kbp_ref, vbp_ref, kbm_ref, vbm_ref, ubuf_ref, state_ref, *, layer, first_block):
    n = pl.program_id(1)

    @pl.when(n == 0)
    def _init():
        kbp_ref[...] = jnp.zeros_like(kbp_ref)
        vbp_ref[...] = jnp.zeros_like(vbp_ref)
        if first_block:
            ubuf_ref[1, BLOCK:BLOCK + CONV_TAIL, :] = jnp.zeros((CONV_TAIL, ubuf_ref.shape[2]), F32)
            state_ref[1] = jnp.zeros(state_ref.shape[1:], F32)
            for g in range(2):
                ktf = pt_ref[HEAD_DIM * g:HEAD_DIM * (g + 1), :].astype(F32)
                tok = lax.broadcasted_iota(jnp.int32, ktf.shape, 1)
                ktm = jnp.where(tok >= PAD, ktf, 0.0)
                kbm_ref[g] = jnp.concatenate(
                    [pltpu.roll(ktm, N_META * (j + 1), axis=1) for j in range(4)], axis=0).astype(BF16)
                vm = pb_ref[PAD:BLOCK, PB_V + 128 * g:PB_V + 128 * (g + 1)]
                z = jnp.zeros((BLOCK - 4 * N_META, 256), BF16)
                vbm_ref[g] = jnp.concatenate(
                    [_row_blocks(jnp.concatenate([vm, vm], axis=1), hm_ref), z], axis=0)
        else:
            ubuf_ref[1, BLOCK:BLOCK + CONV_TAIL, :] = tail_in_ref[...]
            state_ref[1] = state_in_ref[...]
            kbm_ref[...] = kbm_in_ref[...]
            vbm_ref[...] = vbm_in_ref[...]

    for parity in range(2):
        @pl.when(lax.rem(n, 2) == parity)
        def _step():
            _mixer_step(sink_ref, pb_ref, pf_ref, pt_ref, bias_ref, hm_ref, aog_ref,
                        dww_ref, dwb_ref, lng_ref, lnb_ref, pw_ref, cvg_ref,
                        dec_ref, xi_ref, cdec_ref, gng_ref,
                        o_ref, kbm_out_ref, vbm_out_ref, tail_out_ref, state_out_ref,
                        kbp_ref, vbp_ref, kbm_ref, vbm_ref, ubuf_ref, state_ref,
                        layer=layer, wr=parity, rd=1 - parity)


def _mixer_step(sink_ref, pb_ref, pf_ref, pt_ref, bias_ref, hm_ref, aog_ref,
                dww_ref, dwb_ref, lng_ref, lnb_ref, pw_ref, cvg_ref,
                dec_ref, xi_ref, cdec_ref, gng_ref,
                o_ref, kbm_out_ref, vbm_out_ref, tail_out_ref, state_out_ref,
                kbp_ref, vbp_ref, kbm_ref, vbm_ref, ubuf_ref, state_ref, *, layer, wr, rd):
    n = pl.program_id(1)
    m_head = lax.shift_right_logical(lax.broadcasted_iota(jnp.int32, (BLOCK, BLOCK), 1), 4)

    att = []
    for g in range(2):
        kt = pt_ref[HEAD_DIM * g:HEAD_DIM * (g + 1), :]
        vg = pb_ref[:, PB_V + 128 * g:PB_V + 128 * (g + 1)]
        vrep = jnp.concatenate([vg, vg], axis=1)
        kb_cur = _diag_blocks([kt] * 4, 4)
        vb_cur = _row_blocks(vrep, hm_ref)

        qg = pb_ref[:, PB_Q + 256 * g:PB_Q + 256 * (g + 1)]
        gb = 1152 * g
        s_prev = _dot(qg, kbp_ref[2 * rd + g]) + bias_ref[0, :, gb:gb + 512]
        s_cur = _dot(qg, kb_cur) + bias_ref[0, :, gb + 512:gb + 1024]
        sm = _dot(qg, kbm_ref[g]) + bias_ref[0, :, gb + 1024:gb + 1152]

        mx = []
        for j in range(4):
            t = jnp.maximum(jnp.maximum(s_prev[:, 128 * j:128 * (j + 1)], s_cur[:, 128 * j:128 * (j + 1)]),
                            jnp.where(m_head == j, sm, NEG))
            mx.append(jnp.maximum(jnp.max(t, axis=-1, keepdims=True), sink_ref[layer, 4 * g + j]))
        m_meta = jnp.where(m_head == 0, mx[0],
                           jnp.where(m_head == 1, mx[1],
                                     jnp.where(m_head == 2, mx[2], mx[3])))
        pm = jnp.exp(sm - m_meta)
        pp, pc, linv = [], [], []
        for j in range(4):
            ppj = jnp.exp(s_prev[:, 128 * j:128 * (j + 1)] - mx[j])
            pcj = jnp.exp(s_cur[:, 128 * j:128 * (j + 1)] - mx[j])
            lj = jnp.sum(ppj + pcj + jnp.where(m_head == j, pm, 0.0), axis=-1, keepdims=True)
            lj = lj + jnp.exp(sink_ref[layer, 4 * g + j] - mx[j])
            pp.append(ppj.astype(BF16))
            pc.append(pcj.astype(BF16))
            linv.append(1.0 / lj)
        o = (_dot(jnp.concatenate(pp, axis=1), vbp_ref[2 * rd + g])
             + _dot(jnp.concatenate(pc, axis=1), vb_cur)
             + _dot(pm.astype(BF16), vbm_ref[g]))
        seg = lax.shift_right_logical(lax.broadcasted_iota(jnp.int32, (BLOCK, 256), 1), 6)
        lfull = jnp.where(seg == 0, linv[0],
                          jnp.where(seg == 1, linv[1],
                                    jnp.where(seg == 2, linv[2], linv[3])))
        att.append(o * lfull)
        kbp_ref[2 * wr + g] = kb_cur
        vbp_ref[2 * wr + g] = vb_cur
    y_att = _rms(jnp.concatenate(att, axis=1), aog_ref[...])

    ubuf_ref[wr, 0:CONV_TAIL, :] = ubuf_ref[rd, BLOCK:BLOCK + CONV_TAIL, :]
    ubuf_ref[wr, CONV_TAIL:CONV_TAIL + BLOCK, :] = pf_ref[:, 0:256]
    base = CONV_TAIL - (CONV_K - 1)
    ub = ubuf_ref[wr]
    nrow = CONV_TAIL + BLOCK
    acc = None
    for r in range(SUBLANES):
        taps = [t for t in range(CONV_K) if (base + t) % SUBLANES == r]
        xr = ub if r == 0 else pltpu.roll(ub, nrow - r, axis=0)
        for t in taps:
            a = base + t - r
            term = dww_ref[t:t + 1, :] * xr[a:a + BLOCK, :]
            acc = term if acc is None else acc + term
    acc = acc + dwb_ref[...]
    mu = jnp.mean(acc, axis=-1, keepdims=True)
    dc = acc - mu
    var = jnp.mean(dc * dc, axis=-1, keepdims=True)
    cn = dc * lax.rsqrt(var + LN_EPS) * lng_ref[...] + lnb_ref[...]
    cs = cn * _sigmoid(cn)
    y_cv = _rms(_dot(cs.astype(BF16), pw_ref[...]), cvg_ref[...])

    rq = pb_ref[:, PB_RQ:PB_RQ + 256]
    rkt = pt_ref[128:384, :]
    kb_r = _diag_blocks([rkt[HEAD_DIM * hh:HEAD_DIM * (hh + 1), :] for hh in range(RET_HEADS)],
                        RET_HEADS)
    vb_r = _row_blocks(pb_ref[:, PB_RV:PB_RV + 256], hm_ref)
    scores = _dot(rq, kb_r) * dec_ref[...]
    state = state_ref[rd]
    y_r = _dot(scores.astype(BF16), vb_r) + _dot(rq, state.astype(BF16)) * xi_ref[...]
    kv = _dot(rkt, pb_ref[:, PB_RVZ:PB_RVZ + 256])
    r2 = lax.shift_right_logical(lax.broadcasted_iota(jnp.int32, (256, 256), 0), 6)
    c2 = lax.shift_right_logical(lax.broadcasted_iota(jnp.int32, (256, 256), 1), 6)
    state_ref[wr] = cdec_ref[...] * state + jnp.where(r2 == c2, kv, 0.0)
    dr = y_r - _head_mean(y_r)
    var_r = _head_mean(dr * dr)
    yn = dr * lax.rsqrt(var_r + LN_EPS) * gng_ref[...]
    y_ret = pf_ref[:, 256:512] * yn

    o_ref[...] = jnp.concatenate([y_att, y_cv, y_ret], axis=1).astype(BF16)

    @pl.when(n == pl.num_programs(1) - 1)
    def _hand_over():
        kbm_out_ref[...] = kbm_ref[...]
        vbm_out_ref[...] = vbm_ref[...]
        tail_out_ref[...] = ubuf_ref[wr, BLOCK:BLOCK + CONV_TAIL, :]
        state_out_ref[...] = state_ref[wr]


def _mixer(pb, pf, pt, l, sinks, bias, hm, aog, dww, dwb, lng, lnb, pw, cvg,
           dec, xi, cdec, gng, state_in, batch, nblk, first_block):
    rows = pb.shape[0]
    d = aog.shape[2] + cvg.shape[2] + gng.shape[2]
    pos0 = 0 if first_block else 1

    def rowblk(b, n):
        return (b * nblk + n, 0)

    def colblk(b, n):
        return (0, b * nblk + n)

    def lspec(a):
        return _layer_spec(a, l)

    kern = functools.partial(_mixer_kernel, layer=l, first_block=first_block)
    state_shapes = tuple(jax.ShapeDtypeStruct(a.shape, a.dtype) for a in state_in)
    return pl.pallas_call(
        kern,
        out_shape=(jax.ShapeDtypeStruct((rows, d), BF16),) + state_shapes,
        grid=(batch, nblk),
        in_specs=[
            pl.BlockSpec(memory_space=pltpu.SMEM),
            pl.BlockSpec((BLOCK, pb.shape[1]), rowblk),
            pl.BlockSpec((BLOCK, pf.shape[1]), rowblk),
            pl.BlockSpec((pt.shape[0], BLOCK), colblk),
            pl.BlockSpec((1,) + bias.shape[1:], lambda b, n: (jnp.minimum(n + pos0, 2), 0, 0)),
            _const_spec(hm), lspec(aog),
            lspec(dww), lspec(dwb), lspec(lng), lspec(lnb), lspec(pw), lspec(cvg),
            _const_spec(dec), _const_spec(xi), _const_spec(cdec), lspec(gng),
        ] + [_const_spec(a) for a in state_in],
        out_specs=(pl.BlockSpec((BLOCK, d), rowblk),) + tuple(_const_spec(a) for a in state_in),
        scratch_shapes=[
            pltpu.VMEM((4, 256, 4 * BLOCK), BF16),
            pltpu.VMEM((4, 4 * BLOCK, 256), BF16),
            pltpu.VMEM((2, 256, BLOCK), BF16),
            pltpu.VMEM((2, BLOCK, 256), BF16),
            pltpu.VMEM((2, CONV_TAIL + BLOCK, 256), F32),
            pltpu.VMEM((2, 256, 256), F32),
        ],
        compiler_params=pltpu.CompilerParams(
            dimension_semantics=("arbitrary", "arbitrary"), vmem_limit_bytes=VMEM_LIMIT),
        name="mixer",
    )(sinks, pb, pf, pt, bias, hm, aog, dww, dwb, lng, lnb, pw, cvg,
      dec, xi, cdec, gng, *state_in)


def _mixer_state_zeros():
    return (jnp.zeros((2, 256, BLOCK), BF16), jnp.zeros((2, BLOCK, 256), BF16),
            jnp.zeros((CONV_TAIL, 256), F32), jnp.zeros((256, 256), F32))


def _ffn_kernel(h_ref, y_ref, wout_ref, g_ref, up_ref, dww_ref, dwb_ref, down_ref, carry_in_ref,
                o_ref, carry_out_ref, carry_ref, act_ref, *, tm, dff, cw, pad_rows):
    t = pl.program_id(1)
    nchunk = dff // cw
    wr = lax.rem(t, 2)
    rd = 1 - wr

    @pl.when(t == 0)
    def _init():
        carry_ref[1] = carry_in_ref[...]

    def keep(y):
        if not pad_rows:
            return y
        row = t * tm + lax.broadcasted_iota(jnp.int32, (tm, 1), 0)
        return jnp.where(row >= pad_rows, y, 0.0)

    h = h_ref[...] + keep(_dot(y_ref[...], wout_ref[...]))
    u = _rms(h, g_ref[...]).astype(BF16)
    for c in range(nchunk):
        halves = []
        for off in (c * cw, dff + c * cw):
            f = _dot(u, up_ref[:, off:off + cw])
            fe = jnp.concatenate([carry_ref[rd, :, off:off + cw], f], axis=0)
            carry_ref[wr, :, off:off + cw] = f[tm - FFN_TAIL:tm, :]
            fc = dwb_ref[:, off:off + cw] + dww_ref[FFN_CONV_K - 1:FFN_CONV_K, off:off + cw] * f
            for s in range(1, FFN_CONV_K):
                kk = FFN_CONV_K - 1 - s
                shifted = pltpu.roll(fe, s, axis=0)[FFN_TAIL:FFN_TAIL + tm, :]
                fc = fc + dww_ref[kk:kk + 1, off:off + cw] * shifted
            halves.append(fc)
        act_ref[:, c * cw:(c + 1) * cw] = (halves[0] * _sigmoid(halves[0]) * halves[1]).astype(BF16)
    o_ref[...] = h + keep(_dot(act_ref[...], down_ref[...]))

    @pl.when(t == pl.num_programs(1) - 1)
    def _hand_over():
        carry_out_ref[...] = carry_ref[wr]


def _ffn(h2, ycat, l, wout, g, up, dww, dwb, down, carry_in, batch, tiles, tm, cw, pad_rows):
    rows, d = h2.shape
    dff = down.shape[1]

    def rowblk(b, t):
        return (b * tiles + t, 0)

    def wspec(a):
        nd = a.ndim - 1
        return pl.BlockSpec((None,) + a.shape[1:], lambda b, t: (l,) + (0,) * nd,
                            pipeline_mode=pl.Buffered(1))

    kern = functools.partial(_ffn_kernel, tm=tm, dff=dff, cw=cw, pad_rows=pad_rows)
    return pl.pallas_call(
        kern,
        out_shape=(jax.ShapeDtypeStruct((rows, d), F32),
                   jax.ShapeDtypeStruct(carry_in.shape, F32)),
        grid=(batch, tiles),
        in_specs=[
            pl.BlockSpec((tm, d), rowblk),
            pl.BlockSpec((tm, d), rowblk),
            wspec(wout), wspec(g), wspec(up), wspec(dww), wspec(dwb), wspec(down),
            _const_spec(carry_in),
        ],
        out_specs=(pl.BlockSpec((tm, d), rowblk), _const_spec(carry_in)),
        scratch_shapes=[
            pltpu.VMEM((2, FFN_TAIL, 2 * dff), F32),
            pltpu.VMEM((tm, dff), BF16),
        ],
        compiler_params=pltpu.CompilerParams(
            dimension_semantics=("arbitrary", "arbitrary"), vmem_limit_bytes=VMEM_LIMIT),
        name="ffn",
    )(h2, ycat, wout, g, up, dww, dwb, down, carry_in)


def _retention_tables():
    log_gamma = jnp.log1p(-jnp.exp2(-5.0 - jnp.arange(RET_HEADS, dtype=F32)))
    idx = jnp.arange(BLOCK, dtype=F32)
    diff = idx[:, None] - idx[None, :]
    decay = jnp.where(diff[None] >= 0,
                      jnp.exp(jnp.maximum(diff, 0.0)[None] * log_gamma[:, None, None]), 0.0)
    zeta = jnp.exp((BLOCK - 1 - idx)[None, :] * log_gamma[:, None])
    xi = jnp.exp((idx + 1.0)[None, :] * log_gamma[:, None])
    chunk_decay = jnp.exp(BLOCK * log_gamma)
    scale = RET_DK ** -0.5
    dec_t = jnp.transpose(decay, (1, 0, 2)).reshape(BLOCK, RET_HEADS * BLOCK) * scale
    zeta_t = jnp.repeat(zeta.T, RET_DK, axis=1) * scale
    xi_t = jnp.repeat(xi.T, RET_DK, axis=1)
    cdec_t = jnp.repeat(chunk_decay, RET_DK)[None, :]
    return dec_t, zeta_t, xi_t, cdec_t


def _attention_bias():
    n_kv = N_Q_HEADS // GQA_GROUP
    slopes = jnp.exp2(-8.0 * jnp.arange(1, N_Q_HEADS + 1, dtype=F32) / N_Q_HEADS).reshape(n_kv, GQA_GROUP)
    n = jnp.arange(3)[:, None, None]
    i = jnp.arange(BLOCK)[None, :, None]
    c = jnp.arange(BLOCK)[None, None, :]

    def head_tiles(ok, dist):
        b = jnp.where(ok[:, :, None, None, :],
                      -slopes[None, None, :, :, None] * dist.astype(F32)[:, :, None, None, :], NEG)
        return b.reshape(3, BLOCK, n_kv, GQA_GROUP * BLOCK)

    prev = head_tiles(jnp.logical_and(c > i, n >= 2), jnp.broadcast_to(i - c + BLOCK, (3, BLOCK, BLOCK)))
    cur = head_tiles(jnp.logical_and(c <= i, n >= 1), jnp.broadcast_to(i - c, (3, BLOCK, BLOCK)))
    d_meta = n * BLOCK + i - PAD - (c % N_META)
    ok_meta = jnp.logical_and(d_meta >= 0, c < GQA_GROUP * N_META)
    sl_meta = slopes[:, jnp.minimum(jnp.arange(BLOCK) // N_META, GQA_GROUP - 1)]
    meta = jnp.where(ok_meta[:, :, None, :],
                     -sl_meta[None, None] * jnp.minimum(d_meta, WINDOW).astype(F32)[:, :, None, :], NEG)
    return jnp.concatenate([prev, cur, meta], axis=-1).reshape(3, BLOCK, -1)


def kernel(x, meta, norm_mix_g, w_in, q_norm_g, k_norm_g, attn_sinks, attn_out_g, cv_dw_w, cv_dw_b, cv_ln_g, cv_ln_b, cv_pw, cv_out_g, ret_gn_g, w_out, norm_ffn_g, ffn_up, ffn_dw_w, ffn_dw_b, ffn_down):
    batch, seq, d = x.shape
    depth = w_in.shape[0]
    dff = ffn_down.shape[1]

    dec_t, zeta_t, xi_t, cdec_t = _retention_tables()
    bias = _attention_bias()
    seg = jnp.arange(256) // HEAD_DIM
    hm = jnp.broadcast_to((seg[None, :] == jnp.arange(4)[:, None])[:, None, :], (4, BLOCK, 256)).astype(BF16)

    def vec(a):
        return a.reshape(depth, 1, a.shape[-1])

    w_in_b, w_out_b, up_b, down_b, pw_b = (a.astype(BF16) for a in (w_in, w_out, ffn_up, ffn_down, cv_pw))
    mix_g, ffn_g = vec(norm_mix_g), vec(norm_ffn_g)
    qg, kg = vec(jnp.tile(q_norm_g, (1, N_Q_HEADS))), vec(jnp.tile(k_norm_g, (1, N_Q_HEADS // GQA_GROUP)))
    aog, dwb, lng, lnb, cvg, gng, fdwb = (vec(a) for a in (
        attn_out_g, cv_dw_b, cv_ln_g, cv_ln_b, cv_out_g, ret_gn_g, ffn_dw_b))

    h0 = jnp.concatenate([jnp.zeros((PAD, d), x.dtype), meta.astype(x.dtype)], axis=0)
    hr = x.reshape(batch * seq, d)
    nblk = seq // BLOCK
    in_tm = 512
    ffn_tm = 512
    zeta_tile = jnp.tile(zeta_t, (in_tm // BLOCK, 1))

    def layer(h2, l, state_in, carry_in, nb, blocks, tm_in, tm_ffn, zt, first):
        pad_rows = PAD if first else 0
        pb, pf, pt = _inproj(h2, l, mix_g, w_in_b, qg, kg, zt, tm_in, pad_rows)
        ycat, *state = _mixer(pb, pf, pt, l, attn_sinks, bias, hm, aog, cv_dw_w, dwb, lng, lnb,
                              pw_b, cvg, dec_t, xi_t, cdec_t, gng, state_in, nb, blocks, first)
        h2, carry = _ffn(h2, ycat, l, w_out_b, ffn_g, up_b, ffn_dw_w, fdwb, down_b, carry_in,
                         nb, blocks * BLOCK // tm_ffn, tm_ffn, 256, pad_rows)
        return h2, tuple(state), carry

    for l in range(depth):
        h0, state0, carry0 = layer(h0, l, _mixer_state_zeros(), jnp.zeros((FFN_TAIL, 2 * dff), F32),
                                   1, 1, BLOCK, BLOCK, zeta_t, True)
        hr, _, _ = layer(hr, l, state0, carry0, batch, nblk, in_tm, ffn_tm, zeta_tile, False)
    return hr.reshape(batch, seq, d)
```

```python
import functools

import jax
import jax.numpy as jnp
from jax import lax
from jax.experimental import pallas as pl
from jax.experimental.pallas import tpu as pltpu

F32 = jnp.float32
BF16 = jnp.bfloat16

HEAD_DIM = 64
BLOCK = 128
WINDOW = 128
N_META = 16
PAD = BLOCK - N_META
N_Q_HEADS = 8
GQA_GROUP = 4
CONV_K = 31
FFN_CONV_K = 3
RET_DK = 64
RET_HEADS = 4
RMS_EPS = 1e-6
LN_EPS = 1e-5
NEG = -1e30

Q0, K0, V0, CA0, CB0, RQ0, RK0, RV0, RG0, IN_END = 0, 512, 640, 768, 1024, 1280, 1536, 1792, 2048, 2304
PB_Q, PB_V, PB_RQ, PB_RV, PB_RVZ, PB_END = 0, 512, 768, 1024, 1280, 1536

SUBLANES = 8
CONV_TAIL = 32
FFN_TAIL = SUBLANES
VMEM_LIMIT = 56 * 1024 * 1024


def _dot(a, b):
    return jnp.dot(a, b, preferred_element_type=F32)


def _sigmoid(x):
    return 1.0 / (1.0 + jnp.exp(-x))


def _rms(x, g):
    return x * lax.rsqrt(jnp.mean(x * x, axis=-1, keepdims=True) + RMS_EPS) * g


def _head_mean(x):
    low = lax.broadcasted_iota(jnp.int32, (x.shape[0], 2 * HEAD_DIM), 1) < HEAD_DIM
    cols = []
    for c in range(x.shape[1] // (2 * HEAD_DIM)):
        xs = x[:, 2 * HEAD_DIM * c:2 * HEAD_DIM * (c + 1)]
        lo = jnp.where(low, xs, 0.0)
        s_lo = jnp.sum(lo, axis=-1, keepdims=True)
        s_hi = jnp.sum(xs - lo, axis=-1, keepdims=True)
        cols.append(jnp.where(low, s_lo, s_hi))
    return jnp.concatenate(cols, axis=1) * (1.0 / HEAD_DIM)


def _layer_spec(a, l):
    nd = a.ndim - 1
    return pl.BlockSpec((None,) + a.shape[1:], lambda *_: (l,) + (0,) * nd)


def _const_spec(a):
    return pl.BlockSpec(a.shape, lambda *_: (0,) * a.ndim)


def _inproj_kernel(h_ref, g_ref, w_ref, qg_ref, kg_ref, zeta_ref,
                   pb_ref, pf_ref, pt_ref, p0_ref, p1_ref, *, tm, pad_rows):
    i = pl.program_id(0)

    @pl.when(i == 0)
    def _init():
        p1_ref[...] = jnp.zeros_like(p1_ref)

    @pl.when(lax.rem(i, 2) == 0)
    def _even():
        _inproj_step(h_ref, g_ref, w_ref, qg_ref, kg_ref, zeta_ref,
                     pb_ref, pf_ref, pt_ref, p0_ref, p1_ref, tm=tm, pad_rows=pad_rows)

    @pl.when(lax.rem(i, 2) == 1)
    def _odd():
        _inproj_step(h_ref, g_ref, w_ref, qg_ref, kg_ref, zeta_ref,
                     pb_ref, pf_ref, pt_ref, p1_ref, p0_ref, tm=tm, pad_rows=pad_rows)


def _inproj_step(h_ref, g_ref, w_ref, qg_ref, kg_ref, zeta_ref,
                 pb_ref, pf_ref, pt_ref, pw_ref, pr_ref, *, tm, pad_rows):
    u = _rms(h_ref[...], g_ref[...]).astype(BF16)
    pw_ref[...] = _dot(u, w_ref[...])

    def proj(lo, hi):
        return pr_ref[:, lo:hi]

    for half in range(2):
        lo = Q0 + 256 * half
        q = proj(lo, lo + 256)
        qn = q * lax.rsqrt(_head_mean(q * q) + RMS_EPS) * qg_ref[:, 256 * half:256 * (half + 1)]
        pb_ref[:, PB_Q + 256 * half:PB_Q + 256 * (half + 1)] = (qn * (HEAD_DIM ** -0.5)).astype(BF16)

    kv = proj(K0, CA0)
    k = kv[:, 0:128]
    v = kv[:, 128:256]
    kn = k * lax.rsqrt(_head_mean(k * k) + RMS_EPS) * kg_ref[...]
    pt_ref[0:128, :] = kn.T.astype(BF16)
    low = lax.broadcasted_iota(jnp.int32, v.shape, 1) < HEAD_DIM
    v_sw = pltpu.roll(v, HEAD_DIM, axis=1)
    pb_ref[:, PB_V:PB_V + 128] = jnp.where(low, v, v_sw).astype(BF16)
    pb_ref[:, PB_V + 128:PB_V + 256] = jnp.where(low, v_sw, v).astype(BF16)

    cab = proj(CA0, RQ0)
    pf_ref[:, 0:256] = cab[:, 0:256] * _sigmoid(cab[:, 256:512])

    pb_ref[:, PB_RQ:PB_RQ + 256] = proj(RQ0, RK0).astype(BF16)
    rk = proj(RK0, RV0)
    if pad_rows:
        row = lax.broadcasted_iota(jnp.int32, (tm, 1), 0)
        rk = jnp.where(row < pad_rows, 0.0, rk)
    pt_ref[128:384, :] = rk.T.astype(BF16)
    rv = proj(RV0, RG0)
    pb_ref[:, PB_RV:PB_RV + 256] = rv.astype(BF16)
    pb_ref[:, PB_RVZ:PB_RVZ + 256] = (rv * zeta_ref[...]).astype(BF16)
    rg = proj(RG0, IN_END)
    pf_ref[:, 256:512] = rg * _sigmoid(rg)


def _inproj(h2, l, g, w, qg, kg, zeta_tile, tm, pad_rows):
    rows, d = h2.shape
    ntiles = rows // tm
    nout = w.shape[2]
    kern = functools.partial(_inproj_kernel, tm=tm, pad_rows=pad_rows)
    return pl.pallas_call(
        kern,
        out_shape=(jax.ShapeDtypeStruct((rows, PB_END), BF16),
                   jax.ShapeDtypeStruct((rows, 512), F32),
                   jax.ShapeDtypeStruct((384, rows), BF16)),
        grid=(ntiles + 1,),
        in_specs=[
            pl.BlockSpec((tm, d), lambda i: (jnp.minimum(i, ntiles - 1), 0)),
            _layer_spec(g, l), _layer_spec(w, l), _layer_spec(qg, l), _layer_spec(kg, l),
            _const_spec(zeta_tile),
        ],
        out_specs=(pl.BlockSpec((tm, PB_END), lambda i: (jnp.maximum(i - 1, 0), 0)),
                   pl.BlockSpec((tm, 512), lambda i: (jnp.maximum(i - 1, 0), 0)),
                   pl.BlockSpec((384, tm), lambda i: (0, jnp.maximum(i - 1, 0)))),
        scratch_shapes=[pltpu.VMEM((tm, nout), F32)] * 2,
        compiler_params=pltpu.CompilerParams(
            dimension_semantics=("arbitrary",), vmem_limit_bytes=VMEM_LIMIT),
        name="in_proj",
    )(h2, g, w, qg, kg, zeta_tile)


def _row_blocks(x, hm_ref):
    r = x.shape[0]
    return jnp.concatenate([x * hm_ref[j, 0:r, :] for j in range(4)], axis=0)


def _diag_blocks(xt, heads):
    z = jnp.zeros_like(xt[0])
    return jnp.concatenate(
        [jnp.concatenate([xt[j] if i == j else z for i in range(heads)], axis=1)
         for j in range(heads)], axis=0)


def _mixer_kernel(sink_ref, pb_ref, pf_ref, pt_ref, bias_a_ref, bias_b_ref, hm_ref, aog_ref,
                  dww_ref, dwb_ref, lng_ref, lnb_ref, pw_ref, cvg_ref,
                  dec_ref, xi_ref, cdec_ref, gng_ref,
                  kbm_in_ref, vbm_in_ref, tail_in_ref, state_in_ref,
                  o_ref, kbm_out_ref, vbm_out_ref, tail_out_ref, state_out_ref,
                  kbp_ref, vbp_ref, kbm_ref, vbm_ref, tail_ref, state_ref, *, layer, first_block, bps):
    n = pl.program_id(1)

    @pl.when(n == 0)
    def _init():
        kbp_ref[...] = jnp.zeros_like(kbp_ref)
        vbp_ref[...] = jnp.zeros_like(vbp_ref)
        if first_block:
            tail_ref[1] = jnp.zeros(tail_ref.shape[1:], F32)
            state_ref[1] = jnp.zeros(state_ref.shape[1:], F32)
            for g in range(2):
                ktf = pt_ref[HEAD_DIM * g:HEAD_DIM * (g + 1), :].astype(F32)
                tok = lax.broadcasted_iota(jnp.int32, ktf.shape, 1)
                ktm = jnp.where(tok >= PAD, ktf, 0.0)
                kbm_ref[g] = jnp.concatenate(
                    [pltpu.roll(ktm, N_META * (j + 1), axis=1) for j in range(4)], axis=0).astype(BF16)
                vm = pb_ref[PAD:BLOCK, PB_V + 128 * g:PB_V + 128 * (g + 1)]
                z = jnp.zeros((BLOCK - 4 * N_META, 256), BF16)
                vbm_ref[g] = jnp.concatenate(
                    [_row_blocks(jnp.concatenate([vm, vm], axis=1), hm_ref), z], axis=0)
        else:
            tail_ref[1] = tail_in_ref[...]
            state_ref[1] = state_in_ref[...]
            kbm_ref[...] = kbm_in_ref[...]
            vbm_ref[...] = vbm_in_ref[...]

    for wr in range(2):
        rd = 1 - wr

        @pl.when(lax.rem(n, 2) == wr)
        def _step():
            prev = dict(kb=[lambda g=g: kbp_ref[2 * rd + g] for g in range(2)],
                        vb=[lambda g=g: vbp_ref[2 * rd + g] for g in range(2)],
                        tail=lambda: tail_ref[rd], state=lambda: state_ref[rd])
            for k in range(bps):
                cur = _mixer_block(sink_ref, pb_ref, pf_ref, pt_ref, bias_a_ref if k == 0 else bias_b_ref, hm_ref,
                                   aog_ref, dww_ref, dwb_ref, lng_ref, lnb_ref, pw_ref, cvg_ref,
                                   dec_ref, xi_ref, cdec_ref, gng_ref, o_ref, kbm_ref, vbm_ref,
                                   prev, layer=layer, row0=k * BLOCK)
                prev = dict(kb=[lambda g=g, c=cur: c["kb"][g] for g in range(2)],
                            vb=[lambda g=g, c=cur: c["vb"][g] for g in range(2)],
                            tail=lambda c=cur: c["tail"], state=lambda c=cur: c["state"])
            for g in range(2):
                kbp_ref[2 * wr + g] = cur["kb"][g]
                vbp_ref[2 * wr + g] = cur["vb"][g]
            tail_ref[wr] = cur["tail"]
            state_ref[wr] = cur["state"]

    @pl.when(n == pl.num_programs(1) - 1)
    def _hand_over():
        kbm_out_ref[...] = kbm_ref[...]
        vbm_out_ref[...] = vbm_ref[...]
        last = (pl.num_programs(1) - 1) % 2
        tail_out_ref[...] = tail_ref[last]
        state_out_ref[...] = state_ref[last]


def _mixer_block(sink_ref, pb_ref, pf_ref, pt_ref, bias_ref, hm_ref, aog_ref,
                 dww_ref, dwb_ref, lng_ref, lnb_ref, pw_ref, cvg_ref,
                 dec_ref, xi_ref, cdec_ref, gng_ref, o_ref, kbm_ref, vbm_ref,
                 prev, *, layer, row0):
    pb_ref = pb_ref.at[row0:row0 + BLOCK, :]
    pf_ref = pf_ref.at[row0:row0 + BLOCK, :]
    pt_ref = pt_ref.at[:, row0:row0 + BLOCK]
    o_ref = o_ref.at[row0:row0 + BLOCK, :]
    m_head = lax.shift_right_logical(lax.broadcasted_iota(jnp.int32, (BLOCK, BLOCK), 1), 4)

    att, kb_out, vb_out = [], [], []
    for g in range(2):
        kt = pt_ref[HEAD_DIM * g:HEAD_DIM * (g + 1), :]
        vg = pb_ref[:, PB_V + 128 * g:PB_V + 128 * (g + 1)]
        vrep = jnp.concatenate([vg, vg], axis=1)
        kb_cur = _diag_blocks([kt] * 4, 4)
        vb_cur = _row_blocks(vrep, hm_ref)
        kb_out.append(kb_cur)
        vb_out.append(vb_cur)

        qg = pb_ref[:, PB_Q + 256 * g:PB_Q + 256 * (g + 1)]
        gb = 1152 * g
        s_prev = _dot(qg, prev["kb"][g]()) + bias_ref[0, :, gb:gb + 512]
        s_cur = _dot(qg, kb_cur) + bias_ref[0, :, gb + 512:gb + 1024]
        sm = _dot(qg, kbm_ref[g]) + bias_ref[0, :, gb + 1024:gb + 1152]

        mx = []
        for j in range(4):
            t = jnp.maximum(jnp.maximum(s_prev[:, 128 * j:128 * (j + 1)], s_cur[:, 128 * j:128 * (j + 1)]),
                            jnp.where(m_head == j, sm, NEG))
            mx.append(jnp.maximum(jnp.max(t, axis=-1, keepdims=True), sink_ref[layer, 4 * g + j]))
        m_meta = jnp.where(m_head == 0, mx[0],
                           jnp.where(m_head == 1, mx[1],
                                     jnp.where(m_head == 2, mx[2], mx[3])))
        pm = jnp.exp(sm - m_meta)
        pp, pc, linv = [], [], []
        for j in range(4):
            ppj = jnp.exp(s_prev[:, 128 * j:128 * (j + 1)] - mx[j])
            pcj = jnp.exp(s_cur[:, 128 * j:128 * (j + 1)] - mx[j])
            lj = jnp.sum(ppj + pcj + jnp.where(m_head == j, pm, 0.0), axis=-1, keepdims=True)
            lj = lj + jnp.exp(sink_ref[layer, 4 * g + j] - mx[j])
            pp.append(ppj.astype(BF16))
            pc.append(pcj.astype(BF16))
            linv.append(1.0 / lj)
        o = (_dot(jnp.concatenate(pp, axis=1), prev["vb"][g]())
             + _dot(jnp.concatenate(pc, axis=1), vb_cur)
             + _dot(pm.astype(BF16), vbm_ref[g]))
        seg = lax.shift_right_logical(lax.broadcasted_iota(jnp.int32, (BLOCK, 256), 1), 6)
        lfull = jnp.where(seg == 0, linv[0],
                          jnp.where(seg == 1, linv[1],
                                    jnp.where(seg == 2, linv[2], linv[3])))
        att.append(o * lfull)
    y_att = _rms(jnp.concatenate(att, axis=1), aog_ref[...])

    u = pf_ref[:, 0:256]
    ub = jnp.concatenate([prev["tail"](), u], axis=0)
    base = CONV_TAIL - (CONV_K - 1)
    nrow = CONV_TAIL + BLOCK
    acc = None
    for r in range(SUBLANES):
        taps = [t for t in range(CONV_K) if (base + t) % SUBLANES == r]
        xr = ub if r == 0 else pltpu.roll(ub, nrow - r, axis=0)
        for t in taps:
            a = base + t - r
            term = dww_ref[t:t + 1, :] * xr[a:a + BLOCK, :]
            acc = term if acc is None else acc + term
    acc = acc + dwb_ref[...]
    mu = jnp.mean(acc, axis=-1, keepdims=True)
    dc = acc - mu
    var = jnp.mean(dc * dc, axis=-1, keepdims=True)
    cn = dc * lax.rsqrt(var + LN_EPS) * lng_ref[...] + lnb_ref[...]
    cs = cn * _sigmoid(cn)
    y_cv = _rms(_dot(cs.astype(BF16), pw_ref[...]), cvg_ref[...])

    rq = pb_ref[:, PB_RQ:PB_RQ + 256]
    rkt = pt_ref[128:384, :]
    kb_r = _diag_blocks([rkt[HEAD_DIM * hh:HEAD_DIM * (hh + 1), :] for hh in range(RET_HEADS)],
                        RET_HEADS)
    vb_r = _row_blocks(pb_ref[:, PB_RV:PB_RV + 256], hm_ref)
    scores = _dot(rq, kb_r) * dec_ref[...]
    state = prev["state"]()
    y_r = _dot(scores.astype(BF16), vb_r) + _dot(rq, state.astype(BF16)) * xi_ref[...]
    kv = _dot(rkt, pb_ref[:, PB_RVZ:PB_RVZ + 256])
    r2 = lax.shift_right_logical(lax.broadcasted_iota(jnp.int32, (256, 256), 0), 6)
    c2 = lax.shift_right_logical(lax.broadcasted_iota(jnp.int32, (256, 256), 1), 6)
    state_new = cdec_ref[...] * state + jnp.where(r2 == c2, kv, 0.0)
    dr = y_r - _head_mean(y_r)
    var_r = _head_mean(dr * dr)
    yn = dr * lax.rsqrt(var_r + LN_EPS) * gng_ref[...]
    y_ret = pf_ref[:, 256:512] * yn

    o_ref[...] = jnp.concatenate([y_att, y_cv, y_ret], axis=1).astype(BF16)
    return dict(kb=kb_out, vb=vb_out, tail=u[BLOCK - CONV_TAIL:BLOCK, :], state=state_new)


def _mixer(pb, pf, pt, l, sinks, bias, hm, aog, dww, dwb, lng, lnb, pw, cvg,
           dec, xi, cdec, gng, state_in, batch, nblk, first_block):
    rows = pb.shape[0]
    d = aog.shape[2] + cvg.shape[2] + gng.shape[2]
    pos0 = 0 if first_block else 1
    bps = 2 if nblk % 2 == 0 else 1
    steps = nblk // bps
    tb = bps * BLOCK

    def rowblk(b, n):
        return (b * steps + n, 0)

    def colblk(b, n):
        return (0, b * steps + n)

    def lspec(a):
        return _layer_spec(a, l)

    def bias_spec(k):
        return pl.BlockSpec((1,) + bias.shape[1:], lambda b, n: (jnp.minimum(pos0 + n * bps + k, 2), 0, 0))

    kern = functools.partial(_mixer_kernel, layer=l, first_block=first_block, bps=bps)
    state_shapes = tuple(jax.ShapeDtypeStruct(a.shape, a.dtype) for a in state_in)
    return pl.pallas_call(
        kern,
        out_shape=(jax.ShapeDtypeStruct((rows, d), BF16),) + state_shapes,
        grid=(batch, steps),
        in_specs=[
            pl.BlockSpec(memory_space=pltpu.SMEM),
            pl.BlockSpec((tb, pb.shape[1]), rowblk),
            pl.BlockSpec((tb, pf.shape[1]), rowblk),
            pl.BlockSpec((pt.shape[0], tb), colblk),
            bias_spec(0), bias_spec(bps - 1),
            _const_spec(hm), lspec(aog),
            lspec(dww), lspec(dwb), lspec(lng), lspec(lnb), lspec(pw), lspec(cvg),
            _const_spec(dec), _const_spec(xi), _const_spec(cdec), lspec(gng),
        ] + [_const_spec(a) for a in state_in],
        out_specs=(pl.BlockSpec((tb, d), rowblk),) + tuple(_const_spec(a) for a in state_in),
        scratch_shapes=[
            pltpu.VMEM((4, 256, 4 * BLOCK), BF16),
            pltpu.VMEM((4, 4 * BLOCK, 256), BF16),
            pltpu.VMEM((2, 256, BLOCK), BF16),
            pltpu.VMEM((2, BLOCK, 256), BF16),
            pltpu.VMEM((2, CONV_TAIL, 256), F32),
            pltpu.VMEM((2, 256, 256), F32),
        ],
        compiler_params=pltpu.CompilerParams(
            dimension_semantics=("arbitrary", "arbitrary"), vmem_limit_bytes=VMEM_LIMIT),
        name="mixer",
    )(sinks, pb, pf, pt, bias, bias, hm, aog, dww, dwb, lng, lnb, pw, cvg,
      dec, xi, cdec, gng, *state_in)


def _mixer_state_zeros():
    return (jnp.zeros((2, 256, BLOCK), BF16), jnp.zeros((2, BLOCK, 256), BF16),
            jnp.zeros((CONV_TAIL, 256), F32), jnp.zeros((256, 256), F32))


def _ffn_kernel(h_ref, y_ref, wout_ref, g_ref, up_ref, dww_ref, dwb_ref, down_ref, carry_in_ref,
                o_ref, carry_out_ref, carry_ref, act_ref, *, tm, dff, cw, pad_rows):
    t = pl.program_id(1)
    nchunk = dff // cw
    wr = lax.rem(t, 2)
    rd = 1 - wr

    @pl.when(t == 0)
    def _init():
        carry_ref[1] = carry_in_ref[...]

    def keep(y):
        if not pad_rows:
            return y
        row = t * tm + lax.broadcasted_iota(jnp.int32, (tm, 1), 0)
        return jnp.where(row >= pad_rows, y, 0.0)

    h = h_ref[...] + keep(_dot(y_ref[...], wout_ref[...]))
    u = _rms(h, g_ref[...]).astype(BF16)
    for c in range(nchunk):
        halves = []
        for off in (c * cw, dff + c * cw):
            f = _dot(u, up_ref[:, off:off + cw])
            fe = jnp.concatenate([carry_ref[rd, :, off:off + cw], f], axis=0)
            carry_ref[wr, :, off:off + cw] = f[tm - FFN_TAIL:tm, :]
            fc = dwb_ref[:, off:off + cw] + dww_ref[FFN_CONV_K - 1:FFN_CONV_K, off:off + cw] * f
            for s in range(1, FFN_CONV_K):
                kk = FFN_CONV_K - 1 - s
                shifted = pltpu.roll(fe, s, axis=0)[FFN_TAIL:FFN_TAIL + tm, :]
                fc = fc + dww_ref[kk:kk + 1, off:off + cw] * shifted
            halves.append(fc)
        act_ref[:, c * cw:(c + 1) * cw] = (halves[0] * _sigmoid(halves[0]) * halves[1]).astype(BF16)
    o_ref[...] = h + keep(_dot(act_ref[...], down_ref[...]))

    @pl.when(t == pl.num_programs(1) - 1)
    def _hand_over():
        carry_out_ref[...] = carry_ref[wr]


def _ffn(h2, ycat, l, wout, g, up, dww, dwb, down, carry_in, batch, tiles, tm, cw, pad_rows):
    rows, d = h2.shape
    dff = down.shape[1]

    def rowblk(b, t):
        return (b * tiles + t, 0)

    def wspec(a):
        nd = a.ndim - 1
        return pl.BlockSpec((None,) + a.shape[1:], lambda b, t: (l,) + (0,) * nd,
                            pipeline_mode=pl.Buffered(1))

    kern = functools.partial(_ffn_kernel, tm=tm, dff=dff, cw=cw, pad_rows=pad_rows)
    return pl.pallas_call(
        kern,
        out_shape=(jax.ShapeDtypeStruct((rows, d), F32),
                   jax.ShapeDtypeStruct(carry_in.shape, F32)),
        grid=(batch, tiles),
        in_specs=[
            pl.BlockSpec((tm, d), rowblk),
            pl.BlockSpec((tm, d), rowblk),
            wspec(wout), wspec(g), wspec(up), wspec(dww), wspec(dwb), wspec(down),
            _const_spec(carry_in),
        ],
        out_specs=(pl.BlockSpec((tm, d), rowblk), _const_spec(carry_in)),
        scratch_shapes=[
            pltpu.VMEM((2, FFN_TAIL, 2 * dff), F32),
            pltpu.VMEM((tm, dff), BF16),
        ],
        compiler_params=pltpu.CompilerParams(
            dimension_semantics=("arbitrary", "arbitrary"), vmem_limit_bytes=VMEM_LIMIT),
        name="ffn",
    )(h2, ycat, wout, g, up, dww, dwb, down, carry_in)


def _retention_tables():
    log_gamma = jnp.log1p(-jnp.exp2(-5.0 - jnp.arange(RET_HEADS, dtype=F32)))
    idx = jnp.arange(BLOCK, dtype=F32)
    diff = idx[:, None] - idx[None, :]
    decay = jnp.where(diff[None] >= 0,
                      jnp.exp(jnp.maximum(diff, 0.0)[None] * log_gamma[:, None, None]), 0.0)
    zeta = jnp.exp((BLOCK - 1 - idx)[None, :] * log_gamma[:, None])
    xi = jnp.exp((idx + 1.0)[None, :] * log_gamma[:, None])
    chunk_decay = jnp.exp(BLOCK * log_gamma)
    scale = RET_DK ** -0.5
    dec_t = jnp.transpose(decay, (1, 0, 2)).reshape(BLOCK, RET_HEADS * BLOCK) * scale
    zeta_t = jnp.repeat(zeta.T, RET_DK, axis=1) * scale
    xi_t = jnp.repeat(xi.T, RET_DK, axis=1)
    cdec_t = jnp.repeat(chunk_decay, RET_DK)[None, :]
    return dec_t, zeta_t, xi_t, cdec_t


def _attention_bias():
    n_kv = N_Q_HEADS // GQA_GROUP
    slopes = jnp.exp2(-8.0 * jnp.arange(1, N_Q_HEADS + 1, dtype=F32) / N_Q_HEADS).reshape(n_kv, GQA_GROUP)
    n = jnp.arange(3)[:, None, None]
    i = jnp.arange(BLOCK)[None, :, None]
    c = jnp.arange(BLOCK)[None, None, :]

    def head_tiles(ok, dist):
        b = jnp.where(ok[:, :, None, None, :],
                      -slopes[None, None, :, :, None] * dist.astype(F32)[:, :, None, None, :], NEG)
        return b.reshape(3, BLOCK, n_kv, GQA_GROUP * BLOCK)

    prev = head_tiles(jnp.logical_and(c > i, n >= 2), jnp.broadcast_to(i - c + BLOCK, (3, BLOCK, BLOCK)))
    cur = head_tiles(jnp.logical_and(c <= i, n >= 1), jnp.broadcast_to(i - c, (3, BLOCK, BLOCK)))
    d_meta = n * BLOCK + i - PAD - (c % N_META)
    ok_meta = jnp.logical_and(d_meta >= 0, c < GQA_GROUP * N_META)
    sl_meta = slopes[:, jnp.minimum(jnp.arange(BLOCK) // N_META, GQA_GROUP - 1)]
    meta = jnp.where(ok_meta[:, :, None, :],
                     -sl_meta[None, None] * jnp.minimum(d_meta, WINDOW).astype(F32)[:, :, None, :], NEG)
    return jnp.concatenate([prev, cur, meta], axis=-1).reshape(3, BLOCK, -1)


def kernel(x, meta, norm_mix_g, w_in, q_norm_g, k_norm_g, attn_sinks, attn_out_g, cv_dw_w, cv_dw_b, cv_ln_g, cv_ln_b, cv_pw, cv_out_g, ret_gn_g, w_out, norm_ffn_g, ffn_up, ffn_dw_w, ffn_dw_b, ffn_down):
    batch, seq, d = x.shape
    depth = w_in.shape[0]
    dff = ffn_down.shape[1]

    dec_t, zeta_t, xi_t, cdec_t = _retention_tables()
    bias = _attention_bias()
    seg = jnp.arange(256) // HEAD_DIM
    hm = jnp.broadcast_to((seg[None, :] == jnp.arange(4)[:, None])[:, None, :], (4, BLOCK, 256)).astype(BF16)

    def vec(a):
        return a.reshape(depth, 1, a.shape[-1])

    w_in_b, w_out_b, up_b, down_b, pw_b = (a.astype(BF16) for a in (w_in, w_out, ffn_up, ffn_down, cv_pw))
    mix_g, ffn_g = vec(norm_mix_g), vec(norm_ffn_g)
    qg, kg = vec(jnp.tile(q_norm_g, (1, N_Q_HEADS))), vec(jnp.tile(k_norm_g, (1, N_Q_HEADS // GQA_GROUP)))
    aog, dwb, lng, lnb, cvg, gng, fdwb = (vec(a) for a in (
        attn_out_g, cv_dw_b, cv_ln_g, cv_ln_b, cv_out_g, ret_gn_g, ffn_dw_b))

    h0 = jnp.concatenate([jnp.zeros((PAD, d), x.dtype), meta.astype(x.dtype)], axis=0)
    hr = x.reshape(batch * seq, d)
    nblk = seq // BLOCK
    in_tm = 512
    ffn_tm = 512
    zeta_tile = jnp.tile(zeta_t, (in_tm // BLOCK, 1))

    def layer(h2, l, state_in, carry_in, nb, blocks, tm_in, tm_ffn, zt, first):
        pad_rows = PAD if first else 0
        pb, pf, pt = _inproj(h2, l, mix_g, w_in_b, qg, kg, zt, tm_in, pad_rows)
        ycat, *state = _mixer(pb, pf, pt, l, attn_sinks, bias, hm, aog, cv_dw_w, dwb, lng, lnb,
                              pw_b, cvg, dec_t, xi_t, cdec_t, gng, state_in, nb, blocks, first)
        h2, carry = _ffn(h2, ycat, l, w_out_b, ffn_g, up_b, ffn_dw_w, fdwb, down_b, carry_in,
                         nb, blocks * BLOCK // tm_ffn, tm_ffn, 256, pad_rows)
        return h2, tuple(state), carry

    for l in range(depth):
        h0, state0, carry0 = layer(h0, l, _mixer_state_zeros(), jnp.zeros((FFN_TAIL, 2 * dff), F32),
                                   1, 1, BLOCK, BLOCK, zeta_t, True)
        hr, _, _ = layer(hr, l, state0, carry0, batch, nblk, in_tm, ffn_tm, zeta_tile, False)
    return hr.reshape(batch, seq, d)
```

```python
import functools

import jax
import jax.numpy as jnp
from jax import lax
from jax.experimental import pallas as pl
from jax.experimental.pallas import tpu as pltpu

F32 = jnp.float32
BF16 = jnp.bfloat16

HEAD_DIM = 64
BLOCK = 128
WINDOW = 128
N_META = 16
PAD = BLOCK - N_META
N_Q_HEADS = 8
GQA_GROUP = 4
CONV_K = 31
FFN_CONV_K = 3
RET_DK = 64
RET_HEADS = 4
RMS_EPS = 1e-6
LN_EPS = 1e-5
NEG = -1e30

Q0, K0, V0, CA0, CB0, RQ0, RK0, RV0, RG0, IN_END = 0, 512, 640, 768, 1024, 1280, 1536, 1792, 2048, 2304
PB_Q, PB_V, PB_RQ, PB_RV, PB_RVZ, PB_END = 0, 512, 768, 1024, 1280, 1536

SUBLANES = 8
CONV_TAIL = 32
FFN_TAIL = SUBLANES
VMEM_LIMIT = 56 * 1024 * 1024


def _dot(a, b):
    return jnp.dot(a, b, preferred_element_type=F32)


def _sigmoid(x):
    return 1.0 / (1.0 + jnp.exp(-x))


def _rms(x, g):
    return x * lax.rsqrt(jnp.mean(x * x, axis=-1, keepdims=True) + RMS_EPS) * g


def _head_mean(x):
    low = lax.broadcasted_iota(jnp.int32, (x.shape[0], 2 * HEAD_DIM), 1) < HEAD_DIM
    cols = []
    for c in range(x.shape[1] // (2 * HEAD_DIM)):
        xs = x[:, 2 * HEAD_DIM * c:2 * HEAD_DIM * (c + 1)]
        lo = jnp.where(low, xs, 0.0)
        s_lo = jnp.sum(lo, axis=-1, keepdims=True)
        s_hi = jnp.sum(xs - lo, axis=-1, keepdims=True)
        cols.append(jnp.where(low, s_lo, s_hi))
    return jnp.concatenate(cols, axis=1) * (1.0 / HEAD_DIM)


def _layer_spec(a, l):
    nd = a.ndim - 1
    return pl.BlockSpec((None,) + a.shape[1:], lambda *_: (l,) + (0,) * nd)


def _const_spec(a):
    return pl.BlockSpec(a.shape, lambda *_: (0,) * a.ndim)


def _inproj_kernel(h_ref, g_ref, w_ref, qg_ref, kg_ref, zeta_ref,
                   pb_ref, pf_ref, pt_ref, p0_ref, p1_ref, *, tm, pad_rows):
    i = pl.program_id(0)

    @pl.when(i == 0)
    def _init():
        p1_ref[...] = jnp.zeros_like(p1_ref)

    @pl.when(lax.rem(i, 2) == 0)
    def _even():
        _inproj_step(h_ref, g_ref, w_ref, qg_ref, kg_ref, zeta_ref,
                     pb_ref, pf_ref, pt_ref, p0_ref, p1_ref, tm=tm, pad_rows=pad_rows)

    @pl.when(lax.rem(i, 2) == 1)
    def _odd():
        _inproj_step(h_ref, g_ref, w_ref, qg_ref, kg_ref, zeta_ref,
                     pb_ref, pf_ref, pt_ref, p1_ref, p0_ref, tm=tm, pad_rows=pad_rows)


def _inproj_step(h_ref, g_ref, w_ref, qg_ref, kg_ref, zeta_ref,
                 pb_ref, pf_ref, pt_ref, pw_ref, pr_ref, *, tm, pad_rows):
    u = _rms(h_ref[...], g_ref[...]).astype(BF16)
    pw_ref[...] = _dot(u, w_ref[...])
    for _ in _proj_groups(lambda lo, hi: pr_ref[:, lo:hi], qg_ref, kg_ref, zeta_ref,
                          pb_ref, pf_ref, pt_ref, tm=tm, pad_rows=pad_rows):
        pass


def _proj_groups(proj, qg_ref, kg_ref, zeta_ref, pb_ref, pf_ref, pt_ref, *, tm, pad_rows):
    for half in range(2):
        lo = Q0 + 256 * half
        q = proj(lo, lo + 256)
        qn = q * lax.rsqrt(_head_mean(q * q) + RMS_EPS) * qg_ref[:, 256 * half:256 * (half + 1)]
        pb_ref[:, PB_Q + 256 * half:PB_Q + 256 * (half + 1)] = (qn * (HEAD_DIM ** -0.5)).astype(BF16)
        yield

    kv = proj(K0, CA0)
    k = kv[:, 0:128]
    v = kv[:, 128:256]
    kn = k * lax.rsqrt(_head_mean(k * k) + RMS_EPS) * kg_ref[...]
    pt_ref[0:128, :] = kn.T.astype(BF16)
    low = lax.broadcasted_iota(jnp.int32, v.shape, 1) < HEAD_DIM
    v_sw = pltpu.roll(v, HEAD_DIM, axis=1)
    pb_ref[:, PB_V:PB_V + 128] = jnp.where(low, v, v_sw).astype(BF16)
    pb_ref[:, PB_V + 128:PB_V + 256] = jnp.where(low, v_sw, v).astype(BF16)
    yield

    cab = proj(CA0, RQ0)
    pf_ref[:, 0:256] = cab[:, 0:256] * _sigmoid(cab[:, 256:512])
    yield

    pb_ref[:, PB_RQ:PB_RQ + 256] = proj(RQ0, RK0).astype(BF16)
    yield
    rk = proj(RK0, RV0)
    if pad_rows:
        row = lax.broadcasted_iota(jnp.int32, (tm, 1), 0)
        rk = jnp.where(row < pad_rows, 0.0, rk)
    pt_ref[128:384, :] = rk.T.astype(BF16)
    yield
    rv = proj(RV0, RG0)
    pb_ref[:, PB_RV:PB_RV + 256] = rv.astype(BF16)
    pb_ref[:, PB_RVZ:PB_RVZ + 256] = (rv * zeta_ref[...]).astype(BF16)
    yield
    rg = proj(RG0, IN_END)
    pf_ref[:, 256:512] = rg * _sigmoid(rg)
    yield


def _inproj(h2, l, g, w, qg, kg, zeta_tile, tm, pad_rows):
    rows, d = h2.shape
    ntiles = rows // tm
    nout = w.shape[2]
    kern = functools.partial(_inproj_kernel, tm=tm, pad_rows=pad_rows)
    return pl.pallas_call(
        kern,
        out_shape=(jax.ShapeDtypeStruct((rows, PB_END), BF16),
                   jax.ShapeDtypeStruct((rows, 512), F32),
                   jax.ShapeDtypeStruct((384, rows), BF16)),
        grid=(ntiles + 1,),
        in_specs=[
            pl.BlockSpec((tm, d), lambda i: (jnp.minimum(i, ntiles - 1), 0)),
            _layer_spec(g, l), _layer_spec(w, l), _layer_spec(qg, l), _layer_spec(kg, l),
            _const_spec(zeta_tile),
        ],
        out_specs=(pl.BlockSpec((tm, PB_END), lambda i: (jnp.maximum(i - 1, 0), 0)),
                   pl.BlockSpec((tm, 512), lambda i: (jnp.maximum(i - 1, 0), 0)),
                   pl.BlockSpec((384, tm), lambda i: (0, jnp.maximum(i - 1, 0)))),
        scratch_shapes=[pltpu.VMEM((tm, nout), F32)] * 2,
        compiler_params=pltpu.CompilerParams(
            dimension_semantics=("arbitrary",), vmem_limit_bytes=VMEM_LIMIT),
        name="in_proj",
    )(h2, g, w, qg, kg, zeta_tile)


def _row_blocks(x, hm_ref):
    r = x.shape[0]
    return jnp.concatenate([x * hm_ref[j, 0:r, :] for j in range(4)], axis=0)


def _diag_blocks(xt, heads):
    z = jnp.zeros_like(xt[0])
    return jnp.concatenate(
        [jnp.concatenate([xt[j] if i == j else z for i in range(heads)], axis=1)
         for j in range(heads)], axis=0)


def _mixer_kernel(sink_ref, pb_ref, pf_ref, pt_ref, bias_a_ref, bias_b_ref, hm_ref, aog_ref,
                  dww_ref, dwb_ref, lng_ref, lnb_ref, pw_ref, cvg_ref,
                  dec_ref, xi_ref, cdec_ref, gng_ref,
                  kbm_in_ref, vbm_in_ref, tail_in_ref, state_in_ref,
                  o_ref, kbm_out_ref, vbm_out_ref, tail_out_ref, state_out_ref,
                  kbp_ref, vbp_ref, kbm_ref, vbm_ref, tail_ref, state_ref, *, layer, first_block, bps):
    n = pl.program_id(1)

    @pl.when(n == 0)
    def _init():
        kbp_ref[...] = jnp.zeros_like(kbp_ref)
        vbp_ref[...] = jnp.zeros_like(vbp_ref)
        if first_block:
            tail_ref[1] = jnp.zeros(tail_ref.shape[1:], F32)
            state_ref[1] = jnp.zeros(state_ref.shape[1:], F32)
            for g in range(2):
                ktf = pt_ref[HEAD_DIM * g:HEAD_DIM * (g + 1), :].astype(F32)
                tok = lax.broadcasted_iota(jnp.int32, ktf.shape, 1)
                ktm = jnp.where(tok >= PAD, ktf, 0.0)
                kbm_ref[g] = jnp.concatenate(
                    [pltpu.roll(ktm, N_META * (j + 1), axis=1) for j in range(4)], axis=0).astype(BF16)
                vm = pb_ref[PAD:BLOCK, PB_V + 128 * g:PB_V + 128 * (g + 1)]
                z = jnp.zeros((BLOCK - 4 * N_META, 256), BF16)
                vbm_ref[g] = jnp.concatenate(
                    [_row_blocks(jnp.concatenate([vm, vm], axis=1), hm_ref), z], axis=0)
        else:
            tail_ref[1] = tail_in_ref[...]
            state_ref[1] = state_in_ref[...]
            kbm_ref[...] = kbm_in_ref[...]
            vbm_ref[...] = vbm_in_ref[...]

    for wr in range(2):
        rd = 1 - wr

        @pl.when(lax.rem(n, 2) == wr)
        def _step():
            prev = dict(kb=[lambda g=g: kbp_ref[2 * rd + g] for g in range(2)],
                        vb=[lambda g=g: vbp_ref[2 * rd + g] for g in range(2)],
                        tail=lambda: tail_ref[rd], state=lambda: state_ref[rd])
            for k in range(bps):
                cur = _mixer_block(sink_ref, pb_ref, pf_ref, pt_ref, bias_a_ref if k == 0 else bias_b_ref, hm_ref,
                                   aog_ref, dww_ref, dwb_ref, lng_ref, lnb_ref, pw_ref, cvg_ref,
                                   dec_ref, xi_ref, cdec_ref, gng_ref, o_ref, kbm_ref, vbm_ref,
                                   prev, layer=layer, row0=k * BLOCK)
                prev = dict(kb=[lambda g=g, c=cur: c["kb"][g] for g in range(2)],
                            vb=[lambda g=g, c=cur: c["vb"][g] for g in range(2)],
                            tail=lambda c=cur: c["tail"], state=lambda c=cur: c["state"])
            for g in range(2):
                kbp_ref[2 * wr + g] = cur["kb"][g]
                vbp_ref[2 * wr + g] = cur["vb"][g]
            tail_ref[wr] = cur["tail"]
            state_ref[wr] = cur["state"]

    @pl.when(n == pl.num_programs(1) - 1)
    def _hand_over():
        kbm_out_ref[...] = kbm_ref[...]
        vbm_out_ref[...] = vbm_ref[...]
        last = (pl.num_programs(1) - 1) % 2
        tail_out_ref[...] = tail_ref[last]
        state_out_ref[...] = state_ref[last]


def _mixer_block(sink_ref, pb_ref, pf_ref, pt_ref, bias_ref, hm_ref, aog_ref,
                 dww_ref, dwb_ref, lng_ref, lnb_ref, pw_ref, cvg_ref,
                 dec_ref, xi_ref, cdec_ref, gng_ref, o_ref, kbm_ref, vbm_ref,
                 prev, *, layer, row0, tick=lambda: None):
    pb_ref = pb_ref.at[row0:row0 + BLOCK, :]
    pf_ref = pf_ref.at[row0:row0 + BLOCK, :]
    pt_ref = pt_ref.at[:, row0:row0 + BLOCK]
    o_ref = o_ref.at[row0:row0 + BLOCK, :]
    m_head = lax.shift_right_logical(lax.broadcasted_iota(jnp.int32, (BLOCK, BLOCK), 1), 4)

    att, kb_out, vb_out = [], [], []
    for g in range(2):
        tick()
        kt = pt_ref[HEAD_DIM * g:HEAD_DIM * (g + 1), :]
        vg = pb_ref[:, PB_V + 128 * g:PB_V + 128 * (g + 1)]
        vrep = jnp.concatenate([vg, vg], axis=1)
        kb_cur = _diag_blocks([kt] * 4, 4)
        vb_cur = _row_blocks(vrep, hm_ref)
        kb_out.append(kb_cur)
        vb_out.append(vb_cur)

        qg = pb_ref[:, PB_Q + 256 * g:PB_Q + 256 * (g + 1)]
        gb = 1152 * g
        s_prev = _dot(qg, prev["kb"][g]()) + bias_ref[0, :, gb:gb + 512]
        s_cur = _dot(qg, kb_cur) + bias_ref[0, :, gb + 512:gb + 1024]
        sm = _dot(qg, kbm_ref[g]) + bias_ref[0, :, gb + 1024:gb + 1152]

        mx = []
        for j in range(4):
            t = jnp.maximum(jnp.maximum(s_prev[:, 128 * j:128 * (j + 1)], s_cur[:, 128 * j:128 * (j + 1)]),
                            jnp.where(m_head == j, sm, NEG))
            mx.append(jnp.maximum(jnp.max(t, axis=-1, keepdims=True), sink_ref[layer, 4 * g + j]))
        m_meta = jnp.where(m_head == 0, mx[0],
                           jnp.where(m_head == 1, mx[1],
                                     jnp.where(m_head == 2, mx[2], mx[3])))
        pm = jnp.exp(sm - m_meta)
        pp, pc, linv = [], [], []
        for j in range(4):
            tick()
            ppj = jnp.exp(s_prev[:, 128 * j:128 * (j + 1)] - mx[j])
            pcj = jnp.exp(s_cur[:, 128 * j:128 * (j + 1)] - mx[j])
            lj = jnp.sum(ppj + pcj + jnp.where(m_head == j, pm, 0.0), axis=-1, keepdims=True)
            lj = lj + jnp.exp(sink_ref[layer, 4 * g + j] - mx[j])
            pp.append(ppj.astype(BF16))
            pc.append(pcj.astype(BF16))
            linv.append(1.0 / lj)
        o = (_dot(jnp.concatenate(pp, axis=1), prev["vb"][g]())
             + _dot(jnp.concatenate(pc, axis=1), vb_cur)
             + _dot(pm.astype(BF16), vbm_ref[g]))
        seg = lax.shift_right_logical(lax.broadcasted_iota(jnp.int32, (BLOCK, 256), 1), 6)
        lfull = jnp.where(seg == 0, linv[0],
                          jnp.where(seg == 1, linv[1],
                                    jnp.where(seg == 2, linv[2], linv[3])))
        att.append(o * lfull)
    y_att = _rms(jnp.concatenate(att, axis=1), aog_ref[...])

    tick()
    u = pf_ref[:, 0:256]
    ub = jnp.concatenate([prev["tail"](), u], axis=0)
    base = CONV_TAIL - (CONV_K - 1)
    nrow = CONV_TAIL + BLOCK
    acc = None
    for r in range(SUBLANES):
        taps = [t for t in range(CONV_K) if (base + t) % SUBLANES == r]
        xr = ub if r == 0 else pltpu.roll(ub, nrow - r, axis=0)
        for t in taps:
            a = base + t - r
            term = dww_ref[t:t + 1, :] * xr[a:a + BLOCK, :]
            acc = term if acc is None else acc + term
    acc = acc + dwb_ref[...]
    mu = jnp.mean(acc, axis=-1, keepdims=True)
    dc = acc - mu
    var = jnp.mean(dc * dc, axis=-1, keepdims=True)
    cn = dc * lax.rsqrt(var + LN_EPS) * lng_ref[...] + lnb_ref[...]
    cs = cn * _sigmoid(cn)
    y_cv = _rms(_dot(cs.astype(BF16), pw_ref[...]), cvg_ref[...])

    tick()
    rq = pb_ref[:, PB_RQ:PB_RQ + 256]
    rkt = pt_ref[128:384, :]
    kb_r = _diag_blocks([rkt[HEAD_DIM * hh:HEAD_DIM * (hh + 1), :] for hh in range(RET_HEADS)],
                        RET_HEADS)
    vb_r = _row_blocks(pb_ref[:, PB_RV:PB_RV + 256], hm_ref)
    scores = _dot(rq, kb_r) * dec_ref[...]
    state = prev["state"]()
    y_r = _dot(scores.astype(BF16), vb_r) + _dot(rq, state.astype(BF16)) * xi_ref[...]
    kv = _dot(rkt, pb_ref[:, PB_RVZ:PB_RVZ + 256])
    r2 = lax.shift_right_logical(lax.broadcasted_iota(jnp.int32, (256, 256), 0), 6)
    c2 = lax.shift_right_logical(lax.broadcasted_iota(jnp.int32, (256, 256), 1), 6)
    state_new = cdec_ref[...] * state + jnp.where(r2 == c2, kv, 0.0)
    dr = y_r - _head_mean(y_r)
    var_r = _head_mean(dr * dr)
    yn = dr * lax.rsqrt(var_r + LN_EPS) * gng_ref[...]
    y_ret = pf_ref[:, 256:512] * yn

    tick()
    o_ref[...] = jnp.concatenate([y_att, y_cv, y_ret], axis=1).astype(BF16)
    return dict(kb=kb_out, vb=vb_out, tail=u[BLOCK - CONV_TAIL:BLOCK, :], state=state_new)


def _mixer(pb, pf, pt, l, sinks, bias, hm, aog, dww, dwb, lng, lnb, pw, cvg,
           dec, xi, cdec, gng, state_in, batch, nblk, first_block):
    rows = pb.shape[0]
    d = aog.shape[2] + cvg.shape[2] + gng.shape[2]
    pos0 = 0 if first_block else 1
    bps = 2 if nblk % 2 == 0 else 1
    steps = nblk // bps
    tb = bps * BLOCK

    def rowblk(b, n):
        return (b * steps + n, 0)

    def colblk(b, n):
        return (0, b * steps + n)

    def lspec(a):
        return _layer_spec(a, l)

    def bias_spec(k):
        return pl.BlockSpec((1,) + bias.shape[1:], lambda b, n: (jnp.minimum(pos0 + n * bps + k, 2), 0, 0))

    kern = functools.partial(_mixer_kernel, layer=l, first_block=first_block, bps=bps)
    state_shapes = tuple(jax.ShapeDtypeStruct(a.shape, a.dtype) for a in state_in)
    return pl.pallas_call(
        kern,
        out_shape=(jax.ShapeDtypeStruct((rows, d), BF16),) + state_shapes,
        grid=(batch, steps),
        in_specs=[
            pl.BlockSpec(memory_space=pltpu.SMEM),
            pl.BlockSpec((tb, pb.shape[1]), rowblk),
            pl.BlockSpec((tb, pf.shape[1]), rowblk),
            pl.BlockSpec((pt.shape[0], tb), colblk),
            bias_spec(0), bias_spec(bps - 1),
            _const_spec(hm), lspec(aog),
            lspec(dww), lspec(dwb), lspec(lng), lspec(lnb), lspec(pw), lspec(cvg),
            _const_spec(dec), _const_spec(xi), _const_spec(cdec), lspec(gng),
        ] + [_const_spec(a) for a in state_in],
        out_specs=(pl.BlockSpec((tb, d), rowblk),) + tuple(_const_spec(a) for a in state_in),
        scratch_shapes=[
            pltpu.VMEM((4, 256, 4 * BLOCK), BF16),
            pltpu.VMEM((4, 4 * BLOCK, 256), BF16),
            pltpu.VMEM((2, 256, BLOCK), BF16),
            pltpu.VMEM((2, BLOCK, 256), BF16),
            pltpu.VMEM((2, CONV_TAIL, 256), F32),
            pltpu.VMEM((2, 256, 256), F32),
        ],
        compiler_params=pltpu.CompilerParams(
            dimension_semantics=("arbitrary", "arbitrary"), vmem_limit_bytes=VMEM_LIMIT),
        name="mixer",
    )(sinks, pb, pf, pt, bias, bias, hm, aog, dww, dwb, lng, lnb, pw, cvg,
      dec, xi, cdec, gng, *state_in)


def _mixer_state_zeros():
    return (jnp.zeros((2, 256, BLOCK), BF16), jnp.zeros((2, BLOCK, 256), BF16),
            jnp.zeros((CONV_TAIL, 256), F32), jnp.zeros((256, 256), F32))


def _fused_kernel(sink_ref, h_ref, h0_ref, g_ref, w_ref, qg_ref, kg_ref, zeta_ref,
                  bias_a_ref, bias_b_ref, hm_ref, aog_ref,
                  dww_ref, dwb_ref, lng_ref, lnb_ref, pw_ref, cvg_ref,
                  dec_ref, xi_ref, cdec_ref, gng_ref,
                  kbm_in_ref, vbm_in_ref, tail_in_ref, state_in_ref,
                  o_ref,
                  kbp_ref, vbp_ref, kbm_ref, vbm_ref, tail_ref, state_ref,
                  pbs_ref, pfs_ref, pts_ref, *, layer, bps):
    b = pl.program_id(0)
    n = pl.program_id(1)
    tb = bps * BLOCK

    def project(rows_ref, slot):
        u = _rms(rows_ref[...], g_ref[...]).astype(BF16)
        yield
        bounds = (Q0, Q0 + 256, K0, CA0, RQ0, RK0, RV0, RG0, IN_END)
        ahead = {bounds[0]: _dot(u, w_ref[:, bounds[0]:bounds[1]])}
        yield

        def proj(lo, hi):
            i = bounds.index(hi)
            if i + 1 < len(bounds):
                ahead[hi] = _dot(u, w_ref[:, hi:bounds[i + 1]])
            return ahead.pop(lo)

        yield from _proj_groups(proj, qg_ref, kg_ref, zeta_ref,
                                pbs_ref.at[slot], pfs_ref.at[slot], pts_ref.at[slot], tm=tb, pad_rows=0)

    @pl.when(jnp.logical_and(b == 0, n == 0))
    def _first_operands():
        for _ in project(h0_ref, 0):
            pass

    @pl.when(n == 0)
    def _init():
        kbp_ref[...] = jnp.zeros_like(kbp_ref)
        vbp_ref[...] = jnp.zeros_like(vbp_ref)
        tail_ref[1] = tail_in_ref[...]
        state_ref[1] = state_in_ref[...]
        kbm_ref[...] = kbm_in_ref[...]
        vbm_ref[...] = vbm_in_ref[...]

    for cur_slot in range(2):
        nxt_slot = 1 - cur_slot

        @pl.when(lax.rem(n, 2) == cur_slot)
        def _step():
            pieces = project(h_ref, nxt_slot)

            ticks = [0]

            def tick():
                if ticks[0] % 3 == 0:
                    next(pieces, None)
                ticks[0] += 1

            prev = dict(kb=[lambda g=g: kbp_ref[2 * nxt_slot + g] for g in range(2)],
                        vb=[lambda g=g: vbp_ref[2 * nxt_slot + g] for g in range(2)],
                        tail=lambda: tail_ref[nxt_slot], state=lambda: state_ref[nxt_slot])
            for k in range(bps):
                cur = _mixer_block(sink_ref, pbs_ref.at[cur_slot], pfs_ref.at[cur_slot], pts_ref.at[cur_slot],
                                   bias_a_ref if k == 0 else bias_b_ref, hm_ref,
                                   aog_ref, dww_ref, dwb_ref, lng_ref, lnb_ref, pw_ref, cvg_ref,
                                   dec_ref, xi_ref, cdec_ref, gng_ref, o_ref, kbm_ref, vbm_ref,
                                   prev, layer=layer, row0=k * BLOCK, tick=tick)
                prev = dict(kb=[lambda g=g, c=cur: c["kb"][g] for g in range(2)],
                            vb=[lambda g=g, c=cur: c["vb"][g] for g in range(2)],
                            tail=lambda c=cur: c["tail"], state=lambda c=cur: c["state"])
            for _ in pieces:
                pass
            for g in range(2):
                kbp_ref[2 * cur_slot + g] = cur["kb"][g]
                vbp_ref[2 * cur_slot + g] = cur["vb"][g]
            tail_ref[cur_slot] = cur["tail"]
            state_ref[cur_slot] = cur["state"]


def _fused(h2, l, g, w, qg, kg, zeta_tile, sinks, bias, hm, aog, dww, dwb, lng, lnb, pw, cvg,
           dec, xi, cdec, gng, state_in, batch, nblk):
    rows, d = h2.shape
    dout = aog.shape[2] + cvg.shape[2] + gng.shape[2]
    bps = 2
    steps = nblk // bps
    assert steps % 2 == 0
    tb = bps * BLOCK
    last = batch * steps - 1

    def rowblk(b, n):
        return (b * steps + n, 0)

    def lspec(a):
        return _layer_spec(a, l)

    def bias_spec(k):
        return pl.BlockSpec((1,) + bias.shape[1:], lambda b, n: (jnp.minimum(1 + n * bps + k, 2), 0, 0))

    kern = functools.partial(_fused_kernel, layer=l, bps=bps)
    return pl.pallas_call(
        kern,
        out_shape=jax.ShapeDtypeStruct((rows, dout), BF16),
        grid=(batch, steps),
        in_specs=[
            pl.BlockSpec(memory_space=pltpu.SMEM),
            pl.BlockSpec((tb, d), lambda b, n: (jnp.minimum(b * steps + n + 1, last), 0)),
            pl.BlockSpec((tb, d), lambda b, n: (0, 0)),
            lspec(g), lspec(w), lspec(qg), lspec(kg), _const_spec(zeta_tile),
            bias_spec(0), bias_spec(1),
            _const_spec(hm), lspec(aog),
            lspec(dww), lspec(dwb), lspec(lng), lspec(lnb), lspec(pw), lspec(cvg),
            _const_spec(dec), _const_spec(xi), _const_spec(cdec), lspec(gng),
        ] + [_const_spec(a) for a in state_in],
        out_specs=pl.BlockSpec((tb, dout), rowblk),
        scratch_shapes=[
            pltpu.VMEM((4, 256, 4 * BLOCK), BF16),
            pltpu.VMEM((4, 4 * BLOCK, 256), BF16),
            pltpu.VMEM((2, 256, BLOCK), BF16),
            pltpu.VMEM((2, BLOCK, 256), BF16),
            pltpu.VMEM((2, CONV_TAIL, 256), F32),
            pltpu.VMEM((2, 256, 256), F32),
            pltpu.VMEM((2, tb, PB_END), BF16),
            pltpu.VMEM((2, tb, 512), F32),
            pltpu.VMEM((2, 384, tb), BF16),
        ],
        compiler_params=pltpu.CompilerParams(
            dimension_semantics=("arbitrary", "arbitrary"), vmem_limit_bytes=VMEM_LIMIT),
        name="proj_mixer",
    )(sinks, h2, h2, g, w, qg, kg, zeta_tile, bias, bias, hm, aog, dww, dwb, lng, lnb, pw, cvg,
      dec, xi, cdec, gng, *state_in)


def _ffn_kernel(h_ref, y_ref, wout_ref, g_ref, up_ref, dww_ref, dwb_ref, down_ref, carry_in_ref,
                o_ref, carry_out_ref, carry_ref, act_ref, *, tm, dff, cw, pad_rows):
    t = pl.program_id(1)
    nchunk = dff // cw
    wr = lax.rem(t, 2)
    rd = 1 - wr

    @pl.when(t == 0)
    def _init():
        carry_ref[1] = carry_in_ref[...]

    def keep(y):
        if not pad_rows:
            return y
        row = t * tm + lax.broadcasted_iota(jnp.int32, (tm, 1), 0)
        return jnp.where(row >= pad_rows, y, 0.0)

    h = h_ref[...] + keep(_dot(y_ref[...], wout_ref[...]))
    u = _rms(h, g_ref[...]).astype(BF16)
    for c in range(nchunk):
        halves = []
        for off in (c * cw, dff + c * cw):
            f = _dot(u, up_ref[:, off:off + cw])
            fe = jnp.concatenate([carry_ref[rd, :, off:off + cw], f], axis=0)
            carry_ref[wr, :, off:off + cw] = f[tm - FFN_TAIL:tm, :]
            fc = dwb_ref[:, off:off + cw] + dww_ref[FFN_CONV_K - 1:FFN_CONV_K, off:off + cw] * f
            for s in range(1, FFN_CONV_K):
                kk = FFN_CONV_K - 1 - s
                shifted = pltpu.roll(fe, s, axis=0)[FFN_TAIL:FFN_TAIL + tm, :]
                fc = fc + dww_ref[kk:kk + 1, off:off + cw] * shifted
            halves.append(fc)
        act_ref[:, c * cw:(c + 1) * cw] = (halves[0] * _sigmoid(halves[0]) * halves[1]).astype(BF16)
    o_ref[...] = h + keep(_dot(act_ref[...], down_ref[...]))

    @pl.when(t == pl.num_programs(1) - 1)
    def _hand_over():
        carry_out_ref[...] = carry_ref[wr]


def _ffn(h2, ycat, l, wout, g, up, dww, dwb, down, carry_in, batch, tiles, tm, cw, pad_rows):
    rows, d = h2.shape
    dff = down.shape[1]

    def rowblk(b, t):
        return (b * tiles + t, 0)

    def wspec(a):
        nd = a.ndim - 1
        return pl.BlockSpec((None,) + a.shape[1:], lambda b, t: (l,) + (0,) * nd,
                            pipeline_mode=pl.Buffered(1))

    kern = functools.partial(_ffn_kernel, tm=tm, dff=dff, cw=cw, pad_rows=pad_rows)
    return pl.pallas_call(
        kern,
        out_shape=(jax.ShapeDtypeStruct((rows, d), F32),
                   jax.ShapeDtypeStruct(carry_in.shape, F32)),
        grid=(batch, tiles),
        in_specs=[
            pl.BlockSpec((tm, d), rowblk),
            pl.BlockSpec((tm, d), rowblk),
            wspec(wout), wspec(g), wspec(up), wspec(dww), wspec(dwb), wspec(down),
            _const_spec(carry_in),
        ],
        out_specs=(pl.BlockSpec((tm, d), rowblk), _const_spec(carry_in)),
        scratch_shapes=[
            pltpu.VMEM((2, FFN_TAIL, 2 * dff), F32),
            pltpu.VMEM((tm, dff), BF16),
        ],
        compiler_params=pltpu.CompilerParams(
            dimension_semantics=("arbitrary", "arbitrary"), vmem_limit_bytes=VMEM_LIMIT),
        name="ffn",
    )(h2, ycat, wout, g, up, dww, dwb, down, carry_in)


def _retention_tables():
    log_gamma = jnp.log1p(-jnp.exp2(-5.0 - jnp.arange(RET_HEADS, dtype=F32)))
    idx = jnp.arange(BLOCK, dtype=F32)
    diff = idx[:, None] - idx[None, :]
    decay = jnp.where(diff[None] >= 0,
                      jnp.exp(jnp.maximum(diff, 0.0)[None] * log_gamma[:, None, None]), 0.0)
    zeta = jnp.exp((BLOCK - 1 - idx)[None, :] * log_gamma[:, None])
    xi = jnp.exp((idx + 1.0)[None, :] * log_gamma[:, None])
    chunk_decay = jnp.exp(BLOCK * log_gamma)
    scale = RET_DK ** -0.5
    dec_t = jnp.transpose(decay, (1, 0, 2)).reshape(BLOCK, RET_HEADS * BLOCK) * scale
    zeta_t = jnp.repeat(zeta.T, RET_DK, axis=1) * scale
    xi_t = jnp.repeat(xi.T, RET_DK, axis=1)
    cdec_t = jnp.repeat(chunk_decay, RET_DK)[None, :]
    return dec_t, zeta_t, xi_t, cdec_t


def _attention_bias():
    n_kv = N_Q_HEADS // GQA_GROUP
    slopes = jnp.exp2(-8.0 * jnp.arange(1, N_Q_HEADS + 1, dtype=F32) / N_Q_HEADS).reshape(n_kv, GQA_GROUP)
    n = jnp.arange(3)[:, None, None]
    i = jnp.arange(BLOCK)[None, :, None]
    c = jnp.arange(BLOCK)[None, None, :]

    def head_tiles(ok, dist):
        b = jnp.where(ok[:, :, None, None, :],
                      -slopes[None, None, :, :, None] * dist.astype(F32)[:, :, None, None, :], NEG)
        return b.reshape(3, BLOCK, n_kv, GQA_GROUP * BLOCK)

    prev = head_tiles(jnp.logical_and(c > i, n >= 2), jnp.broadcast_to(i - c + BLOCK, (3, BLOCK, BLOCK)))
    cur = head_tiles(jnp.logical_and(c <= i, n >= 1), jnp.broadcast_to(i - c, (3, BLOCK, BLOCK)))
    d_meta = n * BLOCK + i - PAD - (c % N_META)
    ok_meta = jnp.logical_and(d_meta >= 0, c < GQA_GROUP * N_META)
    sl_meta = slopes[:, jnp.minimum(jnp.arange(BLOCK) // N_META, GQA_GROUP - 1)]
    meta = jnp.where(ok_meta[:, :, None, :],
                     -sl_meta[None, None] * jnp.minimum(d_meta, WINDOW).astype(F32)[:, :, None, :], NEG)
    return jnp.concatenate([prev, cur, meta], axis=-1).reshape(3, BLOCK, -1)


def kernel(x, meta, norm_mix_g, w_in, q_norm_g, k_norm_g, attn_sinks, attn_out_g, cv_dw_w, cv_dw_b, cv_ln_g, cv_ln_b, cv_pw, cv_out_g, ret_gn_g, w_out, norm_ffn_g, ffn_up, ffn_dw_w, ffn_dw_b, ffn_down):
    batch, seq, d = x.shape
    depth = w_in.shape[0]
    dff = ffn_down.shape[1]

    dec_t, zeta_t, xi_t, cdec_t = _retention_tables()
    bias = _attention_bias()
    seg = jnp.arange(256) // HEAD_DIM
    hm = jnp.broadcast_to((seg[None, :] == jnp.arange(4)[:, None])[:, None, :], (4, BLOCK, 256)).astype(BF16)

    def vec(a):
        return a.reshape(depth, 1, a.shape[-1])

    w_in_b, w_out_b, up_b, down_b, pw_b = (a.astype(BF16) for a in (w_in, w_out, ffn_up, ffn_down, cv_pw))
    mix_g, ffn_g = vec(norm_mix_g), vec(norm_ffn_g)
    qg, kg = vec(jnp.tile(q_norm_g, (1, N_Q_HEADS))), vec(jnp.tile(k_norm_g, (1, N_Q_HEADS // GQA_GROUP)))
    aog, dwb, lng, lnb, cvg, gng, fdwb = (vec(a) for a in (
        attn_out_g, cv_dw_b, cv_ln_g, cv_ln_b, cv_out_g, ret_gn_g, ffn_dw_b))

    h0 = jnp.concatenate([jnp.zeros((PAD, d), x.dtype), meta.astype(x.dtype)], axis=0)
    hr = x.reshape(batch * seq, d)
    nblk = seq // BLOCK
    in_tm = 2 * BLOCK
    ffn_tm = 512
    zeta_tile = jnp.tile(zeta_t, (in_tm // BLOCK, 1))

    def layer(h2, l, state_in, carry_in, nb, blocks, tm_in, tm_ffn, zt, first):
        pad_rows = PAD if first else 0
        if first:
            pb, pf, pt = _inproj(h2, l, mix_g, w_in_b, qg, kg, zt, tm_in, pad_rows)
            ycat, *state = _mixer(pb, pf, pt, l, attn_sinks, bias, hm, aog, cv_dw_w, dwb, lng, lnb,
                                  pw_b, cvg, dec_t, xi_t, cdec_t, gng, state_in, nb, blocks, first)
        else:
            ycat = _fused(h2, l, mix_g, w_in_b, qg, kg, zt, attn_sinks, bias, hm, aog, cv_dw_w, dwb, lng, lnb,
                          pw_b, cvg, dec_t, xi_t, cdec_t, gng, state_in, nb, blocks)
            state = ()
        h2, carry = _ffn(h2, ycat, l, w_out_b, ffn_g, up_b, ffn_dw_w, fdwb, down_b, carry_in,
                         nb, blocks * BLOCK // tm_ffn, tm_ffn, 256, pad_rows)
        return h2, tuple(state), carry

    for l in range(depth):
        h0, state0, carry0 = layer(h0, l, _mixer_state_zeros(), jnp.zeros((FFN_TAIL, 2 * dff), F32),
                                   1, 1, BLOCK, BLOCK, zeta_t, True)
        hr, _, _ = layer(hr, l, state0, carry0, batch, nblk, in_tm, ffn_tm, zeta_tile, False)
    return hr.reshape(batch, seq, d)
```

```python
import functools

import jax
import jax.numpy as jnp
from jax import lax
from jax.experimental import pallas as pl
from jax.experimental.pallas import tpu as pltpu

F32 = jnp.float32
BF16 = jnp.bfloat16

HEAD_DIM = 64
BLOCK = 128
WINDOW = 128
N_META = 16
PAD = BLOCK - N_META
N_Q_HEADS = 8
GQA_GROUP = 4
CONV_K = 31
FFN_CONV_K = 3
RET_DK = 64
RET_HEADS = 4
RMS_EPS = 1e-6
LN_EPS = 1e-5
NEG = -1e30

Q0, K0, V0, CA0, CB0, RQ0, RK0, RV0, RG0, IN_END = 0, 512, 640, 768, 1024, 1280, 1536, 1792, 2048, 2304
PB_Q, PB_V, PB_RQ, PB_RV, PB_RVZ, PB_END = 0, 512, 768, 1024, 1280, 1536

SUBLANES = 8
CONV_TAIL = 32
FFN_TAIL = SUBLANES
VMEM_LIMIT = 56 * 1024 * 1024


def _dot(a, b):
    return jnp.dot(a, b, preferred_element_type=F32)


def _sigmoid(x):
    return 1.0 / (1.0 + jnp.exp(-x))


def _rms(x, g):
    return x * lax.rsqrt(jnp.mean(x * x, axis=-1, keepdims=True) + RMS_EPS) * g


def _head_mean(x):
    low = lax.broadcasted_iota(jnp.int32, (x.shape[0], 2 * HEAD_DIM), 1) < HEAD_DIM
    cols = []
    for c in range(x.shape[1] // (2 * HEAD_DIM)):
        xs = x[:, 2 * HEAD_DIM * c:2 * HEAD_DIM * (c + 1)]
        lo = jnp.where(low, xs, 0.0)
        s_lo = jnp.sum(lo, axis=-1, keepdims=True)
        s_hi = jnp.sum(xs - lo, axis=-1, keepdims=True)
        cols.append(jnp.where(low, s_lo, s_hi))
    return jnp.concatenate(cols, axis=1) * (1.0 / HEAD_DIM)


def _layer_spec(a, l):
    nd = a.ndim - 1
    return pl.BlockSpec((None,) + a.shape[1:], lambda *_: (l,) + (0,) * nd)


def _const_spec(a):
    return pl.BlockSpec(a.shape, lambda *_: (0,) * a.ndim)


def _inproj_kernel(h_ref, g_ref, w_ref, qg_ref, kg_ref, zeta_ref,
                   pb_ref, pf_ref, pt_ref, p0_ref, p1_ref, *, tm, pad_rows):
    i = pl.program_id(0)

    @pl.when(i == 0)
    def _init():
        p1_ref[...] = jnp.zeros_like(p1_ref)

    @pl.when(lax.rem(i, 2) == 0)
    def _even():
        _inproj_step(h_ref, g_ref, w_ref, qg_ref, kg_ref, zeta_ref,
                     pb_ref, pf_ref, pt_ref, p0_ref, p1_ref, tm=tm, pad_rows=pad_rows)

    @pl.when(lax.rem(i, 2) == 1)
    def _odd():
        _inproj_step(h_ref, g_ref, w_ref, qg_ref, kg_ref, zeta_ref,
                     pb_ref, pf_ref, pt_ref, p1_ref, p0_ref, tm=tm, pad_rows=pad_rows)


def _inproj_step(h_ref, g_ref, w_ref, qg_ref, kg_ref, zeta_ref,
                 pb_ref, pf_ref, pt_ref, pw_ref, pr_ref, *, tm, pad_rows):
    u = _rms(h_ref[...], g_ref[...]).astype(BF16)
    pw_ref[...] = _dot(u, w_ref[...])
    for _ in _proj_groups(lambda lo, hi: pr_ref[:, lo:hi], qg_ref, kg_ref, zeta_ref,
                          pb_ref, pf_ref, pt_ref, tm=tm, pad_rows=pad_rows):
        pass


def _proj_groups(proj, qg_ref, kg_ref, zeta_ref, pb_ref, pf_ref, pt_ref, *, tm, pad_rows):
    for half in range(2):
        lo = Q0 + 256 * half
        q = proj(lo, lo + 256)
        qn = q * lax.rsqrt(_head_mean(q * q) + RMS_EPS) * qg_ref[:, 256 * half:256 * (half + 1)]
        pb_ref[:, PB_Q + 256 * half:PB_Q + 256 * (half + 1)] = (qn * (HEAD_DIM ** -0.5)).astype(BF16)
        yield

    kv = proj(K0, CA0)
    k = kv[:, 0:128]
    v = kv[:, 128:256]
    kn = k * lax.rsqrt(_head_mean(k * k) + RMS_EPS) * kg_ref[...]
    pt_ref[0:128, :] = kn.T.astype(BF16)
    low = lax.broadcasted_iota(jnp.int32, v.shape, 1) < HEAD_DIM
    v_sw = pltpu.roll(v, HEAD_DIM, axis=1)
    pb_ref[:, PB_V:PB_V + 128] = jnp.where(low, v, v_sw).astype(BF16)
    pb_ref[:, PB_V + 128:PB_V + 256] = jnp.where(low, v_sw, v).astype(BF16)
    yield

    cab = proj(CA0, RQ0)
    pf_ref[:, 0:256] = cab[:, 0:256] * _sigmoid(cab[:, 256:512])
    yield

    pb_ref[:, PB_RQ:PB_RQ + 256] = proj(RQ0, RK0).astype(BF16)
    yield
    rk = proj(RK0, RV0)
    if pad_rows:
        row = lax.broadcasted_iota(jnp.int32, (tm, 1), 0)
        rk = jnp.where(row < pad_rows, 0.0, rk)
    pt_ref[128:384, :] = rk.T.astype(BF16)
    yield
    rv = proj(RV0, RG0)
    pb_ref[:, PB_RV:PB_RV + 256] = rv.astype(BF16)
    pb_ref[:, PB_RVZ:PB_RVZ + 256] = (rv * zeta_ref[...]).astype(BF16)
    yield
    rg = proj(RG0, IN_END)
    pf_ref[:, 256:512] = rg * _sigmoid(rg)
    yield


def _inproj(h2, l, g, w, qg, kg, zeta_tile, tm, pad_rows):
    rows, d = h2.shape
    ntiles = rows // tm
    nout = w.shape[2]
    kern = functools.partial(_inproj_kernel, tm=tm, pad_rows=pad_rows)
    return pl.pallas_call(
        kern,
        out_shape=(jax.ShapeDtypeStruct((rows, PB_END), BF16),
                   jax.ShapeDtypeStruct((rows, 512), F32),
                   jax.ShapeDtypeStruct((384, rows), BF16)),
        grid=(ntiles + 1,),
        in_specs=[
            pl.BlockSpec((tm, d), lambda i: (jnp.minimum(i, ntiles - 1), 0)),
            _layer_spec(g, l), _layer_spec(w, l), _layer_spec(qg, l), _layer_spec(kg, l),
            _const_spec(zeta_tile),
        ],
        out_specs=(pl.BlockSpec((tm, PB_END), lambda i: (jnp.maximum(i - 1, 0), 0)),
                   pl.BlockSpec((tm, 512), lambda i: (jnp.maximum(i - 1, 0), 0)),
                   pl.BlockSpec((384, tm), lambda i: (0, jnp.maximum(i - 1, 0)))),
        scratch_shapes=[pltpu.VMEM((tm, nout), F32)] * 2,
        compiler_params=pltpu.CompilerParams(
            dimension_semantics=("arbitrary",), vmem_limit_bytes=VMEM_LIMIT),
        name="in_proj",
    )(h2, g, w, qg, kg, zeta_tile)


def _row_blocks(x, hm_ref):
    r = x.shape[0]
    return jnp.concatenate([x * hm_ref[j, 0:r, :] for j in range(4)], axis=0)


def _diag_blocks(xt, heads):
    z = jnp.zeros_like(xt[0])
    return jnp.concatenate(
        [jnp.concatenate([xt[j] if i == j else z for i in range(heads)], axis=1)
         for j in range(heads)], axis=0)


def _mixer_kernel(sink_ref, pb_ref, pf_ref, pt_ref, bias_a_ref, bias_b_ref, hm_ref, aog_ref,
                  dww_ref, dwb_ref, lng_ref, lnb_ref, pw_ref, cvg_ref,
                  dec_ref, xi_ref, cdec_ref, gng_ref,
                  kbm_in_ref, vbm_in_ref, tail_in_ref, state_in_ref,
                  o_ref, kbm_out_ref, vbm_out_ref, tail_out_ref, state_out_ref,
                  kbp_ref, vbp_ref, kbm_ref, vbm_ref, tail_ref, state_ref, *, layer, first_block, bps):
    n = pl.program_id(1)

    @pl.when(n == 0)
    def _init():
        kbp_ref[...] = jnp.zeros_like(kbp_ref)
        vbp_ref[...] = jnp.zeros_like(vbp_ref)
        if first_block:
            tail_ref[1] = jnp.zeros(tail_ref.shape[1:], F32)
            state_ref[1] = jnp.zeros(state_ref.shape[1:], F32)
            for g in range(2):
                ktf = pt_ref[HEAD_DIM * g:HEAD_DIM * (g + 1), :].astype(F32)
                tok = lax.broadcasted_iota(jnp.int32, ktf.shape, 1)
                ktm = jnp.where(tok >= PAD, ktf, 0.0)
                kbm_ref[g] = jnp.concatenate(
                    [pltpu.roll(ktm, N_META * (j + 1), axis=1) for j in range(4)], axis=0).astype(BF16)
                vm = pb_ref[PAD:BLOCK, PB_V + 128 * g:PB_V + 128 * (g + 1)]
                z = jnp.zeros((BLOCK - 4 * N_META, 256), BF16)
                vbm_ref[g] = jnp.concatenate(
                    [_row_blocks(jnp.concatenate([vm, vm], axis=1), hm_ref), z], axis=0)
        else:
            tail_ref[1] = tail_in_ref[...]
            state_ref[1] = state_in_ref[...]
            kbm_ref[...] = kbm_in_ref[...]
            vbm_ref[...] = vbm_in_ref[...]

    for wr in range(2):
        rd = 1 - wr

        @pl.when(lax.rem(n, 2) == wr)
        def _step():
            prev = dict(kb=[lambda g=g: kbp_ref[2 * rd + g] for g in range(2)],
                        vb=[lambda g=g: vbp_ref[2 * rd + g] for g in range(2)],
                        tail=lambda: tail_ref[rd], state=lambda: state_ref[rd])
            for k in range(bps):
                cur = _mixer_block(sink_ref, pb_ref, pf_ref, pt_ref, bias_a_ref if k == 0 else bias_b_ref, hm_ref,
                                   aog_ref, dww_ref, dwb_ref, lng_ref, lnb_ref, pw_ref, cvg_ref,
                                   dec_ref, xi_ref, cdec_ref, gng_ref, o_ref, kbm_ref, vbm_ref,
                                   prev, layer=layer, row0=k * BLOCK)
                prev = dict(kb=[lambda g=g, c=cur: c["kb"][g] for g in range(2)],
                            vb=[lambda g=g, c=cur: c["vb"][g] for g in range(2)],
                            tail=lambda c=cur: c["tail"], state=lambda c=cur: c["state"])
            for g in range(2):
                kbp_ref[2 * wr + g] = cur["kb"][g]
                vbp_ref[2 * wr + g] = cur["vb"][g]
            tail_ref[wr] = cur["tail"]
            state_ref[wr] = cur["state"]

    @pl.when(n == pl.num_programs(1) - 1)
    def _hand_over():
        kbm_out_ref[...] = kbm_ref[...]
        vbm_out_ref[...] = vbm_ref[...]
        last = (pl.num_programs(1) - 1) % 2
        tail_out_ref[...] = tail_ref[last]
        state_out_ref[...] = state_ref[last]


def _mixer_block(sink_ref, pb_ref, pf_ref, pt_ref, bias_ref, hm_ref, aog_ref,
                 dww_ref, dwb_ref, lng_ref, lnb_ref, pw_ref, cvg_ref,
                 dec_ref, xi_ref, cdec_ref, gng_ref, o_ref, kbm_ref, vbm_ref,
                 prev, *, layer, row0, tick=lambda: None):
    pb_ref = pb_ref.at[row0:row0 + BLOCK, :]
    pf_ref = pf_ref.at[row0:row0 + BLOCK, :]
    pt_ref = pt_ref.at[:, row0:row0 + BLOCK]
    m_head = lax.shift_right_logical(lax.broadcasted_iota(jnp.int32, (BLOCK, BLOCK), 1), 4)

    att, kb_out, vb_out = [], [], []
    for g in range(2):
        tick()
        kt = pt_ref[HEAD_DIM * g:HEAD_DIM * (g + 1), :]
        vg = pb_ref[:, PB_V + 128 * g:PB_V + 128 * (g + 1)]
        vrep = jnp.concatenate([vg, vg], axis=1)
        kb_cur = _diag_blocks([kt] * 4, 4)
        vb_cur = _row_blocks(vrep, hm_ref)
        kb_out.append(kb_cur)
        vb_out.append(vb_cur)

        qg = pb_ref[:, PB_Q + 256 * g:PB_Q + 256 * (g + 1)]
        gb = 1152 * g
        s_prev = _dot(qg, prev["kb"][g]()) + bias_ref[0, :, gb:gb + 512]
        s_cur = _dot(qg, kb_cur) + bias_ref[0, :, gb + 512:gb + 1024]
        sm = _dot(qg, kbm_ref[g]) + bias_ref[0, :, gb + 1024:gb + 1152]

        mx = []
        for j in range(4):
            t = jnp.maximum(jnp.maximum(s_prev[:, 128 * j:128 * (j + 1)], s_cur[:, 128 * j:128 * (j + 1)]),
                            jnp.where(m_head == j, sm, NEG))
            mx.append(jnp.maximum(jnp.max(t, axis=-1, keepdims=True), sink_ref[layer, 4 * g + j]))
        m_meta = jnp.where(m_head == 0, mx[0],
                           jnp.where(m_head == 1, mx[1],
                                     jnp.where(m_head == 2, mx[2], mx[3])))
        pm = jnp.exp(sm - m_meta)
        pp, pc, linv = [], [], []
        for j in range(4):
            tick()
            ppj = jnp.exp(s_prev[:, 128 * j:128 * (j + 1)] - mx[j])
            pcj = jnp.exp(s_cur[:, 128 * j:128 * (j + 1)] - mx[j])
            lj = jnp.sum(ppj + pcj + jnp.where(m_head == j, pm, 0.0), axis=-1, keepdims=True)
            lj = lj + jnp.exp(sink_ref[layer, 4 * g + j] - mx[j])
            pp.append(ppj.astype(BF16))
            pc.append(pcj.astype(BF16))
            linv.append(1.0 / lj)
        o = (_dot(jnp.concatenate(pp, axis=1), prev["vb"][g]())
             + _dot(jnp.concatenate(pc, axis=1), vb_cur)
             + _dot(pm.astype(BF16), vbm_ref[g]))
        seg = lax.shift_right_logical(lax.broadcasted_iota(jnp.int32, (BLOCK, 256), 1), 6)
        lfull = jnp.where(seg == 0, linv[0],
                          jnp.where(seg == 1, linv[1],
                                    jnp.where(seg == 2, linv[2], linv[3])))
        att.append(o * lfull)
    y_att = _rms(jnp.concatenate(att, axis=1), aog_ref[...])

    tick()
    u = pf_ref[:, 0:256]
    ub = jnp.concatenate([prev["tail"](), u], axis=0)
    base = CONV_TAIL - (CONV_K - 1)
    nrow = CONV_TAIL + BLOCK
    acc = None
    for r in range(SUBLANES):
        taps = [t for t in range(CONV_K) if (base + t) % SUBLANES == r]
        xr = ub if r == 0 else pltpu.roll(ub, nrow - r, axis=0)
        for t in taps:
            a = base + t - r
            term = dww_ref[t:t + 1, :] * xr[a:a + BLOCK, :]
            acc = term if acc is None else acc + term
    acc = acc + dwb_ref[...]
    mu = jnp.mean(acc, axis=-1, keepdims=True)
    dc = acc - mu
    var = jnp.mean(dc * dc, axis=-1, keepdims=True)
    cn = dc * lax.rsqrt(var + LN_EPS) * lng_ref[...] + lnb_ref[...]
    cs = cn * _sigmoid(cn)
    y_cv = _rms(_dot(cs.astype(BF16), pw_ref[...]), cvg_ref[...])

    tick()
    rq = pb_ref[:, PB_RQ:PB_RQ + 256]
    rkt = pt_ref[128:384, :]
    kb_r = _diag_blocks([rkt[HEAD_DIM * hh:HEAD_DIM * (hh + 1), :] for hh in range(RET_HEADS)],
                        RET_HEADS)
    vb_r = _row_blocks(pb_ref[:, PB_RV:PB_RV + 256], hm_ref)
    scores = _dot(rq, kb_r) * dec_ref[...]
    state = prev["state"]()
    y_r = _dot(scores.astype(BF16), vb_r) + _dot(rq, state.astype(BF16)) * xi_ref[...]
    kv = _dot(rkt, pb_ref[:, PB_RVZ:PB_RVZ + 256])
    r2 = lax.shift_right_logical(lax.broadcasted_iota(jnp.int32, (256, 256), 0), 6)
    c2 = lax.shift_right_logical(lax.broadcasted_iota(jnp.int32, (256, 256), 1), 6)
    state_new = cdec_ref[...] * state + jnp.where(r2 == c2, kv, 0.0)
    dr = y_r - _head_mean(y_r)
    var_r = _head_mean(dr * dr)
    yn = dr * lax.rsqrt(var_r + LN_EPS) * gng_ref[...]
    y_ret = pf_ref[:, 256:512] * yn

    tick()
    y = jnp.concatenate([y_att, y_cv, y_ret], axis=1).astype(BF16)
    if o_ref is not None:
        o_ref[row0:row0 + BLOCK, :] = y
    return dict(kb=kb_out, vb=vb_out, tail=u[BLOCK - CONV_TAIL:BLOCK, :], state=state_new, y=y)


def _mixer(pb, pf, pt, l, sinks, bias, hm, aog, dww, dwb, lng, lnb, pw, cvg,
           dec, xi, cdec, gng, state_in, batch, nblk, first_block):
    rows = pb.shape[0]
    d = aog.shape[2] + cvg.shape[2] + gng.shape[2]
    pos0 = 0 if first_block else 1
    bps = 2 if nblk % 2 == 0 else 1
    steps = nblk // bps
    tb = bps * BLOCK

    def rowblk(b, n):
        return (b * steps + n, 0)

    def colblk(b, n):
        return (0, b * steps + n)

    def lspec(a):
        return _layer_spec(a, l)

    def bias_spec(k):
        return pl.BlockSpec((1,) + bias.shape[1:], lambda b, n: (jnp.minimum(pos0 + n * bps + k, 2), 0, 0))

    kern = functools.partial(_mixer_kernel, layer=l, first_block=first_block, bps=bps)
    state_shapes = tuple(jax.ShapeDtypeStruct(a.shape, a.dtype) for a in state_in)
    return pl.pallas_call(
        kern,
        out_shape=(jax.ShapeDtypeStruct((rows, d), BF16),) + state_shapes,
        grid=(batch, steps),
        in_specs=[
            pl.BlockSpec(memory_space=pltpu.SMEM),
            pl.BlockSpec((tb, pb.shape[1]), rowblk),
            pl.BlockSpec((tb, pf.shape[1]), rowblk),
            pl.BlockSpec((pt.shape[0], tb), colblk),
            bias_spec(0), bias_spec(bps - 1),
            _const_spec(hm), lspec(aog),
            lspec(dww), lspec(dwb), lspec(lng), lspec(lnb), lspec(pw), lspec(cvg),
            _const_spec(dec), _const_spec(xi), _const_spec(cdec), lspec(gng),
        ] + [_const_spec(a) for a in state_in],
        out_specs=(pl.BlockSpec((tb, d), rowblk),) + tuple(_const_spec(a) for a in state_in),
        scratch_shapes=[
            pltpu.VMEM((4, 256, 4 * BLOCK), BF16),
            pltpu.VMEM((4, 4 * BLOCK, 256), BF16),
            pltpu.VMEM((2, 256, BLOCK), BF16),
            pltpu.VMEM((2, BLOCK, 256), BF16),
            pltpu.VMEM((2, CONV_TAIL, 256), F32),
            pltpu.VMEM((2, 256, 256), F32),
        ],
        compiler_params=pltpu.CompilerParams(
            dimension_semantics=("arbitrary", "arbitrary"), vmem_limit_bytes=VMEM_LIMIT),
        name="mixer",
    )(sinks, pb, pf, pt, bias, bias, hm, aog, dww, dwb, lng, lnb, pw, cvg,
      dec, xi, cdec, gng, *state_in)


def _mixer_state_zeros():
    return (jnp.zeros((2, 256, BLOCK), BF16), jnp.zeros((2, BLOCK, 256), BF16),
            jnp.zeros((CONV_TAIL, 256), F32), jnp.zeros((256, 256), F32))


def _fused_kernel(sink_ref, h_ref, h0_ref, hcur_ref, wout_ref, g_ref, w_ref, qg_ref, kg_ref, zeta_ref,
                  bias_a_ref, bias_b_ref, hm_ref, aog_ref,
                  dww_ref, dwb_ref, lng_ref, lnb_ref, pw_ref, cvg_ref,
                  dec_ref, xi_ref, cdec_ref, gng_ref,
                  kbm_in_ref, vbm_in_ref, tail_in_ref, state_in_ref,
                  o_ref,
                  kbp_ref, vbp_ref, kbm_ref, vbm_ref, tail_ref, state_ref,
                  pbs_ref, pfs_ref, pts_ref, *, layer, bps):
    b = pl.program_id(0)
    n = pl.program_id(1)
    tb = bps * BLOCK

    def project(rows_ref, slot):
        u = _rms(rows_ref[...], g_ref[...]).astype(BF16)
        yield
        bounds = (Q0, Q0 + 256, K0, CA0, RQ0, RK0, RV0, RG0, IN_END)
        ahead = {bounds[0]: _dot(u, w_ref[:, bounds[0]:bounds[1]])}
        yield

        def proj(lo, hi):
            i = bounds.index(hi)
            if i + 1 < len(bounds):
                ahead[hi] = _dot(u, w_ref[:, hi:bounds[i + 1]])
            return ahead.pop(lo)

        yield from _proj_groups(proj, qg_ref, kg_ref, zeta_ref,
                                pbs_ref.at[slot], pfs_ref.at[slot], pts_ref.at[slot], tm=tb, pad_rows=0)

    @pl.when(jnp.logical_and(b == 0, n == 0))
    def _first_operands():
        for _ in project(h0_ref, 0):
            pass

    @pl.when(n == 0)
    def _init():
        kbp_ref[...] = jnp.zeros_like(kbp_ref)
        vbp_ref[...] = jnp.zeros_like(vbp_ref)
        tail_ref[1] = tail_in_ref[...]
        state_ref[1] = state_in_ref[...]
        kbm_ref[...] = kbm_in_ref[...]
        vbm_ref[...] = vbm_in_ref[...]

    for cur_slot in range(2):
        nxt_slot = 1 - cur_slot

        @pl.when(lax.rem(n, 2) == cur_slot)
        def _step():
            pieces = project(h_ref, nxt_slot)

            ticks = [0]

            def tick():
                if ticks[0] % 3 == 0:
                    next(pieces, None)
                ticks[0] += 1

            prev = dict(kb=[lambda g=g: kbp_ref[2 * nxt_slot + g] for g in range(2)],
                        vb=[lambda g=g: vbp_ref[2 * nxt_slot + g] for g in range(2)],
                        tail=lambda: tail_ref[nxt_slot], state=lambda: state_ref[nxt_slot])
            ys = []
            for k in range(bps):
                cur = _mixer_block(sink_ref, pbs_ref.at[cur_slot], pfs_ref.at[cur_slot], pts_ref.at[cur_slot],
                                   bias_a_ref if k == 0 else bias_b_ref, hm_ref,
                                   aog_ref, dww_ref, dwb_ref, lng_ref, lnb_ref, pw_ref, cvg_ref,
                                   dec_ref, xi_ref, cdec_ref, gng_ref, None, kbm_ref, vbm_ref,
                                   prev, layer=layer, row0=k * BLOCK, tick=tick)
                ys.append(cur["y"])
                prev = dict(kb=[lambda g=g, c=cur: c["kb"][g] for g in range(2)],
                            vb=[lambda g=g, c=cur: c["vb"][g] for g in range(2)],
                            tail=lambda c=cur: c["tail"], state=lambda c=cur: c["state"])
            for _ in pieces:
                pass
            o_ref[...] = hcur_ref[...] + _dot(jnp.concatenate(ys, axis=0), wout_ref[...])
            for g in range(2):
                kbp_ref[2 * cur_slot + g] = cur["kb"][g]
                vbp_ref[2 * cur_slot + g] = cur["vb"][g]
            tail_ref[cur_slot] = cur["tail"]
            state_ref[cur_slot] = cur["state"]


def _fused(h2, l, wout, g, w, qg, kg, zeta_tile, sinks, bias, hm, aog, dww, dwb, lng, lnb, pw, cvg,
           dec, xi, cdec, gng, state_in, batch, nblk):
    rows, d = h2.shape
    bps = 2
    steps = nblk // bps
    assert steps % 2 == 0
    tb = bps * BLOCK
    last = batch * steps - 1

    def rowblk(b, n):
        return (b * steps + n, 0)

    def lspec(a):
        return _layer_spec(a, l)

    def bias_spec(k):
        return pl.BlockSpec((1,) + bias.shape[1:], lambda b, n: (jnp.minimum(1 + n * bps + k, 2), 0, 0))

    kern = functools.partial(_fused_kernel, layer=l, bps=bps)
    return pl.pallas_call(
        kern,
        out_shape=jax.ShapeDtypeStruct((rows, d), F32),
        grid=(batch, steps),
        in_specs=[
            pl.BlockSpec(memory_space=pltpu.SMEM),
            pl.BlockSpec((tb, d), lambda b, n: (jnp.minimum(b * steps + n + 1, last), 0)),
            pl.BlockSpec((tb, d), lambda b, n: (0, 0)),
            pl.BlockSpec((tb, d), rowblk),
            lspec(wout), lspec(g), lspec(w), lspec(qg), lspec(kg), _const_spec(zeta_tile),
            bias_spec(0), bias_spec(1),
            _const_spec(hm), lspec(aog),
            lspec(dww), lspec(dwb), lspec(lng), lspec(lnb), lspec(pw), lspec(cvg),
            _const_spec(dec), _const_spec(xi), _const_spec(cdec), lspec(gng),
        ] + [_const_spec(a) for a in state_in],
        out_specs=pl.BlockSpec((tb, d), rowblk),
        scratch_shapes=[
            pltpu.VMEM((4, 256, 4 * BLOCK), BF16),
            pltpu.VMEM((4, 4 * BLOCK, 256), BF16),
            pltpu.VMEM((2, 256, BLOCK), BF16),
            pltpu.VMEM((2, BLOCK, 256), BF16),
            pltpu.VMEM((2, CONV_TAIL, 256), F32),
            pltpu.VMEM((2, 256, 256), F32),
            pltpu.VMEM((2, tb, PB_END), BF16),
            pltpu.VMEM((2, tb, 512), F32),
            pltpu.VMEM((2, 384, tb), BF16),
        ],
        compiler_params=pltpu.CompilerParams(
            dimension_semantics=("arbitrary", "arbitrary"), vmem_limit_bytes=VMEM_LIMIT),
        name="proj_mixer",
    )(sinks, h2, h2, h2, wout, g, w, qg, kg, zeta_tile, bias, bias, hm, aog, dww, dwb, lng, lnb, pw, cvg,
      dec, xi, cdec, gng, *state_in)


def _ffn_kernel(*refs, tm, dff, cw, pad_rows, outproj):
    if outproj:
        h_ref, y_ref, wout_ref, *refs = refs
    else:
        h_ref, *refs = refs
    g_ref, up_ref, dww_ref, dwb_ref, down_ref, carry_in_ref, o_ref, carry_out_ref, carry_ref, act_ref = refs
    t = pl.program_id(1)
    nchunk = dff // cw
    wr = lax.rem(t, 2)
    rd = 1 - wr

    @pl.when(t == 0)
    def _init():
        carry_ref[1] = carry_in_ref[...]

    def keep(y):
        if not pad_rows:
            return y
        row = t * tm + lax.broadcasted_iota(jnp.int32, (tm, 1), 0)
        return jnp.where(row >= pad_rows, y, 0.0)

    h = h_ref[...]
    if outproj:
        h = h + keep(_dot(y_ref[...], wout_ref[...]))
    u = _rms(h, g_ref[...]).astype(BF16)
    for c in range(nchunk):
        halves = []
        for off in (c * cw, dff + c * cw):
            f = _dot(u, up_ref[:, off:off + cw])
            fe = jnp.concatenate([carry_ref[rd, :, off:off + cw], f], axis=0)
            carry_ref[wr, :, off:off + cw] = f[tm - FFN_TAIL:tm, :]
            fc = dwb_ref[:, off:off + cw] + dww_ref[FFN_CONV_K - 1:FFN_CONV_K, off:off + cw] * f
            for s in range(1, FFN_CONV_K):
                kk = FFN_CONV_K - 1 - s
                shifted = pltpu.roll(fe, s, axis=0)[FFN_TAIL:FFN_TAIL + tm, :]
                fc = fc + dww_ref[kk:kk + 1, off:off + cw] * shifted
            halves.append(fc)
        act_ref[:, c * cw:(c + 1) * cw] = (halves[0] * _sigmoid(halves[0]) * halves[1]).astype(BF16)
    o_ref[...] = h + keep(_dot(act_ref[...], down_ref[...]))

    @pl.when(t == pl.num_programs(1) - 1)
    def _hand_over():
        carry_out_ref[...] = carry_ref[wr]


def _ffn(h2, ycat, l, wout, g, up, dww, dwb, down, carry_in, batch, tiles, tm, cw, pad_rows):
    outproj = ycat is not None
    rows, d = h2.shape
    dff = down.shape[1]

    def rowblk(b, t):
        return (b * tiles + t, 0)

    def wspec(a):
        nd = a.ndim - 1
        return pl.BlockSpec((None,) + a.shape[1:], lambda b, t: (l,) + (0,) * nd,
                            pipeline_mode=pl.Buffered(1))

    kern = functools.partial(_ffn_kernel, tm=tm, dff=dff, cw=cw, pad_rows=pad_rows, outproj=outproj)
    head_specs = [pl.BlockSpec((tm, d), rowblk)]
    head_args = [h2]
    if outproj:
        head_specs += [pl.BlockSpec((tm, d), rowblk), wspec(wout)]
        head_args += [ycat, wout]
    return pl.pallas_call(
        kern,
        out_shape=(jax.ShapeDtypeStruct((rows, d), F32),
                   jax.ShapeDtypeStruct(carry_in.shape, F32)),
        grid=(batch, tiles),
        in_specs=head_specs + [
            wspec(g), wspec(up), wspec(dww), wspec(dwb), wspec(down),
            _const_spec(carry_in),
        ],
        out_specs=(pl.BlockSpec((tm, d), rowblk), _const_spec(carry_in)),
        scratch_shapes=[
            pltpu.VMEM((2, FFN_TAIL, 2 * dff), F32),
            pltpu.VMEM((tm, dff), BF16),
        ],
        compiler_params=pltpu.CompilerParams(
            dimension_semantics=("arbitrary", "arbitrary"), vmem_limit_bytes=VMEM_LIMIT),
        name="ffn",
    )(*head_args, g, up, dww, dwb, down, carry_in)


def _retention_tables():
    log_gamma = jnp.log1p(-jnp.exp2(-5.0 - jnp.arange(RET_HEADS, dtype=F32)))
    idx = jnp.arange(BLOCK, dtype=F32)
    diff = idx[:, None] - idx[None, :]
    decay = jnp.where(diff[None] >= 0,
                      jnp.exp(jnp.maximum(diff, 0.0)[None] * log_gamma[:, None, None]), 0.0)
    zeta = jnp.exp((BLOCK - 1 - idx)[None, :] * log_gamma[:, None])
    xi = jnp.exp((idx + 1.0)[None, :] * log_gamma[:, None])
    chunk_decay = jnp.exp(BLOCK * log_gamma)
    scale = RET_DK ** -0.5
    dec_t = jnp.transpose(decay, (1, 0, 2)).reshape(BLOCK, RET_HEADS * BLOCK) * scale
    zeta_t = jnp.repeat(zeta.T, RET_DK, axis=1) * scale
    xi_t = jnp.repeat(xi.T, RET_DK, axis=1)
    cdec_t = jnp.repeat(chunk_decay, RET_DK)[None, :]
    return dec_t, zeta_t, xi_t, cdec_t


def _attention_bias():
    n_kv = N_Q_HEADS // GQA_GROUP
    slopes = jnp.exp2(-8.0 * jnp.arange(1, N_Q_HEADS + 1, dtype=F32) / N_Q_HEADS).reshape(n_kv, GQA_GROUP)
    n = jnp.arange(3)[:, None, None]
    i = jnp.arange(BLOCK)[None, :, None]
    c = jnp.arange(BLOCK)[None, None, :]

    def head_tiles(ok, dist):
        b = jnp.where(ok[:, :, None, None, :],
                      -slopes[None, None, :, :, None] * dist.astype(F32)[:, :, None, None, :], NEG)
        return b.reshape(3, BLOCK, n_kv, GQA_GROUP * BLOCK)

    prev = head_tiles(jnp.logical_and(c > i, n >= 2), jnp.broadcast_to(i - c + BLOCK, (3, BLOCK, BLOCK)))
    cur = head_tiles(jnp.logical_and(c <= i, n >= 1), jnp.broadcast_to(i - c, (3, BLOCK, BLOCK)))
    d_meta = n * BLOCK + i - PAD - (c % N_META)
    ok_meta = jnp.logical_and(d_meta >= 0, c < GQA_GROUP * N_META)
    sl_meta = slopes[:, jnp.minimum(jnp.arange(BLOCK) // N_META, GQA_GROUP - 1)]
    meta = jnp.where(ok_meta[:, :, None, :],
                     -sl_meta[None, None] * jnp.minimum(d_meta, WINDOW).astype(F32)[:, :, None, :], NEG)
    return jnp.concatenate([prev, cur, meta], axis=-1).reshape(3, BLOCK, -1)


def kernel(x, meta, norm_mix_g, w_in, q_norm_g, k_norm_g, attn_sinks, attn_out_g, cv_dw_w, cv_dw_b, cv_ln_g, cv_ln_b, cv_pw, cv_out_g, ret_gn_g, w_out, norm_ffn_g, ffn_up, ffn_dw_w, ffn_dw_b, ffn_down):
    batch, seq, d = x.shape
    depth = w_in.shape[0]
    dff = ffn_down.shape[1]

    dec_t, zeta_t, xi_t, cdec_t = _retention_tables()
    bias = _attention_bias()
    seg = jnp.arange(256) // HEAD_DIM
    hm = jnp.broadcast_to((seg[None, :] == jnp.arange(4)[:, None])[:, None, :], (4, BLOCK, 256)).astype(BF16)

    def vec(a):
        return a.reshape(depth, 1, a.shape[-1])

    w_in_b, w_out_b, up_b, down_b, pw_b = (a.astype(BF16) for a in (w_in, w_out, ffn_up, ffn_down, cv_pw))
    mix_g, ffn_g = vec(norm_mix_g), vec(norm_ffn_g)
    qg, kg = vec(jnp.tile(q_norm_g, (1, N_Q_HEADS))), vec(jnp.tile(k_norm_g, (1, N_Q_HEADS // GQA_GROUP)))
    aog, dwb, lng, lnb, cvg, gng, fdwb = (vec(a) for a in (
        attn_out_g, cv_dw_b, cv_ln_g, cv_ln_b, cv_out_g, ret_gn_g, ffn_dw_b))

    h0 = jnp.concatenate([jnp.zeros((PAD, d), x.dtype), meta.astype(x.dtype)], axis=0)
    hr = x.reshape(batch * seq, d)
    nblk = seq // BLOCK
    in_tm = 2 * BLOCK
    ffn_tm = 512
    zeta_tile = jnp.tile(zeta_t, (in_tm // BLOCK, 1))

    def layer(h2, l, state_in, carry_in, nb, blocks, tm_in, tm_ffn, zt, first):
        pad_rows = PAD if first else 0
        if first:
            pb, pf, pt = _inproj(h2, l, mix_g, w_in_b, qg, kg, zt, tm_in, pad_rows)
            ycat, *state = _mixer(pb, pf, pt, l, attn_sinks, bias, hm, aog, cv_dw_w, dwb, lng, lnb,
                                  pw_b, cvg, dec_t, xi_t, cdec_t, gng, state_in, nb, blocks, first)
        else:
            h2 = _fused(h2, l, w_out_b, mix_g, w_in_b, qg, kg, zt, attn_sinks, bias, hm, aog, cv_dw_w, dwb, lng, lnb,
                          pw_b, cvg, dec_t, xi_t, cdec_t, gng, state_in, nb, blocks)
            ycat, state = None, ()
        h2, carry = _ffn(h2, ycat, l, w_out_b, ffn_g, up_b, ffn_dw_w, fdwb, down_b, carry_in,
                         nb, blocks * BLOCK // tm_ffn, tm_ffn, 256, pad_rows)
        return h2, tuple(state), carry

    for l in range(depth):
        h0, state0, carry0 = layer(h0, l, _mixer_state_zeros(), jnp.zeros((FFN_TAIL, 2 * dff), F32),
                                   1, 1, BLOCK, BLOCK, zeta_t, True)
        hr, _, _ = layer(hr, l, state0, carry0, batch, nblk, in_tm, ffn_tm, zeta_tile, False)
    return hr.reshape(batch, seq, d)
```

```python
import functools

import jax
import jax.numpy as jnp
from jax import lax
from jax.experimental import pallas as pl
from jax.experimental.pallas import tpu as pltpu

F32 = jnp.float32
BF16 = jnp.bfloat16

HEAD_DIM = 64
BLOCK = 128
WINDOW = 128
N_META = 16
PAD = BLOCK - N_META
N_Q_HEADS = 8
GQA_GROUP = 4
CONV_K = 31
FFN_CONV_K = 3
RET_DK = 64
RET_HEADS = 4
RMS_EPS = 1e-6
LN_EPS = 1e-5
NEG = -1e30

Q0, K0, V0, CA0, CB0, RQ0, RK0, RV0, RG0, IN_END = 0, 512, 640, 768, 1024, 1280, 1536, 1792, 2048, 2304
PB_Q, PB_V, PB_RQ, PB_RV, PB_RVZ, PB_END = 0, 512, 768, 1024, 1280, 1536

SUBLANES = 8
CONV_TAIL = 32
FFN_TAIL = SUBLANES
VMEM_LIMIT = 56 * 1024 * 1024


def _dot(a, b):
    return jnp.dot(a, b, preferred_element_type=F32)


def _sigmoid(x):
    return 1.0 / (1.0 + jnp.exp(-x))


def _rms(x, g):
    return x * lax.rsqrt(jnp.mean(x * x, axis=-1, keepdims=True) + RMS_EPS) * g


def _head_mean(x):
    low = lax.broadcasted_iota(jnp.int32, (x.shape[0], 2 * HEAD_DIM), 1) < HEAD_DIM
    cols = []
    for c in range(x.shape[1] // (2 * HEAD_DIM)):
        xs = x[:, 2 * HEAD_DIM * c:2 * HEAD_DIM * (c + 1)]
        lo = jnp.where(low, xs, 0.0)
        s_lo = jnp.sum(lo, axis=-1, keepdims=True)
        s_hi = jnp.sum(xs - lo, axis=-1, keepdims=True)
        cols.append(jnp.where(low, s_lo, s_hi))
    return jnp.concatenate(cols, axis=1) * (1.0 / HEAD_DIM)


def _layer_spec(a, l):
    nd = a.ndim - 1
    return pl.BlockSpec((None,) + a.shape[1:], lambda *_: (l,) + (0,) * nd)


def _const_spec(a):
    return pl.BlockSpec(a.shape, lambda *_: (0,) * a.ndim)


def _inproj_kernel(h_ref, g_ref, w_ref, qg_ref, kg_ref, zeta_ref,
                   pb_ref, pf_ref, pt_ref, p0_ref, p1_ref, *, tm, pad_rows):
    i = pl.program_id(0)

    @pl.when(i == 0)
    def _init():
        p1_ref[...] = jnp.zeros_like(p1_ref)

    @pl.when(lax.rem(i, 2) == 0)
    def _even():
        _inproj_step(h_ref, g_ref, w_ref, qg_ref, kg_ref, zeta_ref,
                     pb_ref, pf_ref, pt_ref, p0_ref, p1_ref, tm=tm, pad_rows=pad_rows)

    @pl.when(lax.rem(i, 2) == 1)
    def _odd():
        _inproj_step(h_ref, g_ref, w_ref, qg_ref, kg_ref, zeta_ref,
                     pb_ref, pf_ref, pt_ref, p1_ref, p0_ref, tm=tm, pad_rows=pad_rows)


def _inproj_step(h_ref, g_ref, w_ref, qg_ref, kg_ref, zeta_ref,
                 pb_ref, pf_ref, pt_ref, pw_ref, pr_ref, *, tm, pad_rows):
    u = _rms(h_ref[...], g_ref[...]).astype(BF16)
    pw_ref[...] = _dot(u, w_ref[...])
    for _ in _proj_groups(lambda lo, hi: pr_ref[:, lo:hi], qg_ref, kg_ref, zeta_ref,
                          pb_ref, pf_ref, pt_ref, tm=tm, pad_rows=pad_rows):
        pass


def _proj_groups(proj, qg_ref, kg_ref, zeta_ref, pb_ref, pf_ref, pt_ref, *, tm, pad_rows):
    for half in range(2):
        lo = Q0 + 256 * half
        q = proj(lo, lo + 256)
        qn = q * lax.rsqrt(_head_mean(q * q) + RMS_EPS) * qg_ref[:, 256 * half:256 * (half + 1)]
        pb_ref[:, PB_Q + 256 * half:PB_Q + 256 * (half + 1)] = qn.astype(BF16)
        yield

    kv = proj(K0, CA0)
    k = kv[:, 0:128]
    v = kv[:, 128:256]
    kn = k * lax.rsqrt(_head_mean(k * k) + RMS_EPS) * kg_ref[...]
    pt_ref[0:128, :] = kn.T.astype(BF16)
    low = lax.broadcasted_iota(jnp.int32, v.shape, 1) < HEAD_DIM
    v_sw = pltpu.roll(v, HEAD_DIM, axis=1)
    pb_ref[:, PB_V:PB_V + 128] = jnp.where(low, v, v_sw).astype(BF16)
    pb_ref[:, PB_V + 128:PB_V + 256] = jnp.where(low, v_sw, v).astype(BF16)
    yield

    cab = proj(CA0, RQ0)
    pf_ref[:, 0:256] = cab[:, 0:256] * _sigmoid(cab[:, 256:512])
    yield

    pb_ref[:, PB_RQ:PB_RQ + 256] = proj(RQ0, RK0).astype(BF16)
    yield
    rk = proj(RK0, RV0)
    if pad_rows:
        row = lax.broadcasted_iota(jnp.int32, (tm, 1), 0)
        rk = jnp.where(row < pad_rows, 0.0, rk)
    pt_ref[128:384, :] = rk.T.astype(BF16)
    yield
    rv = proj(RV0, RG0)
    pb_ref[:, PB_RV:PB_RV + 256] = rv.astype(BF16)
    pb_ref[:, PB_RVZ:PB_RVZ + 256] = (rv * zeta_ref[...]).astype(BF16)
    yield
    rg = proj(RG0, IN_END)
    pf_ref[:, 256:512] = rg * _sigmoid(rg)
    yield


def _inproj(h2, l, g, w, qg, kg, zeta_tile, tm, pad_rows):
    rows, d = h2.shape
    ntiles = rows // tm
    nout = w.shape[2]
    kern = functools.partial(_inproj_kernel, tm=tm, pad_rows=pad_rows)
    return pl.pallas_call(
        kern,
        out_shape=(jax.ShapeDtypeStruct((rows, PB_END), BF16),
                   jax.ShapeDtypeStruct((rows, 512), F32),
                   jax.ShapeDtypeStruct((384, rows), BF16)),
        grid=(ntiles + 1,),
        in_specs=[
            pl.BlockSpec((tm, d), lambda i: (jnp.minimum(i, ntiles - 1), 0)),
            _layer_spec(g, l), _layer_spec(w, l), _layer_spec(qg, l), _layer_spec(kg, l),
            _const_spec(zeta_tile),
        ],
        out_specs=(pl.BlockSpec((tm, PB_END), lambda i: (jnp.maximum(i - 1, 0), 0)),
                   pl.BlockSpec((tm, 512), lambda i: (jnp.maximum(i - 1, 0), 0)),
                   pl.BlockSpec((384, tm), lambda i: (0, jnp.maximum(i - 1, 0)))),
        scratch_shapes=[pltpu.VMEM((tm, nout), F32)] * 2,
        compiler_params=pltpu.CompilerParams(
            dimension_semantics=("arbitrary",), vmem_limit_bytes=VMEM_LIMIT),
        name="in_proj",
    )(h2, g, w, qg, kg, zeta_tile)


def _row_blocks(x, hm_ref):
    r = x.shape[0]
    return jnp.concatenate([x * hm_ref[j, 0:r, :] for j in range(4)], axis=0)


def _diag_blocks(xt, heads):
    z = jnp.zeros_like(xt[0])
    return jnp.concatenate(
        [jnp.concatenate([xt[j] if i == j else z for i in range(heads)], axis=1)
         for j in range(heads)], axis=0)


def _mixer_kernel(sink_ref, pb_ref, pf_ref, pt_ref, bias_a_ref, bias_b_ref, hm_ref, aog_ref,
                  dww_ref, dwb_ref, lng_ref, lnb_ref, pw_ref, cvg_ref,
                  dec_ref, xi_ref, cdec_ref, gng_ref,
                  kbm_in_ref, vbm_in_ref, tail_in_ref, state_in_ref,
                  o_ref, kbm_out_ref, vbm_out_ref, tail_out_ref, state_out_ref,
                  kbp_ref, vbp_ref, kbm_ref, vbm_ref, tail_ref, state_ref, *, layer, first_block, bps):
    n = pl.program_id(1)

    @pl.when(n == 0)
    def _init():
        kbp_ref[...] = jnp.zeros_like(kbp_ref)
        vbp_ref[...] = jnp.zeros_like(vbp_ref)
        if first_block:
            tail_ref[1] = jnp.zeros(tail_ref.shape[1:], F32)
            state_ref[1] = jnp.zeros(state_ref.shape[1:], F32)
            for g in range(2):
                ktf = pt_ref[HEAD_DIM * g:HEAD_DIM * (g + 1), :].astype(F32)
                tok = lax.broadcasted_iota(jnp.int32, ktf.shape, 1)
                ktm = jnp.where(tok >= PAD, ktf, 0.0)
                kbm_ref[g] = jnp.concatenate(
                    [pltpu.roll(ktm, N_META * (j + 1), axis=1) for j in range(4)], axis=0).astype(BF16)
                vm = pb_ref[PAD:BLOCK, PB_V + 128 * g:PB_V + 128 * (g + 1)]
                z = jnp.zeros((BLOCK - 4 * N_META, 256), BF16)
                vbm_ref[g] = jnp.concatenate(
                    [_row_blocks(jnp.concatenate([vm, vm], axis=1), hm_ref), z], axis=0)
        else:
            tail_ref[1] = tail_in_ref[...]
            state_ref[1] = state_in_ref[...]
            kbm_ref[...] = kbm_in_ref[...]
            vbm_ref[...] = vbm_in_ref[...]

    for wr in range(2):
        rd = 1 - wr

        @pl.when(lax.rem(n, 2) == wr)
        def _step():
            prev = dict(kb=[lambda g=g: kbp_ref[2 * rd + g] for g in range(2)],
                        vb=[lambda g=g: vbp_ref[2 * rd + g] for g in range(2)],
                        tail=lambda: tail_ref[rd], state=lambda: state_ref[rd])
            for k in range(bps):
                cur = _mixer_block(sink_ref, pb_ref, pf_ref, pt_ref, bias_a_ref if k == 0 else bias_b_ref, hm_ref,
                                   aog_ref, dww_ref, dwb_ref, lng_ref, lnb_ref, pw_ref, cvg_ref,
                                   dec_ref, xi_ref, cdec_ref, gng_ref, o_ref, kbm_ref, vbm_ref,
                                   prev, layer=layer, row0=k * BLOCK)
                prev = dict(kb=[lambda g=g, c=cur: c["kb"][g] for g in range(2)],
                            vb=[lambda g=g, c=cur: c["vb"][g] for g in range(2)],
                            tail=lambda c=cur: c["tail"], state=lambda c=cur: c["state"])
            for g in range(2):
                kbp_ref[2 * wr + g] = cur["kb"][g]
                vbp_ref[2 * wr + g] = cur["vb"][g]
            tail_ref[wr] = cur["tail"]
            state_ref[wr] = cur["state"]

    @pl.when(n == pl.num_programs(1) - 1)
    def _hand_over():
        kbm_out_ref[...] = kbm_ref[...]
        vbm_out_ref[...] = vbm_ref[...]
        last = (pl.num_programs(1) - 1) % 2
        tail_out_ref[...] = tail_ref[last]
        state_out_ref[...] = state_ref[last]


def _mixer_block(sink_ref, pb_ref, pf_ref, pt_ref, bias_ref, hm_ref, aog_ref,
                 dww_ref, dwb_ref, lng_ref, lnb_ref, pw_ref, cvg_ref,
                 dec_ref, xi_ref, cdec_ref, gng_ref, o_ref, kbm_ref, vbm_ref,
                 prev, *, layer, row0, tick=lambda: None):
    pb_ref = pb_ref.at[row0:row0 + BLOCK, :]
    pf_ref = pf_ref.at[row0:row0 + BLOCK, :]
    pt_ref = pt_ref.at[:, row0:row0 + BLOCK]
    o_ref = o_ref.at[row0:row0 + BLOCK, :]
    m_head = lax.shift_right_logical(lax.broadcasted_iota(jnp.int32, (BLOCK, BLOCK), 1), 4)

    att, kb_out, vb_out = [], [], []
    for g in range(2):
        tick()
        kt = pt_ref[HEAD_DIM * g:HEAD_DIM * (g + 1), :]
        vg = pb_ref[:, PB_V + 128 * g:PB_V + 128 * (g + 1)]
        vrep = jnp.concatenate([vg, vg], axis=1)
        kb_cur = _diag_blocks([kt] * 4, 4)
        vb_cur = _row_blocks(vrep, hm_ref)
        kb_out.append(kb_cur)
        vb_out.append(vb_cur)

        qg = pb_ref[:, PB_Q + 256 * g:PB_Q + 256 * (g + 1)]
        gb = 1152 * g
        s_prev = _dot(qg, prev["kb"][g]()) + bias_ref[0, :, gb:gb + 512]
        s_cur = _dot(qg, kb_cur) + bias_ref[0, :, gb + 512:gb + 1024]
        sm = _dot(qg, kbm_ref[g]) + bias_ref[0, :, gb + 1024:gb + 1152]

        mx = []
        for j in range(4):
            t = jnp.maximum(jnp.maximum(s_prev[:, 128 * j:128 * (j + 1)], s_cur[:, 128 * j:128 * (j + 1)]),
                            jnp.where(m_head == j, sm, NEG))
            mx.append(jnp.maximum(jnp.max(t, axis=-1, keepdims=True), sink_ref[layer, 4 * g + j]))
        m_meta = jnp.where(m_head == 0, mx[0],
                           jnp.where(m_head == 1, mx[1],
                                     jnp.where(m_head == 2, mx[2], mx[3])))
        pm = jnp.exp(sm - m_meta)
        pp, pc, linv = [], [], []
        for j in range(4):
            tick()
            ppj = jnp.exp(s_prev[:, 128 * j:128 * (j + 1)] - mx[j])
            pcj = jnp.exp(s_cur[:, 128 * j:128 * (j + 1)] - mx[j])
            lj = jnp.sum(ppj + pcj + jnp.where(m_head == j, pm, 0.0), axis=-1, keepdims=True)
            lj = lj + jnp.exp(sink_ref[layer, 4 * g + j] - mx[j])
            pp.append(ppj.astype(BF16))
            pc.append(pcj.astype(BF16))
            linv.append(1.0 / lj)
        o = (_dot(jnp.concatenate(pp, axis=1), prev["vb"][g]())
             + _dot(jnp.concatenate(pc, axis=1), vb_cur)
             + _dot(pm.astype(BF16), vbm_ref[g]))
        seg = lax.shift_right_logical(lax.broadcasted_iota(jnp.int32, (BLOCK, 256), 1), 6)
        lfull = jnp.where(seg == 0, linv[0],
                          jnp.where(seg == 1, linv[1],
                                    jnp.where(seg == 2, linv[2], linv[3])))
        att.append(o * lfull)
    y_att = _rms(jnp.concatenate(att, axis=1), aog_ref[...])

    tick()
    u = pf_ref[:, 0:256]
    ub = jnp.concatenate([prev["tail"](), u], axis=0)
    base = CONV_TAIL - (CONV_K - 1)
    nrow = CONV_TAIL + BLOCK
    acc = None
    for r in range(SUBLANES):
        taps = [t for t in range(CONV_K) if (base + t) % SUBLANES == r]
        xr = ub if r == 0 else pltpu.roll(ub, nrow - r, axis=0)
        for t in taps:
            a = base + t - r
            term = dww_ref[t:t + 1, :] * xr[a:a + BLOCK, :]
            acc = term if acc is None else acc + term
    acc = acc + dwb_ref[...]
    mu = jnp.mean(acc, axis=-1, keepdims=True)
    dc = acc - mu
    var = jnp.mean(dc * dc, axis=-1, keepdims=True)
    cn = dc * lax.rsqrt(var + LN_EPS) * lng_ref[...] + lnb_ref[...]
    cs = cn * _sigmoid(cn)
    y_cv = _rms(_dot(cs.astype(BF16), pw_ref[...]), cvg_ref[...])

    tick()
    rq = pb_ref[:, PB_RQ:PB_RQ + 256]
    rkt = pt_ref[128:384, :]
    kb_r = _diag_blocks([rkt[HEAD_DIM * hh:HEAD_DIM * (hh + 1), :] for hh in range(RET_HEADS)],
                        RET_HEADS)
    vb_r = _row_blocks(pb_ref[:, PB_RV:PB_RV + 256], hm_ref)
    scores = _dot(rq, kb_r) * dec_ref[...]
    state = prev["state"]()
    y_r = _dot(scores.astype(BF16), vb_r) + _dot(rq, state.astype(BF16)) * xi_ref[...]
    kv = _dot(rkt, pb_ref[:, PB_RVZ:PB_RVZ + 256])
    r2 = lax.shift_right_logical(lax.broadcasted_iota(jnp.int32, (256, 256), 0), 6)
    c2 = lax.shift_right_logical(lax.broadcasted_iota(jnp.int32, (256, 256), 1), 6)
    state_new = cdec_ref[...] * state + jnp.where(r2 == c2, kv, 0.0)
    dr = y_r - _head_mean(y_r)
    var_r = _head_mean(dr * dr)
    yn = dr * lax.rsqrt(var_r + LN_EPS) * gng_ref[...]
    y_ret = pf_ref[:, 256:512] * yn

    tick()
    o_ref[...] = jnp.concatenate([y_att, y_cv, y_ret], axis=1).astype(BF16)
    return dict(kb=kb_out, vb=vb_out, tail=u[BLOCK - CONV_TAIL:BLOCK, :], state=state_new)


def _mixer(pb, pf, pt, l, sinks, bias, hm, aog, dww, dwb, lng, lnb, pw, cvg,
           dec, xi, cdec, gng, state_in, batch, nblk, first_block):
    rows = pb.shape[0]
    d = aog.shape[2] + cvg.shape[2] + gng.shape[2]
    pos0 = 0 if first_block else 1
    bps = 2 if nblk % 2 == 0 else 1
    steps = nblk // bps
    tb = bps * BLOCK

    def rowblk(b, n):
        return (b * steps + n, 0)

    def colblk(b, n):
        return (0, b * steps + n)

    def lspec(a):
        return _layer_spec(a, l)

    def bias_spec(k):
        return pl.BlockSpec((1,) + bias.shape[1:], lambda b, n: (jnp.minimum(pos0 + n * bps + k, 2), 0, 0))

    kern = functools.partial(_mixer_kernel, layer=l, first_block=first_block, bps=bps)
    state_shapes = tuple(jax.ShapeDtypeStruct(a.shape, a.dtype) for a in state_in)
    return pl.pallas_call(
        kern,
        out_shape=(jax.ShapeDtypeStruct((rows, d), BF16),) + state_shapes,
        grid=(batch, steps),
        in_specs=[
            pl.BlockSpec(memory_space=pltpu.SMEM),
            pl.BlockSpec((tb, pb.shape[1]), rowblk),
            pl.BlockSpec((tb, pf.shape[1]), rowblk),
            pl.BlockSpec((pt.shape[0], tb), colblk),
            bias_spec(0), bias_spec(bps - 1),
            _const_spec(hm), lspec(aog),
            lspec(dww), lspec(dwb), lspec(lng), lspec(lnb), lspec(pw), lspec(cvg),
            _const_spec(dec), _const_spec(xi), _const_spec(cdec), lspec(gng),
        ] + [_const_spec(a) for a in state_in],
        out_specs=(pl.BlockSpec((tb, d), rowblk),) + tuple(_const_spec(a) for a in state_in),
        scratch_shapes=[
            pltpu.VMEM((4, 256, 4 * BLOCK), BF16),
            pltpu.VMEM((4, 4 * BLOCK, 256), BF16),
            pltpu.VMEM((2, 256, BLOCK), BF16),
            pltpu.VMEM((2, BLOCK, 256), BF16),
            pltpu.VMEM((2, CONV_TAIL, 256), F32),
            pltpu.VMEM((2, 256, 256), F32),
        ],
        compiler_params=pltpu.CompilerParams(
            dimension_semantics=("arbitrary", "arbitrary"), vmem_limit_bytes=VMEM_LIMIT),
        name="mixer",
    )(sinks, pb, pf, pt, bias, bias, hm, aog, dww, dwb, lng, lnb, pw, cvg,
      dec, xi, cdec, gng, *state_in)


def _mixer_state_zeros():
    return (jnp.zeros((2, 256, BLOCK), BF16), jnp.zeros((2, BLOCK, 256), BF16),
            jnp.zeros((CONV_TAIL, 256), F32), jnp.zeros((256, 256), F32))


def _fused_kernel(sink_ref, h_ref, h0_ref, g_ref, w_ref, qg_ref, kg_ref, zeta_ref,
                  bias_a_ref, bias_b_ref, hm_ref, aog_ref,
                  dww_ref, dwb_ref, lng_ref, lnb_ref, pw_ref, cvg_ref,
                  dec_ref, xi_ref, cdec_ref, gng_ref,
                  kbm_in_ref, vbm_in_ref, tail_in_ref, state_in_ref,
                  o_ref,
                  kbp_ref, vbp_ref, kbm_ref, vbm_ref, tail_ref, state_ref,
                  pbs_ref, pfs_ref, pts_ref, *, layer, bps):
    b = pl.program_id(0)
    n = pl.program_id(1)
    tb = bps * BLOCK

    def project(rows_ref, slot):
        u = _rms(rows_ref[...], g_ref[...]).astype(BF16)
        yield
        bounds = (Q0, Q0 + 256, K0, CA0, RQ0, RK0, RV0, RG0, IN_END)
        ahead = {bounds[0]: _dot(u, w_ref[:, bounds[0]:bounds[1]])}
        yield

        def proj(lo, hi):
            i = bounds.index(hi)
            if i + 1 < len(bounds):
                ahead[hi] = _dot(u, w_ref[:, hi:bounds[i + 1]])
            return ahead.pop(lo)

        yield from _proj_groups(proj, qg_ref, kg_ref, zeta_ref,
                                pbs_ref.at[slot], pfs_ref.at[slot], pts_ref.at[slot], tm=tb, pad_rows=0)

    @pl.when(jnp.logical_and(b == 0, n == 0))
    def _first_operands():
        for _ in project(h0_ref, 0):
            pass

    @pl.when(n == 0)
    def _init():
        kbp_ref[...] = jnp.zeros_like(kbp_ref)
        vbp_ref[...] = jnp.zeros_like(vbp_ref)
        tail_ref[1] = tail_in_ref[...]
        state_ref[1] = state_in_ref[...]
        kbm_ref[...] = kbm_in_ref[...]
        vbm_ref[...] = vbm_in_ref[...]

    for cur_slot in range(2):
        nxt_slot = 1 - cur_slot

        @pl.when(lax.rem(n, 2) == cur_slot)
        def _step():
            pieces = project(h_ref, nxt_slot)

            ticks = [0]

            def tick():
                if ticks[0] % 2 == 0:
                    next(pieces, None)
                ticks[0] += 1

            prev = dict(kb=[lambda g=g: kbp_ref[2 * nxt_slot + g] for g in range(2)],
                        vb=[lambda g=g: vbp_ref[2 * nxt_slot + g] for g in range(2)],
                        tail=lambda: tail_ref[nxt_slot], state=lambda: state_ref[nxt_slot])
            for k in range(bps):
                cur = _mixer_block(sink_ref, pbs_ref.at[cur_slot], pfs_ref.at[cur_slot], pts_ref.at[cur_slot],
                                   bias_a_ref if k == 0 else bias_b_ref, hm_ref,
                                   aog_ref, dww_ref, dwb_ref, lng_ref, lnb_ref, pw_ref, cvg_ref,
                                   dec_ref, xi_ref, cdec_ref, gng_ref, o_ref, kbm_ref, vbm_ref,
                                   prev, layer=layer, row0=k * BLOCK, tick=tick)
                prev = dict(kb=[lambda g=g, c=cur: c["kb"][g] for g in range(2)],
                            vb=[lambda g=g, c=cur: c["vb"][g] for g in range(2)],
                            tail=lambda c=cur: c["tail"], state=lambda c=cur: c["state"])
            for _ in pieces:
                pass
            for g in range(2):
                kbp_ref[2 * cur_slot + g] = cur["kb"][g]
                vbp_ref[2 * cur_slot + g] = cur["vb"][g]
            tail_ref[cur_slot] = cur["tail"]
            state_ref[cur_slot] = cur["state"]


def _fused(h2, l, g, w, qg, kg, zeta_tile, sinks, bias, hm, aog, dww, dwb, lng, lnb, pw, cvg,
           dec, xi, cdec, gng, state_in, batch, nblk):
    rows, d = h2.shape
    dout = aog.shape[2] + cvg.shape[2] + gng.shape[2]
    bps = 2
    steps = nblk // bps
    assert steps % 2 == 0
    tb = bps * BLOCK
    last = batch * steps - 1

    def rowblk(b, n):
        return (b * steps + n, 0)

    def lspec(a):
        return _layer_spec(a, l)

    def bias_spec(k):
        return pl.BlockSpec((1,) + bias.shape[1:], lambda b, n: (jnp.minimum(1 + n * bps + k, 2), 0, 0))

    kern = functools.partial(_fused_kernel, layer=l, bps=bps)
    return pl.pallas_call(
        kern,
        out_shape=jax.ShapeDtypeStruct((rows, dout), BF16),
        grid=(batch, steps),
        in_specs=[
            pl.BlockSpec(memory_space=pltpu.SMEM),
            pl.BlockSpec((tb, d), lambda b, n: (jnp.minimum(b * steps + n + 1, last), 0)),
            pl.BlockSpec((tb, d), lambda b, n: (0, 0)),
            lspec(g), lspec(w), lspec(qg), lspec(kg), _const_spec(zeta_tile),
            bias_spec(0), bias_spec(1),
            _const_spec(hm), lspec(aog),
            lspec(dww), lspec(dwb), lspec(lng), lspec(lnb), lspec(pw), lspec(cvg),
            _const_spec(dec), _const_spec(xi), _const_spec(cdec), lspec(gng),
        ] + [_const_spec(a) for a in state_in],
        out_specs=pl.BlockSpec((tb, dout), rowblk),
        scratch_shapes=[
            pltpu.VMEM((4, 256, 4 * BLOCK), BF16),
            pltpu.VMEM((4, 4 * BLOCK, 256), BF16),
            pltpu.VMEM((2, 256, BLOCK), BF16),
            pltpu.VMEM((2, BLOCK, 256), BF16),
            pltpu.VMEM((2, CONV_TAIL, 256), F32),
            pltpu.VMEM((2, 256, 256), F32),
            pltpu.VMEM((2, tb, PB_END), BF16),
            pltpu.VMEM((2, tb, 512), F32),
            pltpu.VMEM((2, 384, tb), BF16),
        ],
        compiler_params=pltpu.CompilerParams(
            dimension_semantics=("arbitrary", "arbitrary"), vmem_limit_bytes=VMEM_LIMIT),
        name="proj_mixer",
    )(sinks, h2, h2, g, w, qg, kg, zeta_tile, bias, bias, hm, aog, dww, dwb, lng, lnb, pw, cvg,
      dec, xi, cdec, gng, *state_in)


def _ffn_kernel(h_ref, y_ref, wout_ref, g_ref, up_ref, dww_ref, dwb_ref, down_ref, carry_in_ref,
                o_ref, carry_out_ref, carry_ref, act_ref, *, tm, dff, cw, pad_rows):
    t = pl.program_id(1)
    nchunk = dff // cw
    wr = lax.rem(t, 2)
    rd = 1 - wr

    @pl.when(t == 0)
    def _init():
        carry_ref[1] = carry_in_ref[...]

    def keep(y):
        if not pad_rows:
            return y
        row = t * tm + lax.broadcasted_iota(jnp.int32, (tm, 1), 0)
        return jnp.where(row >= pad_rows, y, 0.0)

    h = h_ref[...] + keep(_dot(y_ref[...], wout_ref[...]))
    u = _rms(h, g_ref[...]).astype(BF16)
    for c in range(nchunk):
        halves = []
        for off in (c * cw, dff + c * cw):
            f = _dot(u, up_ref[:, off:off + cw])
            fe = jnp.concatenate([carry_ref[rd, :, off:off + cw], f], axis=0)
            carry_ref[wr, :, off:off + cw] = f[tm - FFN_TAIL:tm, :]
            fc = dwb_ref[:, off:off + cw] + dww_ref[FFN_CONV_K - 1:FFN_CONV_K, off:off + cw] * f
            for s in range(1, FFN_CONV_K):
                kk = FFN_CONV_K - 1 - s
                shifted = pltpu.roll(fe, s, axis=0)[FFN_TAIL:FFN_TAIL + tm, :]
                fc = fc + dww_ref[kk:kk + 1, off:off + cw] * shifted
            halves.append(fc)
        act_ref[:, c * cw:(c + 1) * cw] = (halves[0] * _sigmoid(halves[0]) * halves[1]).astype(BF16)
    o_ref[...] = h + keep(_dot(act_ref[...], down_ref[...]))

    @pl.when(t == pl.num_programs(1) - 1)
    def _hand_over():
        carry_out_ref[...] = carry_ref[wr]


def _ffn(h2, ycat, l, wout, g, up, dww, dwb, down, carry_in, batch, tiles, tm, cw, pad_rows):
    rows, d = h2.shape
    dff = down.shape[1]

    def rowblk(b, t):
        return (b * tiles + t, 0)

    def wspec(a):
        nd = a.ndim - 1
        return pl.BlockSpec((None,) + a.shape[1:], lambda b, t: (l,) + (0,) * nd,
                            pipeline_mode=pl.Buffered(1))

    kern = functools.partial(_ffn_kernel, tm=tm, dff=dff, cw=cw, pad_rows=pad_rows)
    return pl.pallas_call(
        kern,
        out_shape=(jax.ShapeDtypeStruct((rows, d), F32),
                   jax.ShapeDtypeStruct(carry_in.shape, F32)),
        grid=(batch, tiles),
        in_specs=[
            pl.BlockSpec((tm, d), rowblk),
            pl.BlockSpec((tm, d), rowblk),
            wspec(wout), wspec(g), wspec(up), wspec(dww), wspec(dwb), wspec(down),
            _const_spec(carry_in),
        ],
        out_specs=(pl.BlockSpec((tm, d), rowblk), _const_spec(carry_in)),
        scratch_shapes=[
            pltpu.VMEM((2, FFN_TAIL, 2 * dff), F32),
            pltpu.VMEM((tm, dff), BF16),
        ],
        compiler_params=pltpu.CompilerParams(
            dimension_semantics=("arbitrary", "arbitrary"), vmem_limit_bytes=VMEM_LIMIT),
        name="ffn",
    )(h2, ycat, wout, g, up, dww, dwb, down, carry_in)


def _retention_tables():
    log_gamma = jnp.log1p(-jnp.exp2(-5.0 - jnp.arange(RET_HEADS, dtype=F32)))
    idx = jnp.arange(BLOCK, dtype=F32)
    diff = idx[:, None] - idx[None, :]
    decay = jnp.where(diff[None] >= 0,
                      jnp.exp(jnp.maximum(diff, 0.0)[None] * log_gamma[:, None, None]), 0.0)
    zeta = jnp.exp((BLOCK - 1 - idx)[None, :] * log_gamma[:, None])
    xi = jnp.exp((idx + 1.0)[None, :] * log_gamma[:, None])
    chunk_decay = jnp.exp(BLOCK * log_gamma)
    scale = RET_DK ** -0.5
    dec_t = jnp.transpose(decay, (1, 0, 2)).reshape(BLOCK, RET_HEADS * BLOCK) * scale
    zeta_t = jnp.repeat(zeta.T, RET_DK, axis=1) * scale
    xi_t = jnp.repeat(xi.T, RET_DK, axis=1)
    cdec_t = jnp.repeat(chunk_decay, RET_DK)[None, :]
    return dec_t, zeta_t, xi_t, cdec_t


def _attention_bias():
    n_kv = N_Q_HEADS // GQA_GROUP
    slopes = jnp.exp2(-8.0 * jnp.arange(1, N_Q_HEADS + 1, dtype=F32) / N_Q_HEADS).reshape(n_kv, GQA_GROUP)
    n = jnp.arange(3)[:, None, None]
    i = jnp.arange(BLOCK)[None, :, None]
    c = jnp.arange(BLOCK)[None, None, :]

    def head_tiles(ok, dist):
        b = jnp.where(ok[:, :, None, None, :],
                      -slopes[None, None, :, :, None] * dist.astype(F32)[:, :, None, None, :], NEG)
        return b.reshape(3, BLOCK, n_kv, GQA_GROUP * BLOCK)

    prev = head_tiles(jnp.logical_and(c > i, n >= 2), jnp.broadcast_to(i - c + BLOCK, (3, BLOCK, BLOCK)))
    cur = head_tiles(jnp.logical_and(c <= i, n >= 1), jnp.broadcast_to(i - c, (3, BLOCK, BLOCK)))
    d_meta = n * BLOCK + i - PAD - (c % N_META)
    ok_meta = jnp.logical_and(d_meta >= 0, c < GQA_GROUP * N_META)
    sl_meta = slopes[:, jnp.minimum(jnp.arange(BLOCK) // N_META, GQA_GROUP - 1)]
    meta = jnp.where(ok_meta[:, :, None, :],
                     -sl_meta[None, None] * jnp.minimum(d_meta, WINDOW).astype(F32)[:, :, None, :], NEG)
    return jnp.concatenate([prev, cur, meta], axis=-1).reshape(3, BLOCK, -1)


def kernel(x, meta, norm_mix_g, w_in, q_norm_g, k_norm_g, attn_sinks, attn_out_g, cv_dw_w, cv_dw_b, cv_ln_g, cv_ln_b, cv_pw, cv_out_g, ret_gn_g, w_out, norm_ffn_g, ffn_up, ffn_dw_w, ffn_dw_b, ffn_down):
    batch, seq, d = x.shape
    depth = w_in.shape[0]
    dff = ffn_down.shape[1]

    dec_t, zeta_t, xi_t, cdec_t = _retention_tables()
    bias = _attention_bias()
    seg = jnp.arange(256) // HEAD_DIM
    hm = jnp.broadcast_to((seg[None, :] == jnp.arange(4)[:, None])[:, None, :], (4, BLOCK, 256)).astype(BF16)

    def vec(a):
        return a.reshape(depth, 1, a.shape[-1])

    w_in_b, w_out_b, up_b, down_b, pw_b = (a.astype(BF16) for a in (w_in, w_out, ffn_up, ffn_down, cv_pw))
    mix_g, ffn_g = vec(norm_mix_g), vec(norm_ffn_g)
    qg = vec(jnp.tile(q_norm_g, (1, N_Q_HEADS)) * (HEAD_DIM ** -0.5))
    kg = vec(jnp.tile(k_norm_g, (1, N_Q_HEADS // GQA_GROUP)))
    aog, dwb, lng, lnb, cvg, gng, fdwb = (vec(a) for a in (
        attn_out_g, cv_dw_b, cv_ln_g, cv_ln_b, cv_out_g, ret_gn_g, ffn_dw_b))

    h0 = jnp.concatenate([jnp.zeros((PAD, d), x.dtype), meta.astype(x.dtype)], axis=0)
    hr = x.reshape(batch * seq, d)
    nblk = seq // BLOCK
    in_tm = 2 * BLOCK
    ffn_tm = 512
    zeta_tile = jnp.tile(zeta_t, (in_tm // BLOCK, 1))

    def layer(h2, l, state_in, carry_in, nb, blocks, tm_in, tm_ffn, zt, first):
        pad_rows = PAD if first else 0
        if first:
            pb, pf, pt = _inproj(h2, l, mix_g, w_in_b, qg, kg, zt, tm_in, pad_rows)
            ycat, *state = _mixer(pb, pf, pt, l, attn_sinks, bias, hm, aog, cv_dw_w, dwb, lng, lnb,
                                  pw_b, cvg, dec_t, xi_t, cdec_t, gng, state_in, nb, blocks, first)
        else:
            ycat = _fused(h2, l, mix_g, w_in_b, qg, kg, zt, attn_sinks, bias, hm, aog, cv_dw_w, dwb, lng, lnb,
                          pw_b, cvg, dec_t, xi_t, cdec_t, gng, state_in, nb, blocks)
            state = ()
        h2, carry = _ffn(h2, ycat, l, w_out_b, ffn_g, up_b, ffn_dw_w, fdwb, down_b, carry_in,
                         nb, blocks * BLOCK // tm_ffn, tm_ffn, 256, pad_rows)
        return h2, tuple(state), carry

    for l in range(depth):
        h0, state0, carry0 = layer(h0, l, _mixer_state_zeros(), jnp.zeros((FFN_TAIL, 2 * dff), F32),
                                   1, 1, BLOCK, BLOCK, zeta_t, True)
        hr, _, _ = layer(hr, l, state0, carry0, batch, nblk, in_tm, ffn_tm, zeta_tile, False)
    return hr.reshape(batch, seq, d)
```

```python
import functools

import jax
import jax.numpy as jnp
from jax import lax
from jax.experimental import pallas as pl
from jax.experimental.pallas import tpu as pltpu

F32 = jnp.float32
BF16 = jnp.bfloat16

HEAD_DIM = 64
BLOCK = 128
WINDOW = 128
N_META = 16
PAD = BLOCK - N_META
N_Q_HEADS = 8
GQA_GROUP = 4
CONV_K = 31
FFN_CONV_K = 3
RET_DK = 64
RET_HEADS = 4
RMS_EPS = 1e-6
LN_EPS = 1e-5
NEG = -1e30

Q0, K0, V0, CA0, CB0, RQ0, RK0, RV0, RG0, IN_END = 0, 512, 640, 768, 1024, 1280, 1536, 1792, 2048, 2304
PB_Q, PB_V, PB_RQ, PB_RV, PB_RVZ, PB_END = 0, 512, 768, 1024, 1280, 1536

SUBLANES = 8
CONV_TAIL = 32
FFN_TAIL = SUBLANES
VMEM_LIMIT = 56 * 1024 * 1024


def _dot(a, b):
    return jnp.dot(a, b, preferred_element_type=F32)


def _sigmoid(x):
    return 1.0 / (1.0 + jnp.exp(-x))


def _rms(x, g):
    return x * lax.rsqrt(jnp.mean(x * x, axis=-1, keepdims=True) + RMS_EPS) * g


def _head_mean(x):
    low = lax.broadcasted_iota(jnp.int32, (x.shape[0], 2 * HEAD_DIM), 1) < HEAD_DIM
    cols = []
    for c in range(x.shape[1] // (2 * HEAD_DIM)):
        xs = x[:, 2 * HEAD_DIM * c:2 * HEAD_DIM * (c + 1)]
        lo = jnp.where(low, xs, 0.0)
        s_lo = jnp.sum(lo, axis=-1, keepdims=True)
        s_hi = jnp.sum(xs - lo, axis=-1, keepdims=True)
        cols.append(jnp.where(low, s_lo, s_hi))
    return jnp.concatenate(cols, axis=1) * (1.0 / HEAD_DIM)


def _layer_spec(a, l):
    nd = a.ndim - 1
    return pl.BlockSpec((None,) + a.shape[1:], lambda *_: (l,) + (0,) * nd)


def _const_spec(a):
    return pl.BlockSpec(a.shape, lambda *_: (0,) * a.ndim)


def _inproj_kernel(h_ref, g_ref, w_ref, qg_ref, kg_ref, zeta_ref,
                   pb_ref, pf_ref, pt_ref, p0_ref, p1_ref, *, tm, pad_rows):
    i = pl.program_id(0)

    @pl.when(i == 0)
    def _init():
        p1_ref[...] = jnp.zeros_like(p1_ref)

    @pl.when(lax.rem(i, 2) == 0)
    def _even():
        _inproj_step(h_ref, g_ref, w_ref, qg_ref, kg_ref, zeta_ref,
                     pb_ref, pf_ref, pt_ref, p0_ref, p1_ref, tm=tm, pad_rows=pad_rows)

    @pl.when(lax.rem(i, 2) == 1)
    def _odd():
        _inproj_step(h_ref, g_ref, w_ref, qg_ref, kg_ref, zeta_ref,
                     pb_ref, pf_ref, pt_ref, p1_ref, p0_ref, tm=tm, pad_rows=pad_rows)


def _inproj_step(h_ref, g_ref, w_ref, qg_ref, kg_ref, zeta_ref,
                 pb_ref, pf_ref, pt_ref, pw_ref, pr_ref, *, tm, pad_rows):
    u = _rms(h_ref[...], g_ref[...]).astype(BF16)
    pw_ref[...] = _dot(u, w_ref[...])
    for _ in _proj_groups(lambda lo, hi: pr_ref[:, lo:hi], qg_ref, kg_ref, zeta_ref,
                          pb_ref, pf_ref, pt_ref, tm=tm, pad_rows=pad_rows):
        pass


def _proj_groups(proj, qg_ref, kg_ref, zeta_ref, pb_ref, pf_ref, pt_ref, *, tm, pad_rows):
    for half in range(2):
        lo = Q0 + 256 * half
        q = proj(lo, lo + 256)
        qn = q * lax.rsqrt(_head_mean(q * q) + RMS_EPS) * qg_ref[:, 256 * half:256 * (half + 1)]
        pb_ref[:, PB_Q + 256 * half:PB_Q + 256 * (half + 1)] = qn.astype(BF16)
        yield

    kv = proj(K0, CA0)
    k = kv[:, 0:128]
    v = kv[:, 128:256]
    kn = k * lax.rsqrt(_head_mean(k * k) + RMS_EPS) * kg_ref[...]
    pt_ref[0:128, :] = kn.T.astype(BF16)
    low = lax.broadcasted_iota(jnp.int32, v.shape, 1) < HEAD_DIM
    v_sw = pltpu.roll(v, HEAD_DIM, axis=1)
    pb_ref[:, PB_V:PB_V + 128] = jnp.where(low, v, v_sw).astype(BF16)
    pb_ref[:, PB_V + 128:PB_V + 256] = jnp.where(low, v_sw, v).astype(BF16)
    yield

    cab = proj(CA0, RQ0)
    pf_ref[:, 0:256] = cab[:, 0:256] * _sigmoid(cab[:, 256:512])
    yield

    pb_ref[:, PB_RQ:PB_RQ + 256] = proj(RQ0, RK0).astype(BF16)
    yield
    rk = proj(RK0, RV0)
    if pad_rows:
        row = lax.broadcasted_iota(jnp.int32, (tm, 1), 0)
        rk = jnp.where(row < pad_rows, 0.0, rk)
    pt_ref[128:384, :] = rk.T.astype(BF16)
    yield
    rv = proj(RV0, RG0)
    pb_ref[:, PB_RV:PB_RV + 256] = rv.astype(BF16)
    pb_ref[:, PB_RVZ:PB_RVZ + 256] = (rv * zeta_ref[...]).astype(BF16)
    yield
    rg = proj(RG0, IN_END)
    pf_ref[:, 256:512] = rg * _sigmoid(rg)
    yield


def _inproj(h2, l, g, w, qg, kg, zeta_tile, tm, pad_rows):
    rows, d = h2.shape
    ntiles = rows // tm
    nout = w.shape[2]
    kern = functools.partial(_inproj_kernel, tm=tm, pad_rows=pad_rows)
    return pl.pallas_call(
        kern,
        out_shape=(jax.ShapeDtypeStruct((rows, PB_END), BF16),
                   jax.ShapeDtypeStruct((rows, 512), F32),
                   jax.ShapeDtypeStruct((384, rows), BF16)),
        grid=(ntiles + 1,),
        in_specs=[
            pl.BlockSpec((tm, d), lambda i: (jnp.minimum(i, ntiles - 1), 0)),
            _layer_spec(g, l), _layer_spec(w, l), _layer_spec(qg, l), _layer_spec(kg, l),
            _const_spec(zeta_tile),
        ],
        out_specs=(pl.BlockSpec((tm, PB_END), lambda i: (jnp.maximum(i - 1, 0), 0)),
                   pl.BlockSpec((tm, 512), lambda i: (jnp.maximum(i - 1, 0), 0)),
                   pl.BlockSpec((384, tm), lambda i: (0, jnp.maximum(i - 1, 0)))),
        scratch_shapes=[pltpu.VMEM((tm, nout), F32)] * 2,
        compiler_params=pltpu.CompilerParams(
            dimension_semantics=("arbitrary",), vmem_limit_bytes=VMEM_LIMIT),
        name="in_proj",
    )(h2, g, w, qg, kg, zeta_tile)


def _row_blocks(x, hm_ref):
    r = x.shape[0]
    return jnp.concatenate([x * hm_ref[j, 0:r, :] for j in range(4)], axis=0)


def _diag_blocks(xt, heads):
    z = jnp.zeros_like(xt[0])
    return jnp.concatenate(
        [jnp.concatenate([xt[j] if i == j else z for i in range(heads)], axis=1)
         for j in range(heads)], axis=0)


def _mixer_kernel(sink_ref, pb_ref, pf_ref, pt_ref, bias_a_ref, bias_b_ref, hm_ref, aog_ref,
                  dww_ref, dwb_ref, lng_ref, lnb_ref, pw_ref, cvg_ref,
                  dec_ref, xi_ref, cdec_ref, gng_ref,
                  kbm_in_ref, vbm_in_ref, tail_in_ref, state_in_ref,
                  o_ref, kbm_out_ref, vbm_out_ref, tail_out_ref, state_out_ref,
                  kbp_ref, vbp_ref, kbm_ref, vbm_ref, tail_ref, state_ref, *, layer, first_block, bps):
    n = pl.program_id(1)

    @pl.when(n == 0)
    def _init():
        kbp_ref[...] = jnp.zeros_like(kbp_ref)
        vbp_ref[...] = jnp.zeros_like(vbp_ref)
        if first_block:
            tail_ref[1] = jnp.zeros(tail_ref.shape[1:], F32)
            state_ref[1] = jnp.zeros(state_ref.shape[1:], F32)
            for g in range(2):
                ktf = pt_ref[HEAD_DIM * g:HEAD_DIM * (g + 1), :].astype(F32)
                tok = lax.broadcasted_iota(jnp.int32, ktf.shape, 1)
                ktm = jnp.where(tok >= PAD, ktf, 0.0)
                kbm_ref[g] = jnp.concatenate(
                    [pltpu.roll(ktm, N_META * (j + 1), axis=1) for j in range(4)], axis=0).astype(BF16)
                vm = pb_ref[PAD:BLOCK, PB_V + 128 * g:PB_V + 128 * (g + 1)]
                z = jnp.zeros((BLOCK - 4 * N_META, 256), BF16)
                vbm_ref[g] = jnp.concatenate(
                    [_row_blocks(jnp.concatenate([vm, vm], axis=1), hm_ref), z], axis=0)
        else:
            tail_ref[1] = tail_in_ref[...]
            state_ref[1] = state_in_ref[...]
            kbm_ref[...] = kbm_in_ref[...]
            vbm_ref[...] = vbm_in_ref[...]

    for wr in range(2):
        rd = 1 - wr

        @pl.when(lax.rem(n, 2) == wr)
        def _step():
            prev = dict(kb=[lambda g=g: kbp_ref[2 * rd + g] for g in range(2)],
                        vb=[lambda g=g: vbp_ref[2 * rd + g] for g in range(2)],
                        tail=lambda: tail_ref[rd], state=lambda: state_ref[rd])
            for k in range(bps):
                cur = _mixer_block(sink_ref, pb_ref, pf_ref, pt_ref, bias_a_ref if k == 0 else bias_b_ref, hm_ref,
                                   aog_ref, dww_ref, dwb_ref, lng_ref, lnb_ref, pw_ref, cvg_ref,
                                   dec_ref, xi_ref, cdec_ref, gng_ref, o_ref, kbm_ref, vbm_ref,
                                   prev, layer=layer, row0=k * BLOCK)
                prev = dict(kb=[lambda g=g, c=cur: c["kb"][g] for g in range(2)],
                            vb=[lambda g=g, c=cur: c["vb"][g] for g in range(2)],
                            tail=lambda c=cur: c["tail"], state=lambda c=cur: c["state"])
            for g in range(2):
                kbp_ref[2 * wr + g] = cur["kb"][g]
                vbp_ref[2 * wr + g] = cur["vb"][g]
            tail_ref[wr] = cur["tail"]
            state_ref[wr] = cur["state"]

    @pl.when(n == pl.num_programs(1) - 1)
    def _hand_over():
        kbm_out_ref[...] = kbm_ref[...]
        vbm_out_ref[...] = vbm_ref[...]
        last = (pl.num_programs(1) - 1) % 2
        tail_out_ref[...] = tail_ref[last]
        state_out_ref[...] = state_ref[last]


def _mixer_block(sink_ref, pb_ref, pf_ref, pt_ref, bias_ref, hm_ref, aog_ref,
                 dww_ref, dwb_ref, lng_ref, lnb_ref, pw_ref, cvg_ref,
                 dec_ref, xi_ref, cdec_ref, gng_ref, o_ref, kbm_ref, vbm_ref,
                 prev, *, layer, row0, tick=lambda: None):
    pb_ref = pb_ref.at[row0:row0 + BLOCK, :]
    pf_ref = pf_ref.at[row0:row0 + BLOCK, :]
    pt_ref = pt_ref.at[:, row0:row0 + BLOCK]
    o_ref = o_ref.at[row0:row0 + BLOCK, :]
    m_head = lax.shift_right_logical(lax.broadcasted_iota(jnp.int32, (BLOCK, BLOCK), 1), 4)

    att, kb_out, vb_out = [], [], []
    for g in range(2):
        tick()
        kt = pt_ref[HEAD_DIM * g:HEAD_DIM * (g + 1), :]
        vg = pb_ref[:, PB_V + 128 * g:PB_V + 128 * (g + 1)]
        vrep = jnp.concatenate([vg, vg], axis=1)
        kb_cur = _diag_blocks([kt] * 4, 4)
        vb_cur = _row_blocks(vrep, hm_ref)
        kb_out.append(kb_cur)
        vb_out.append(vb_cur)

        qg = pb_ref[:, PB_Q + 256 * g:PB_Q + 256 * (g + 1)]
        gb = 1152 * g
        s_prev = _dot(qg, prev["kb"][g]()) + bias_ref[0, :, gb:gb + 512]
        s_cur = _dot(qg, kb_cur) + bias_ref[0, :, gb + 512:gb + 1024]
        sm = _dot(qg, kbm_ref[g]) + bias_ref[0, :, gb + 1024:gb + 1152]

        mx = []
        for j in range(4):
            t = jnp.maximum(jnp.maximum(s_prev[:, 128 * j:128 * (j + 1)], s_cur[:, 128 * j:128 * (j + 1)]),
                            jnp.where(m_head == j, sm, NEG))
            mx.append(jnp.maximum(jnp.max(t, axis=-1, keepdims=True), sink_ref[layer, 4 * g + j]))
        m_meta = jnp.where(m_head == 0, mx[0],
                           jnp.where(m_head == 1, mx[1],
                                     jnp.where(m_head == 2, mx[2], mx[3])))
        pm = jnp.exp(sm - m_meta)
        pp, pc, linv = [], [], []
        for j in range(4):
            tick()
            ppj = jnp.exp(s_prev[:, 128 * j:128 * (j + 1)] - mx[j])
            pcj = jnp.exp(s_cur[:, 128 * j:128 * (j + 1)] - mx[j])
            lj = jnp.sum(ppj + pcj + jnp.where(m_head == j, pm, 0.0), axis=-1, keepdims=True)
            lj = lj + jnp.exp(sink_ref[layer, 4 * g + j] - mx[j])
            pp.append(ppj.astype(BF16))
            pc.append(pcj.astype(BF16))
            linv.append(1.0 / lj)
        o = (_dot(jnp.concatenate(pp, axis=1), prev["vb"][g]())
             + _dot(jnp.concatenate(pc, axis=1), vb_cur)
             + _dot(pm.astype(BF16), vbm_ref[g]))
        seg = lax.shift_right_logical(lax.broadcasted_iota(jnp.int32, (BLOCK, 256), 1), 6)
        lfull = jnp.where(seg == 0, linv[0],
                          jnp.where(seg == 1, linv[1],
                                    jnp.where(seg == 2, linv[2], linv[3])))
        att.append(o * lfull)
    y_att = _rms(jnp.concatenate(att, axis=1), aog_ref[...])

    tick()
    u = pf_ref[:, 0:256]
    ub = jnp.concatenate([prev["tail"](), u], axis=0)
    base = CONV_TAIL - (CONV_K - 1)
    nrow = CONV_TAIL + BLOCK
    acc = None
    for r in range(SUBLANES):
        taps = [t for t in range(CONV_K) if (base + t) % SUBLANES == r]
        xr = ub if r == 0 else pltpu.roll(ub, nrow - r, axis=0)
        for t in taps:
            a = base + t - r
            term = dww_ref[t:t + 1, :] * xr[a:a + BLOCK, :]
            acc = term if acc is None else acc + term
    acc = acc + dwb_ref[...]
    mu = jnp.mean(acc, axis=-1, keepdims=True)
    dc = acc - mu
    var = jnp.mean(dc * dc, axis=-1, keepdims=True)
    cn = dc * lax.rsqrt(var + LN_EPS) * lng_ref[...] + lnb_ref[...]
    cs = cn * _sigmoid(cn)
    y_cv = _rms(_dot(cs.astype(BF16), pw_ref[...]), cvg_ref[...])

    tick()
    rq = pb_ref[:, PB_RQ:PB_RQ + 256]
    rkt = pt_ref[128:384, :]
    kb_r = _diag_blocks([rkt[HEAD_DIM * hh:HEAD_DIM * (hh + 1), :] for hh in range(RET_HEADS)],
                        RET_HEADS)
    vb_r = _row_blocks(pb_ref[:, PB_RV:PB_RV + 256], hm_ref)
    scores = _dot(rq, kb_r) * dec_ref[...]
    state = prev["state"]()
    y_r = _dot(scores.astype(BF16), vb_r) + _dot(rq, state.astype(BF16)) * xi_ref[...]
    kv = _dot(rkt, pb_ref[:, PB_RVZ:PB_RVZ + 256])
    r2 = lax.shift_right_logical(lax.broadcasted_iota(jnp.int32, (256, 256), 0), 6)
    c2 = lax.shift_right_logical(lax.broadcasted_iota(jnp.int32, (256, 256), 1), 6)
    state_new = cdec_ref[...] * state + jnp.where(r2 == c2, kv, 0.0)
    dr = y_r - _head_mean(y_r)
    var_r = _head_mean(dr * dr)
    yn = dr * lax.rsqrt(var_r + LN_EPS) * gng_ref[...]
    y_ret = pf_ref[:, 256:512] * yn

    tick()
    o_ref[...] = jnp.concatenate([y_att, y_cv, y_ret], axis=1).astype(BF16)
    return dict(kb=kb_out, vb=vb_out, tail=u[BLOCK - CONV_TAIL:BLOCK, :], state=state_new)


def _mixer(pb, pf, pt, l, sinks, bias, hm, aog, dww, dwb, lng, lnb, pw, cvg,
           dec, xi, cdec, gng, state_in, batch, nblk, first_block):
    rows = pb.shape[0]
    d = aog.shape[2] + cvg.shape[2] + gng.shape[2]
    pos0 = 0 if first_block else 1
    bps = 2 if nblk % 2 == 0 else 1
    steps = nblk // bps
    tb = bps * BLOCK

    def rowblk(b, n):
        return (b * steps + n, 0)

    def colblk(b, n):
        return (0, b * steps + n)

    def lspec(a):
        return _layer_spec(a, l)

    def bias_spec(k):
        return pl.BlockSpec((1,) + bias.shape[1:], lambda b, n: (jnp.minimum(pos0 + n * bps + k, 2), 0, 0))

    kern = functools.partial(_mixer_kernel, layer=l, first_block=first_block, bps=bps)
    state_shapes = tuple(jax.ShapeDtypeStruct(a.shape, a.dtype) for a in state_in)
    return pl.pallas_call(
        kern,
        out_shape=(jax.ShapeDtypeStruct((rows, d), BF16),) + state_shapes,
        grid=(batch, steps),
        in_specs=[
            pl.BlockSpec(memory_space=pltpu.SMEM),
            pl.BlockSpec((tb, pb.shape[1]), rowblk),
            pl.BlockSpec((tb, pf.shape[1]), rowblk),
            pl.BlockSpec((pt.shape[0], tb), colblk),
            bias_spec(0), bias_spec(bps - 1),
            _const_spec(hm), lspec(aog),
            lspec(dww), lspec(dwb), lspec(lng), lspec(lnb), lspec(pw), lspec(cvg),
            _const_spec(dec), _const_spec(xi), _const_spec(cdec), lspec(gng),
        ] + [_const_spec(a) for a in state_in],
        out_specs=(pl.BlockSpec((tb, d), rowblk),) + tuple(_const_spec(a) for a in state_in),
        scratch_shapes=[
            pltpu.VMEM((4, 256, 4 * BLOCK), BF16),
            pltpu.VMEM((4, 4 * BLOCK, 256), BF16),
            pltpu.VMEM((2, 256, BLOCK), BF16),
            pltpu.VMEM((2, BLOCK, 256), BF16),
            pltpu.VMEM((2, CONV_TAIL, 256), F32),
            pltpu.VMEM((2, 256, 256), F32),
        ],
        compiler_params=pltpu.CompilerParams(
            dimension_semantics=("arbitrary", "arbitrary"), vmem_limit_bytes=VMEM_LIMIT),
        name="mixer",
    )(sinks, pb, pf, pt, bias, bias, hm, aog, dww, dwb, lng, lnb, pw, cvg,
      dec, xi, cdec, gng, *state_in)


def _mixer_state_zeros():
    return (jnp.zeros((2, 256, BLOCK), BF16), jnp.zeros((2, BLOCK, 256), BF16),
            jnp.zeros((CONV_TAIL, 256), F32), jnp.zeros((256, 256), F32))


def _fused_kernel(sink_ref, h_ref, h0_ref, g_ref, w_ref, qg_ref, kg_ref, zeta_ref,
                  bias_a_ref, bias_b_ref, hm_ref, aog_ref,
                  dww_ref, dwb_ref, lng_ref, lnb_ref, pw_ref, cvg_ref,
                  dec_ref, xi_ref, cdec_ref, gng_ref,
                  kbm_in_ref, vbm_in_ref, tail_in_ref, state_in_ref,
                  o_ref,
                  kbp_ref, vbp_ref, kbm_ref, vbm_ref, tail_ref, state_ref,
                  pbs_ref, pfs_ref, pts_ref, *, layer, bps):
    b = pl.program_id(0)
    n = pl.program_id(1)
    tb = bps * BLOCK

    def project(rows_ref, slot):
        u = _rms(rows_ref[...], g_ref[...]).astype(BF16)
        yield
        bounds = (Q0, Q0 + 256, K0, CA0, RQ0, RK0, RV0, RG0, IN_END)
        ahead = {bounds[0]: _dot(u, w_ref[:, bounds[0]:bounds[1]])}
        yield

        def proj(lo, hi):
            i = bounds.index(hi)
            if i + 1 < len(bounds):
                ahead[hi] = _dot(u, w_ref[:, hi:bounds[i + 1]])
            return ahead.pop(lo)

        yield from _proj_groups(proj, qg_ref, kg_ref, zeta_ref,
                                pbs_ref.at[slot], pfs_ref.at[slot], pts_ref.at[slot], tm=tb, pad_rows=0)

    @pl.when(jnp.logical_and(b == 0, n == 0))
    def _first_operands():
        for _ in project(h0_ref, 0):
            pass

    @pl.when(n == 0)
    def _init():
        kbp_ref[...] = jnp.zeros_like(kbp_ref)
        vbp_ref[...] = jnp.zeros_like(vbp_ref)
        tail_ref[1] = tail_in_ref[...]
        state_ref[1] = state_in_ref[...]
        kbm_ref[...] = kbm_in_ref[...]
        vbm_ref[...] = vbm_in_ref[...]

    for cur_slot in range(2):
        nxt_slot = 1 - cur_slot

        @pl.when(lax.rem(n, 2) == cur_slot)
        def _step():
            pieces = project(h_ref, nxt_slot)

            ticks = [0]

            def tick():
                if ticks[0] % 2 == 0:
                    next(pieces, None)
                ticks[0] += 1

            prev = dict(kb=[lambda g=g: kbp_ref[2 * nxt_slot + g] for g in range(2)],
                        vb=[lambda g=g: vbp_ref[2 * nxt_slot + g] for g in range(2)],
                        tail=lambda: tail_ref[nxt_slot], state=lambda: state_ref[nxt_slot])
            for k in range(bps):
                cur = _mixer_block(sink_ref, pbs_ref.at[cur_slot], pfs_ref.at[cur_slot], pts_ref.at[cur_slot],
                                   bias_a_ref if k == 0 else bias_b_ref, hm_ref,
                                   aog_ref, dww_ref, dwb_ref, lng_ref, lnb_ref, pw_ref, cvg_ref,
                                   dec_ref, xi_ref, cdec_ref, gng_ref, o_ref, kbm_ref, vbm_ref,
                                   prev, layer=layer, row0=k * BLOCK, tick=tick)
                prev = dict(kb=[lambda g=g, c=cur: c["kb"][g] for g in range(2)],
                            vb=[lambda g=g, c=cur: c["vb"][g] for g in range(2)],
                            tail=lambda c=cur: c["tail"], state=lambda c=cur: c["state"])
            for _ in pieces:
                pass
            for g in range(2):
                kbp_ref[2 * cur_slot + g] = cur["kb"][g]
                vbp_ref[2 * cur_slot + g] = cur["vb"][g]
            tail_ref[cur_slot] = cur["tail"]
            state_ref[cur_slot] = cur["state"]


def _fused(h2, l, g, w, qg, kg, zeta_tile, sinks, bias, hm, aog, dww, dwb, lng, lnb, pw, cvg,
           dec, xi, cdec, gng, state_in, batch, nblk):
    rows, d = h2.shape
    dout = aog.shape[2] + cvg.shape[2] + gng.shape[2]
    bps = 2
    steps = nblk // bps
    assert steps % 2 == 0
    tb = bps * BLOCK
    last = batch * steps - 1

    def rowblk(b, n):
        return (b * steps + n, 0)

    def lspec(a):
        return _layer_spec(a, l)

    def bias_spec(k):
        return pl.BlockSpec((1,) + bias.shape[1:], lambda b, n: (jnp.minimum(1 + n * bps + k, 2), 0, 0))

    kern = functools.partial(_fused_kernel, layer=l, bps=bps)
    return pl.pallas_call(
        kern,
        out_shape=jax.ShapeDtypeStruct((rows, dout), BF16),
        grid=(batch, steps),
        in_specs=[
            pl.BlockSpec(memory_space=pltpu.SMEM),
            pl.BlockSpec((tb, d), lambda b, n: (jnp.minimum(b * steps + n + 1, last), 0)),
            pl.BlockSpec((tb, d), lambda b, n: (0, 0)),
            lspec(g), lspec(w), lspec(qg), lspec(kg), _const_spec(zeta_tile),
            bias_spec(0), bias_spec(1),
            _const_spec(hm), lspec(aog),
            lspec(dww), lspec(dwb), lspec(lng), lspec(lnb), lspec(pw), lspec(cvg),
            _const_spec(dec), _const_spec(xi), _const_spec(cdec), lspec(gng),
        ] + [_const_spec(a) for a in state_in],
        out_specs=pl.BlockSpec((tb, dout), rowblk),
        scratch_shapes=[
            pltpu.VMEM((4, 256, 4 * BLOCK), BF16),
            pltpu.VMEM((4, 4 * BLOCK, 256), BF16),
            pltpu.VMEM((2, 256, BLOCK), BF16),
            pltpu.VMEM((2, BLOCK, 256), BF16),
            pltpu.VMEM((2, CONV_TAIL, 256), F32),
            pltpu.VMEM((2, 256, 256), F32),
            pltpu.VMEM((2, tb, PB_END), BF16),
            pltpu.VMEM((2, tb, 512), F32),
            pltpu.VMEM((2, 384, tb), BF16),
        ],
        compiler_params=pltpu.CompilerParams(
            dimension_semantics=("arbitrary", "arbitrary"), vmem_limit_bytes=VMEM_LIMIT),
        name="proj_mixer",
    )(sinks, h2, h2, g, w, qg, kg, zeta_tile, bias, bias, hm, aog, dww, dwb, lng, lnb, pw, cvg,
      dec, xi, cdec, gng, *state_in)


def _ffn_kernel(h_ref, y_ref, wout_ref, g_ref, up_ref, dww_ref, dwb_ref, down_ref, carry_in_ref,
                o_ref, carry_out_ref, carry_ref, act_ref, *, tm, dff, cw, pad_rows):
    t = pl.program_id(1)
    nchunk = dff // cw
    wr = lax.rem(t, 2)
    rd = 1 - wr

    @pl.when(t == 0)
    def _init():
        carry_ref[1] = carry_in_ref[...]

    def keep(y):
        if not pad_rows:
            return y
        row = t * tm + lax.broadcasted_iota(jnp.int32, (tm, 1), 0)
        return jnp.where(row >= pad_rows, y, 0.0)

    h = h_ref[...] + keep(_dot(y_ref[...], wout_ref[...]))
    u = _rms(h, g_ref[...]).astype(BF16)
    for c in range(nchunk):
        halves = []
        for off in (c * cw, dff + c * cw):
            f = _dot(u, up_ref[:, off:off + cw])
            fe = jnp.concatenate([carry_ref[rd, :, off:off + cw], f], axis=0)
            carry_ref[wr, :, off:off + cw] = f[tm - FFN_TAIL:tm, :]
            fc = dwb_ref[:, off:off + cw] + dww_ref[FFN_CONV_K - 1:FFN_CONV_K, off:off + cw] * f
            for s in range(1, FFN_CONV_K):
                kk = FFN_CONV_K - 1 - s
                shifted = pltpu.roll(fe, s, axis=0)[FFN_TAIL:FFN_TAIL + tm, :]
                fc = fc + dww_ref[kk:kk + 1, off:off + cw] * shifted
            halves.append(fc)
        act_ref[:, c * cw:(c + 1) * cw] = (halves[0] * _sigmoid(halves[0]) * halves[1]).astype(BF16)
    o_ref[...] = h + keep(_dot(act_ref[...], down_ref[...]))

    @pl.when(t == pl.num_programs(1) - 1)
    def _hand_over():
        carry_out_ref[...] = carry_ref[wr]


def _ffn(h2, ycat, l, wout, g, up, dww, dwb, down, carry_in, batch, tiles, tm, cw, pad_rows):
    rows, d = h2.shape
    dff = down.shape[1]

    def rowblk(b, t):
        return (b * tiles + t, 0)

    def wspec(a):
        nd = a.ndim - 1
        return pl.BlockSpec((None,) + a.shape[1:], lambda b, t: (l,) + (0,) * nd,
                            pipeline_mode=pl.Buffered(1))

    kern = functools.partial(_ffn_kernel, tm=tm, dff=dff, cw=cw, pad_rows=pad_rows)
    return pl.pallas_call(
        kern,
        out_shape=(jax.ShapeDtypeStruct((rows, d), F32),
                   jax.ShapeDtypeStruct(carry_in.shape, F32)),
        grid=(batch, tiles),
        in_specs=[
            pl.BlockSpec((tm, d), rowblk),
            pl.BlockSpec((tm, d), rowblk),
            wspec(wout), wspec(g), wspec(up), wspec(dww), wspec(dwb), wspec(down),
            _const_spec(carry_in),
        ],
        out_specs=(pl.BlockSpec((tm, d), rowblk), _const_spec(carry_in)),
        scratch_shapes=[
            pltpu.VMEM((2, FFN_TAIL, 2 * dff), F32),
            pltpu.VMEM((tm, dff), BF16),
        ],
        compiler_params=pltpu.CompilerParams(
            dimension_semantics=("arbitrary", "arbitrary"), vmem_limit_bytes=VMEM_LIMIT),
        name="ffn",
    )(h2, ycat, wout, g, up, dww, dwb, down, carry_in)


def _retention_tables():
    log_gamma = jnp.log1p(-jnp.exp2(-5.0 - jnp.arange(RET_HEADS, dtype=F32)))
    idx = jnp.arange(BLOCK, dtype=F32)
    diff = idx[:, None] - idx[None, :]
    decay = jnp.where(diff[None] >= 0,
                      jnp.exp(jnp.maximum(diff, 0.0)[None] * log_gamma[:, None, None]), 0.0)
    zeta = jnp.exp((BLOCK - 1 - idx)[None, :] * log_gamma[:, None])
    xi = jnp.exp((idx + 1.0)[None, :] * log_gamma[:, None])
    chunk_decay = jnp.exp(BLOCK * log_gamma)
    scale = RET_DK ** -0.5
    dec_t = jnp.transpose(decay, (1, 0, 2)).reshape(BLOCK, RET_HEADS * BLOCK) * scale
    zeta_t = jnp.repeat(zeta.T, RET_DK, axis=1) * scale
    xi_t = jnp.repeat(xi.T, RET_DK, axis=1)
    cdec_t = jnp.repeat(chunk_decay, RET_DK)[None, :]
    return dec_t, zeta_t, xi_t, cdec_t


def _attention_bias():
    band = GQA_GROUP * BLOCK
    per_kv = 2 * band + BLOCK
    n = jnp.arange(3)[:, None, None]
    i = jnp.arange(BLOCK)[None, :, None]
    col = jnp.arange((N_Q_HEADS // GQA_GROUP) * per_kv)[None, None, :]
    r = col % per_kv
    c = r % BLOCK
    is_meta = r >= 2 * band
    is_cur = jnp.logical_and(r >= band, jnp.logical_not(is_meta))
    head = GQA_GROUP * (col // per_kv) + jnp.where(
        is_meta, jnp.minimum(c // N_META, GQA_GROUP - 1), (r % band) // BLOCK)
    slope = jnp.exp2(-(head + 1).astype(F32))
    ok_band = jnp.where(is_cur, jnp.logical_and(c <= i, n >= 1), jnp.logical_and(c > i, n >= 2))
    d_band = i - c + jnp.where(is_cur, 0, BLOCK)
    d_meta = n * BLOCK + i - PAD - (c % N_META)
    ok_meta = jnp.logical_and(d_meta >= 0, c < GQA_GROUP * N_META)
    dist = jnp.where(is_meta, jnp.minimum(d_meta, WINDOW), d_band).astype(F32)
    return jnp.where(jnp.where(is_meta, ok_meta, ok_band), -slope * dist, NEG)


def kernel(x, meta, norm_mix_g, w_in, q_norm_g, k_norm_g, attn_sinks, attn_out_g, cv_dw_w, cv_dw_b, cv_ln_g, cv_ln_b, cv_pw, cv_out_g, ret_gn_g, w_out, norm_ffn_g, ffn_up, ffn_dw_w, ffn_dw_b, ffn_down):
    batch, seq, d = x.shape
    depth = w_in.shape[0]
    dff = ffn_down.shape[1]

    dec_t, zeta_t, xi_t, cdec_t = _retention_tables()
    bias = _attention_bias()
    seg = jnp.arange(256) // HEAD_DIM
    hm = jnp.broadcast_to((seg[None, :] == jnp.arange(4)[:, None])[:, None, :], (4, BLOCK, 256)).astype(BF16)

    def vec(a):
        return a.reshape(depth, 1, a.shape[-1])

    w_in_b, w_out_b, up_b, down_b, pw_b = (a.astype(BF16) for a in (w_in, w_out, ffn_up, ffn_down, cv_pw))
    mix_g, ffn_g = vec(norm_mix_g), vec(norm_ffn_g)
    qg = vec(jnp.tile(q_norm_g, (1, N_Q_HEADS)) * (HEAD_DIM ** -0.5))
    kg = vec(jnp.tile(k_norm_g, (1, N_Q_HEADS // GQA_GROUP)))
    aog, dwb, lng, lnb, cvg, gng, fdwb = (vec(a) for a in (
        attn_out_g, cv_dw_b, cv_ln_g, cv_ln_b, cv_out_g, ret_gn_g, ffn_dw_b))

    h0 = jnp.concatenate([jnp.zeros((PAD, d), x.dtype), meta.astype(x.dtype)], axis=0)
    hr = x.reshape(batch * seq, d)
    nblk = seq // BLOCK
    in_tm = 2 * BLOCK
    ffn_tm = 512
    zeta_tile = jnp.tile(zeta_t, (in_tm // BLOCK, 1))

    def layer(h2, l, state_in, carry_in, nb, blocks, tm_in, tm_ffn, zt, first):
        pad_rows = PAD if first else 0
        if first:
            pb, pf, pt = _inproj(h2, l, mix_g, w_in_b, qg, kg, zt, tm_in, pad_rows)
            ycat, *state = _mixer(pb, pf, pt, l, attn_sinks, bias, hm, aog, cv_dw_w, dwb, lng, lnb,
                                  pw_b, cvg, dec_t, xi_t, cdec_t, gng, state_in, nb, blocks, first)
        else:
            ycat = _fused(h2, l, mix_g, w_in_b, qg, kg, zt, attn_sinks, bias, hm, aog, cv_dw_w, dwb, lng, lnb,
                          pw_b, cvg, dec_t, xi_t, cdec_t, gng, state_in, nb, blocks)
            state = ()
        h2, carry = _ffn(h2, ycat, l, w_out_b, ffn_g, up_b, ffn_dw_w, fdwb, down_b, carry_in,
                         nb, blocks * BLOCK // tm_ffn, tm_ffn, 256, pad_rows)
        return h2, tuple(state), carry

    for l in range(depth):
        h0, state0, carry0 = layer(h0, l, _mixer_state_zeros(), jnp.zeros((FFN_TAIL, 2 * dff), F32),
                                   1, 1, BLOCK, BLOCK, zeta_t, True)
        hr, _, _ = layer(hr, l, state0, carry0, batch, nblk, in_tm, ffn_tm, zeta_tile, False)
    return hr.reshape(batch, seq, d)
```

```python
import functools

import jax
import jax.numpy as jnp
from jax import lax
from jax.experimental import pallas as pl
from jax.experimental.pallas import tpu as pltpu

F32 = jnp.float32
BF16 = jnp.bfloat16

HEAD_DIM = 64
BLOCK = 128
WINDOW = 128
N_META = 16
PAD = BLOCK - N_META
N_Q_HEADS = 8
GQA_GROUP = 4
CONV_K = 31
FFN_CONV_K = 3
RET_DK = 64
RET_HEADS = 4
RMS_EPS = 1e-6
LN_EPS = 1e-5
NEG = -1e30

Q0, K0, V0, CA0, CB0, RQ0, RK0, RV0, RG0, IN_END = 0, 512, 640, 768, 1024, 1280, 1536, 1792, 2048, 2304
PB_Q, PB_V, PB_RQ, PB_RV, PB_RVZ, PB_END = 0, 512, 768, 1024, 1280, 1536

SUBLANES = 8
CONV_TAIL = 32
FFN_TAIL = SUBLANES
VMEM_LIMIT = 56 * 1024 * 1024


def _dot(a, b):
    return jnp.dot(a, b, preferred_element_type=F32)


def _sigmoid(x):
    return 1.0 / (1.0 + jnp.exp(-x))


def _rms(x, g):
    return x * lax.rsqrt(jnp.mean(x * x, axis=-1, keepdims=True) + RMS_EPS) * g


def _head_mean(x):
    low = lax.broadcasted_iota(jnp.int32, (x.shape[0], 2 * HEAD_DIM), 1) < HEAD_DIM
    cols = []
    for c in range(x.shape[1] // (2 * HEAD_DIM)):
        xs = x[:, 2 * HEAD_DIM * c:2 * HEAD_DIM * (c + 1)]
        lo = jnp.where(low, xs, 0.0)
        s_lo = jnp.sum(lo, axis=-1, keepdims=True)
        s_hi = jnp.sum(xs - lo, axis=-1, keepdims=True)
        cols.append(jnp.where(low, s_lo, s_hi))
    return jnp.concatenate(cols, axis=1) * (1.0 / HEAD_DIM)


def _layer_spec(a, l):
    nd = a.ndim - 1
    return pl.BlockSpec((None,) + a.shape[1:], lambda *_: (l,) + (0,) * nd)


def _const_spec(a):
    return pl.BlockSpec(a.shape, lambda *_: (0,) * a.ndim)


def _inproj_kernel(h_ref, g_ref, w_ref, qg_ref, kg_ref, zeta_ref,
                   pb_ref, pf_ref, pt_ref, p0_ref, p1_ref, *, tm, pad_rows):
    i = pl.program_id(0)

    @pl.when(i == 0)
    def _init():
        p1_ref[...] = jnp.zeros_like(p1_ref)

    @pl.when(lax.rem(i, 2) == 0)
    def _even():
        _inproj_step(h_ref, g_ref, w_ref, qg_ref, kg_ref, zeta_ref,
                     pb_ref, pf_ref, pt_ref, p0_ref, p1_ref, tm=tm, pad_rows=pad_rows)

    @pl.when(lax.rem(i, 2) == 1)
    def _odd():
        _inproj_step(h_ref, g_ref, w_ref, qg_ref, kg_ref, zeta_ref,
                     pb_ref, pf_ref, pt_ref, p1_ref, p0_ref, tm=tm, pad_rows=pad_rows)


def _inproj_step(h_ref, g_ref, w_ref, qg_ref, kg_ref, zeta_ref,
                 pb_ref, pf_ref, pt_ref, pw_ref, pr_ref, *, tm, pad_rows):
    u = _rms(h_ref[...], g_ref[...]).astype(BF16)
    pw_ref[...] = _dot(u, w_ref[...])
    for _ in _proj_groups(lambda lo, hi: pr_ref[:, lo:hi], qg_ref, kg_ref, zeta_ref,
                          pb_ref, pf_ref, pt_ref, tm=tm, pad_rows=pad_rows):
        pass


def _proj_groups(proj, qg_ref, kg_ref, zeta_ref, pb_ref, pf_ref, pt_ref, *, tm, pad_rows):
    for half in range(2):
        lo = Q0 + 256 * half
        q = proj(lo, lo + 256)
        qn = q * lax.rsqrt(_head_mean(q * q) + RMS_EPS) * qg_ref[:, 256 * half:256 * (half + 1)]
        pb_ref[:, PB_Q + 256 * half:PB_Q + 256 * (half + 1)] = qn.astype(BF16)
        yield

    kv = proj(K0, CA0)
    k = kv[:, 0:128]
    v = kv[:, 128:256]
    kn = k * lax.rsqrt(_head_mean(k * k) + RMS_EPS) * kg_ref[...]
    pt_ref[0:128, :] = kn.T.astype(BF16)
    low = lax.broadcasted_iota(jnp.int32, v.shape, 1) < HEAD_DIM
    v_sw = pltpu.roll(v, HEAD_DIM, axis=1)
    pb_ref[:, PB_V:PB_V + 128] = jnp.where(low, v, v_sw).astype(BF16)
    pb_ref[:, PB_V + 128:PB_V + 256] = jnp.where(low, v_sw, v).astype(BF16)
    yield

    cab = proj(CA0, RQ0)
    pf_ref[:, 0:256] = cab[:, 0:256] * _sigmoid(cab[:, 256:512])
    yield

    pb_ref[:, PB_RQ:PB_RQ + 256] = proj(RQ0, RK0).astype(BF16)
    yield
    rk = proj(RK0, RV0)
    if pad_rows:
        row = lax.broadcasted_iota(jnp.int32, (tm, 1), 0)
        rk = jnp.where(row < pad_rows, 0.0, rk)
    pt_ref[128:384, :] = rk.T.astype(BF16)
    yield
    rv = proj(RV0, RG0)
    pb_ref[:, PB_RV:PB_RV + 256] = rv.astype(BF16)
    pb_ref[:, PB_RVZ:PB_RVZ + 256] = (rv * zeta_ref[...]).astype(BF16)
    yield
    rg = proj(RG0, IN_END)
    pf_ref[:, 256:512] = rg * _sigmoid(rg)
    yield


def _inproj(h2, l, g, w, qg, kg, zeta_tile, tm, pad_rows):
    rows, d = h2.shape
    ntiles = rows // tm
    nout = w.shape[2]
    kern = functools.partial(_inproj_kernel, tm=tm, pad_rows=pad_rows)
    return pl.pallas_call(
        kern,
        out_shape=(jax.ShapeDtypeStruct((rows, PB_END), BF16),
                   jax.ShapeDtypeStruct((rows, 512), F32),
                   jax.ShapeDtypeStruct((384, rows), BF16)),
        grid=(ntiles + 1,),
        in_specs=[
            pl.BlockSpec((tm, d), lambda i: (jnp.minimum(i, ntiles - 1), 0)),
            _layer_spec(g, l), _layer_spec(w, l), _layer_spec(qg, l), _layer_spec(kg, l),
            _const_spec(zeta_tile),
        ],
        out_specs=(pl.BlockSpec((tm, PB_END), lambda i: (jnp.maximum(i - 1, 0), 0)),
                   pl.BlockSpec((tm, 512), lambda i: (jnp.maximum(i - 1, 0), 0)),
                   pl.BlockSpec((384, tm), lambda i: (0, jnp.maximum(i - 1, 0)))),
        scratch_shapes=[pltpu.VMEM((tm, nout), F32)] * 2,
        compiler_params=pltpu.CompilerParams(
            dimension_semantics=("arbitrary",), vmem_limit_bytes=VMEM_LIMIT),
        name="in_proj",
    )(h2, g, w, qg, kg, zeta_tile)


def _row_blocks(x, hm_ref):
    r = x.shape[0]
    return jnp.concatenate([x * hm_ref[j, 0:r, :] for j in range(4)], axis=0)


def _diag_blocks(xt, heads):
    z = jnp.zeros_like(xt[0])
    return jnp.concatenate(
        [jnp.concatenate([xt[j] if i == j else z for i in range(heads)], axis=1)
         for j in range(heads)], axis=0)


def _mixer_kernel(sink_ref, pb_ref, pf_ref, pt_ref, bias_a_ref, bias_b_ref, hm_ref, aog_ref,
                  dww_ref, dwb_ref, lng_ref, lnb_ref, pw_ref, cvg_ref,
                  dec_ref, xi_ref, cdec_ref, gng_ref,
                  kbm_in_ref, vbm_in_ref, tail_in_ref, state_in_ref,
                  o_ref, kbm_out_ref, vbm_out_ref, tail_out_ref, state_out_ref,
                  kbp_ref, vbp_ref, kbm_ref, vbm_ref, tail_ref, state_ref, *, layer, first_block, bps):
    n = pl.program_id(1)

    @pl.when(n == 0)
    def _init():
        kbp_ref[...] = jnp.zeros_like(kbp_ref)
        vbp_ref[...] = jnp.zeros_like(vbp_ref)
        if first_block:
            tail_ref[1] = jnp.zeros(tail_ref.shape[1:], F32)
            state_ref[1] = jnp.zeros(state_ref.shape[1:], F32)
            for g in range(2):
                ktf = pt_ref[HEAD_DIM * g:HEAD_DIM * (g + 1), :].astype(F32)
                tok = lax.broadcasted_iota(jnp.int32, ktf.shape, 1)
                ktm = jnp.where(tok >= PAD, ktf, 0.0)
                kbm_ref[g] = jnp.concatenate(
                    [pltpu.roll(ktm, N_META * (j + 1), axis=1) for j in range(4)], axis=0).astype(BF16)
                vm = pb_ref[PAD:BLOCK, PB_V + 128 * g:PB_V + 128 * (g + 1)]
                z = jnp.zeros((BLOCK - 4 * N_META, 256), BF16)
                vbm_ref[g] = jnp.concatenate(
                    [_row_blocks(jnp.concatenate([vm, vm], axis=1), hm_ref), z], axis=0)
        else:
            tail_ref[1] = tail_in_ref[...]
            state_ref[1] = state_in_ref[...]
            kbm_ref[...] = kbm_in_ref[...]
            vbm_ref[...] = vbm_in_ref[...]

    for wr in range(2):
        rd = 1 - wr

        @pl.when(lax.rem(n, 2) == wr)
        def _step():
            prev = dict(kb=[lambda g=g: kbp_ref[2 * rd + g] for g in range(2)],
                        vb=[lambda g=g: vbp_ref[2 * rd + g] for g in range(2)],
                        tail=lambda: tail_ref[rd], state=lambda: state_ref[rd])
            for k in range(bps):
                cur = _mixer_block(sink_ref, pb_ref, pf_ref, pt_ref, bias_a_ref if k == 0 else bias_b_ref, hm_ref,
                                   aog_ref, dww_ref, dwb_ref, lng_ref, lnb_ref, pw_ref, cvg_ref,
                                   dec_ref, xi_ref, cdec_ref, gng_ref, o_ref, kbm_ref, vbm_ref,
                                   prev, layer=layer, row0=k * BLOCK)
                prev = dict(kb=[lambda g=g, c=cur: c["kb"][g] for g in range(2)],
                            vb=[lambda g=g, c=cur: c["vb"][g] for g in range(2)],
                            tail=lambda c=cur: c["tail"], state=lambda c=cur: c["state"])
            for g in range(2):
                kbp_ref[2 * wr + g] = cur["kb"][g]
                vbp_ref[2 * wr + g] = cur["vb"][g]
            tail_ref[wr] = cur["tail"]
            state_ref[wr] = cur["state"]

    @pl.when(n == pl.num_programs(1) - 1)
    def _hand_over():
        kbm_out_ref[...] = kbm_ref[...]
        vbm_out_ref[...] = vbm_ref[...]
        last = (pl.num_programs(1) - 1) % 2
        tail_out_ref[...] = tail_ref[last]
        state_out_ref[...] = state_ref[last]


def _mixer_block(sink_ref, pb_ref, pf_ref, pt_ref, bias_ref, hm_ref, aog_ref,
                 dww_ref, dwb_ref, lng_ref, lnb_ref, pw_ref, cvg_ref,
                 dec_ref, xi_ref, cdec_ref, gng_ref, o_ref, kbm_ref, vbm_ref,
                 prev, *, layer, row0, tick=lambda: None):
    pb_ref = pb_ref.at[row0:row0 + BLOCK, :]
    pf_ref = pf_ref.at[row0:row0 + BLOCK, :]
    pt_ref = pt_ref.at[:, row0:row0 + BLOCK]
    o_ref = o_ref.at[row0:row0 + BLOCK, :]
    m_head = lax.shift_right_logical(lax.broadcasted_iota(jnp.int32, (BLOCK, BLOCK), 1), 4)

    att, kb_out, vb_out = [], [], []
    for g in range(2):
        tick()
        kt = pt_ref[HEAD_DIM * g:HEAD_DIM * (g + 1), :]
        vg = pb_ref[:, PB_V + 128 * g:PB_V + 128 * (g + 1)]
        vrep = jnp.concatenate([vg, vg], axis=1)
        kb_cur = _diag_blocks([kt] * 4, 4)
        vb_cur = _row_blocks(vrep, hm_ref)
        kb_out.append(kb_cur)
        vb_out.append(vb_cur)

        qg = pb_ref[:, PB_Q + 256 * g:PB_Q + 256 * (g + 1)]
        gb = 1152 * g
        s_prev = _dot(qg, prev["kb"][g]()) + bias_ref[0, :, gb:gb + 512]
        s_cur = _dot(qg, kb_cur) + bias_ref[0, :, gb + 512:gb + 1024]
        sm = _dot(qg, kbm_ref[g]) + bias_ref[0, :, gb + 1024:gb + 1152]

        mx = []
        for j in range(4):
            t = jnp.maximum(jnp.maximum(s_prev[:, 128 * j:128 * (j + 1)], s_cur[:, 128 * j:128 * (j + 1)]),
                            jnp.where(m_head == j, sm, NEG))
            mx.append(jnp.maximum(jnp.max(t, axis=-1, keepdims=True), sink_ref[layer, 4 * g + j]))
        m_meta = jnp.where(m_head == 0, mx[0],
                           jnp.where(m_head == 1, mx[1],
                                     jnp.where(m_head == 2, mx[2], mx[3])))
        pm = jnp.exp(sm - m_meta)
        pp, pc, linv = [], [], []
        for j in range(4):
            tick()
            ppj = jnp.exp(s_prev[:, 128 * j:128 * (j + 1)] - mx[j])
            pcj = jnp.exp(s_cur[:, 128 * j:128 * (j + 1)] - mx[j])
            lj = jnp.sum(ppj + pcj + jnp.where(m_head == j, pm, 0.0), axis=-1, keepdims=True)
            lj = lj + jnp.exp(sink_ref[layer, 4 * g + j] - mx[j])
            pp.append(ppj.astype(BF16))
            pc.append(pcj.astype(BF16))
            linv.append(1.0 / lj)
        o = (_dot(jnp.concatenate(pp, axis=1), prev["vb"][g]())
             + _dot(jnp.concatenate(pc, axis=1), vb_cur)
             + _dot(pm.astype(BF16), vbm_ref[g]))
        seg = lax.shift_right_logical(lax.broadcasted_iota(jnp.int32, (BLOCK, 256), 1), 6)
        lfull = jnp.where(seg == 0, linv[0],
                          jnp.where(seg == 1, linv[1],
                                    jnp.where(seg == 2, linv[2], linv[3])))
        att.append(o * lfull)
    y_att = _rms(jnp.concatenate(att, axis=1), aog_ref[...])

    tick()
    u = pf_ref[:, 0:256]
    ub = jnp.concatenate([prev["tail"](), u], axis=0)
    base = CONV_TAIL - (CONV_K - 1)
    nrow = CONV_TAIL + BLOCK
    acc = None
    for r in range(SUBLANES):
        taps = [t for t in range(CONV_K) if (base + t) % SUBLANES == r]
        xr = ub if r == 0 else pltpu.roll(ub, nrow - r, axis=0)
        for t in taps:
            a = base + t - r
            term = dww_ref[t:t + 1, :] * xr[a:a + BLOCK, :]
            acc = term if acc is None else acc + term
    acc = acc + dwb_ref[...]
    mu = jnp.mean(acc, axis=-1, keepdims=True)
    dc = acc - mu
    var = jnp.mean(dc * dc, axis=-1, keepdims=True)
    cn = dc * lax.rsqrt(var + LN_EPS) * lng_ref[...] + lnb_ref[...]
    cs = cn * _sigmoid(cn)
    y_cv = _rms(_dot(cs.astype(BF16), pw_ref[...]), cvg_ref[...])

    tick()
    rq = pb_ref[:, PB_RQ:PB_RQ + 256]
    rkt = pt_ref[128:384, :]
    kb_r = _diag_blocks([rkt[HEAD_DIM * hh:HEAD_DIM * (hh + 1), :] for hh in range(RET_HEADS)],
                        RET_HEADS)
    vb_r = _row_blocks(pb_ref[:, PB_RV:PB_RV + 256], hm_ref)
    scores = _dot(rq, kb_r) * dec_ref[...]
    state = prev["state"]()
    y_r = _dot(scores.astype(BF16), vb_r) + _dot(rq, state.astype(BF16)) * xi_ref[...]
    kv = _dot(rkt, pb_ref[:, PB_RVZ:PB_RVZ + 256])
    r2 = lax.shift_right_logical(lax.broadcasted_iota(jnp.int32, (256, 256), 0), 6)
    c2 = lax.shift_right_logical(lax.broadcasted_iota(jnp.int32, (256, 256), 1), 6)
    state_new = cdec_ref[...] * state + jnp.where(r2 == c2, kv, 0.0)
    dr = y_r - _head_mean(y_r)
    var_r = _head_mean(dr * dr)
    yn = dr * lax.rsqrt(var_r + LN_EPS) * gng_ref[...]
    y_ret = pf_ref[:, 256:512] * yn

    tick()
    o_ref[...] = jnp.concatenate([y_att, y_cv, y_ret], axis=1).astype(BF16)
    return dict(kb=kb_out, vb=vb_out, tail=u[BLOCK - CONV_TAIL:BLOCK, :], state=state_new)


def _mixer(pb, pf, pt, l, sinks, bias, hm, aog, dww, dwb, lng, lnb, pw, cvg,
           dec, xi, cdec, gng, state_in, batch, nblk, first_block):
    rows = pb.shape[0]
    d = aog.shape[2] + cvg.shape[2] + gng.shape[2]
    pos0 = 0 if first_block else 1
    bps = 2 if nblk % 2 == 0 else 1
    steps = nblk // bps
    tb = bps * BLOCK

    def rowblk(b, n):
        return (b * steps + n, 0)

    def colblk(b, n):
        return (0, b * steps + n)

    def lspec(a):
        return _layer_spec(a, l)

    def bias_spec(k):
        return pl.BlockSpec((1,) + bias.shape[1:], lambda b, n: (jnp.minimum(pos0 + n * bps + k, 2), 0, 0))

    kern = functools.partial(_mixer_kernel, layer=l, first_block=first_block, bps=bps)
    state_shapes = tuple(jax.ShapeDtypeStruct(a.shape, a.dtype) for a in state_in)
    return pl.pallas_call(
        kern,
        out_shape=(jax.ShapeDtypeStruct((rows, d), BF16),) + state_shapes,
        grid=(batch, steps),
        in_specs=[
            pl.BlockSpec(memory_space=pltpu.SMEM),
            pl.BlockSpec((tb, pb.shape[1]), rowblk),
            pl.BlockSpec((tb, pf.shape[1]), rowblk),
            pl.BlockSpec((pt.shape[0], tb), colblk),
            bias_spec(0), bias_spec(bps - 1),
            _const_spec(hm), lspec(aog),
            lspec(dww), lspec(dwb), lspec(lng), lspec(lnb), lspec(pw), lspec(cvg),
            _const_spec(dec), _const_spec(xi), _const_spec(cdec), lspec(gng),
        ] + [_const_spec(a) for a in state_in],
        out_specs=(pl.BlockSpec((tb, d), rowblk),) + tuple(_const_spec(a) for a in state_in),
        scratch_shapes=[
            pltpu.VMEM((4, 256, 4 * BLOCK), BF16),
            pltpu.VMEM((4, 4 * BLOCK, 256), BF16),
            pltpu.VMEM((2, 256, BLOCK), BF16),
            pltpu.VMEM((2, BLOCK, 256), BF16),
            pltpu.VMEM((2, CONV_TAIL, 256), F32),
            pltpu.VMEM((2, 256, 256), F32),
        ],
        compiler_params=pltpu.CompilerParams(
            dimension_semantics=("arbitrary", "arbitrary"), vmem_limit_bytes=VMEM_LIMIT),
        name="mixer",
    )(sinks, pb, pf, pt, bias, bias, hm, aog, dww, dwb, lng, lnb, pw, cvg,
      dec, xi, cdec, gng, *state_in)


def _mixer_state_zeros():
    return (jnp.zeros((2, 256, BLOCK), BF16), jnp.zeros((2, BLOCK, 256), BF16),
            jnp.zeros((CONV_TAIL, 256), F32), jnp.zeros((256, 256), F32))


def _fused_kernel(sink_ref, h_ref, h0_ref, g_ref, w_ref, qg_ref, kg_ref, zeta_ref,
                  bias_a_ref, bias_b_ref, hm_ref, aog_ref,
                  dww_ref, dwb_ref, lng_ref, lnb_ref, pw_ref, cvg_ref,
                  dec_ref, xi_ref, cdec_ref, gng_ref,
                  kbm_in_ref, vbm_in_ref, tail_in_ref, state_in_ref,
                  o_ref,
                  kbp_ref, vbp_ref, kbm_ref, vbm_ref, tail_ref, state_ref,
                  pbs_ref, pfs_ref, pts_ref, *, layer, bps):
    b = pl.program_id(0)
    n = pl.program_id(1)
    tb = bps * BLOCK

    def project(rows_ref, slot):
        u = _rms(rows_ref[...], g_ref[...]).astype(BF16)
        yield
        bounds = (Q0, Q0 + 256, K0, CA0, RQ0, RK0, RV0, RG0, IN_END)
        ahead = {bounds[0]: _dot(u, w_ref[:, bounds[0]:bounds[1]])}
        yield

        def proj(lo, hi):
            i = bounds.index(hi)
            if i + 1 < len(bounds):
                ahead[hi] = _dot(u, w_ref[:, hi:bounds[i + 1]])
            return ahead.pop(lo)

        yield from _proj_groups(proj, qg_ref, kg_ref, zeta_ref,
                                pbs_ref.at[slot], pfs_ref.at[slot], pts_ref.at[slot], tm=tb, pad_rows=0)

    @pl.when(jnp.logical_and(b == 0, n == 0))
    def _first_operands():
        for _ in project(h0_ref, 0):
            pass

    @pl.when(n == 0)
    def _init():
        kbp_ref[...] = jnp.zeros_like(kbp_ref)
        vbp_ref[...] = jnp.zeros_like(vbp_ref)
        tail_ref[1] = tail_in_ref[...]
        state_ref[1] = state_in_ref[...]
        kbm_ref[...] = kbm_in_ref[...]
        vbm_ref[...] = vbm_in_ref[...]

    for cur_slot in range(2):
        nxt_slot = 1 - cur_slot

        @pl.when(lax.rem(n, 2) == cur_slot)
        def _step():
            pieces = project(h_ref, nxt_slot)

            ticks = [0]

            def tick():
                if ticks[0] % 2 == 0:
                    next(pieces, None)
                ticks[0] += 1

            prev = dict(kb=[lambda g=g: kbp_ref[2 * nxt_slot + g] for g in range(2)],
                        vb=[lambda g=g: vbp_ref[2 * nxt_slot + g] for g in range(2)],
                        tail=lambda: tail_ref[nxt_slot], state=lambda: state_ref[nxt_slot])
            for k in range(bps):
                cur = _mixer_block(sink_ref, pbs_ref.at[cur_slot], pfs_ref.at[cur_slot], pts_ref.at[cur_slot],
                                   bias_a_ref if k == 0 else bias_b_ref, hm_ref,
                                   aog_ref, dww_ref, dwb_ref, lng_ref, lnb_ref, pw_ref, cvg_ref,
                                   dec_ref, xi_ref, cdec_ref, gng_ref, o_ref, kbm_ref, vbm_ref,
                                   prev, layer=layer, row0=k * BLOCK, tick=tick)
                prev = dict(kb=[lambda g=g, c=cur: c["kb"][g] for g in range(2)],
                            vb=[lambda g=g, c=cur: c["vb"][g] for g in range(2)],
                            tail=lambda c=cur: c["tail"], state=lambda c=cur: c["state"])
            for _ in pieces:
                pass
            for g in range(2):
                kbp_ref[2 * cur_slot + g] = cur["kb"][g]
                vbp_ref[2 * cur_slot + g] = cur["vb"][g]
            tail_ref[cur_slot] = cur["tail"]
            state_ref[cur_slot] = cur["state"]


def _fused(h2, l, g, w, qg, kg, zeta_tile, sinks, bias, hm, aog, dww, dwb, lng, lnb, pw, cvg,
           dec, xi, cdec, gng, state_in, batch, nblk):
    rows, d = h2.shape
    dout = aog.shape[2] + cvg.shape[2] + gng.shape[2]
    bps = 2
    steps = nblk // bps
    assert steps % 2 == 0
    tb = bps * BLOCK
    last = batch * steps - 1

    def rowblk(b, n):
        return (b * steps + n, 0)

    def lspec(a):
        return _layer_spec(a, l)

    def bias_spec(k):
        return pl.BlockSpec((1,) + bias.shape[1:], lambda b, n: (jnp.minimum(1 + n * bps + k, 2), 0, 0))

    kern = functools.partial(_fused_kernel, layer=l, bps=bps)
    return pl.pallas_call(
        kern,
        out_shape=jax.ShapeDtypeStruct((rows, dout), BF16),
        grid=(batch, steps),
        in_specs=[
            pl.BlockSpec(memory_space=pltpu.SMEM),
            pl.BlockSpec((tb, d), lambda b, n: (jnp.minimum(b * steps + n + 1, last), 0)),
            pl.BlockSpec((tb, d), lambda b, n: (0, 0)),
            lspec(g), lspec(w), lspec(qg), lspec(kg), _const_spec(zeta_tile),
            bias_spec(0), bias_spec(1),
            _const_spec(hm), lspec(aog),
            lspec(dww), lspec(dwb), lspec(lng), lspec(lnb), lspec(pw), lspec(cvg),
            _const_spec(dec), _const_spec(xi), _const_spec(cdec), lspec(gng),
        ] + [_const_spec(a) for a in state_in],
        out_specs=pl.BlockSpec((tb, dout), rowblk),
        scratch_shapes=[
            pltpu.VMEM((4, 256, 4 * BLOCK), BF16),
            pltpu.VMEM((4, 4 * BLOCK, 256), BF16),
            pltpu.VMEM((2, 256, BLOCK), BF16),
            pltpu.VMEM((2, BLOCK, 256), BF16),
            pltpu.VMEM((2, CONV_TAIL, 256), F32),
            pltpu.VMEM((2, 256, 256), F32),
            pltpu.VMEM((2, tb, PB_END), BF16),
            pltpu.VMEM((2, tb, 512), F32),
            pltpu.VMEM((2, 384, tb), BF16),
        ],
        compiler_params=pltpu.CompilerParams(
            dimension_semantics=("arbitrary", "arbitrary"), vmem_limit_bytes=VMEM_LIMIT),
        name="proj_mixer",
    )(sinks, h2, h2, g, w, qg, kg, zeta_tile, bias, bias, hm, aog, dww, dwb, lng, lnb, pw, cvg,
      dec, xi, cdec, gng, *state_in)


def _ffn_kernel(h_ref, y_ref, wout_ref, g_ref, up_ref, dww_ref, dwb_ref, down_ref, carry_in_ref,
                o_ref, carry_out_ref, carry_ref, act_ref, *, tm, dff, cw, pad_rows):
    t = pl.program_id(1)
    nchunk = dff // cw
    wr = lax.rem(t, 2)
    rd = 1 - wr

    @pl.when(t == 0)
    def _init():
        carry_ref[1] = carry_in_ref[...]

    def keep(y):
        if not pad_rows:
            return y
        row = t * tm + lax.broadcasted_iota(jnp.int32, (tm, 1), 0)
        return jnp.where(row >= pad_rows, y, 0.0)

    if pad_rows or tm % (2 * BLOCK):
        h = h_ref[...] + keep(_dot(y_ref[...], wout_ref[...]))
        u = _rms(h, g_ref[...]).astype(BF16)
    else:
        half = tm // 2
        hs = [h_ref[r0:r0 + half, :] + _dot(y_ref[r0:r0 + half, :], wout_ref[...]) for r0 in (0, half)]
        h = jnp.concatenate(hs, axis=0)
        u = jnp.concatenate([_rms(x, g_ref[...]).astype(BF16) for x in hs], axis=0)
    for c in range(nchunk):
        halves = []
        for off in (c * cw, dff + c * cw):
            f = _dot(u, up_ref[:, off:off + cw])
            fe = jnp.concatenate([carry_ref[rd, :, off:off + cw], f], axis=0)
            carry_ref[wr, :, off:off + cw] = f[tm - FFN_TAIL:tm, :]
            fc = dwb_ref[:, off:off + cw] + dww_ref[FFN_CONV_K - 1:FFN_CONV_K, off:off + cw] * f
            for s in range(1, FFN_CONV_K):
                kk = FFN_CONV_K - 1 - s
                shifted = pltpu.roll(fe, s, axis=0)[FFN_TAIL:FFN_TAIL + tm, :]
                fc = fc + dww_ref[kk:kk + 1, off:off + cw] * shifted
            halves.append(fc)
        act_ref[:, c * cw:(c + 1) * cw] = (halves[0] * _sigmoid(halves[0]) * halves[1]).astype(BF16)
    o_ref[...] = h + keep(_dot(act_ref[...], down_ref[...]))

    @pl.when(t == pl.num_programs(1) - 1)
    def _hand_over():
        carry_out_ref[...] = carry_ref[wr]


def _ffn(h2, ycat, l, wout, g, up, dww, dwb, down, carry_in, batch, tiles, tm, cw, pad_rows):
    rows, d = h2.shape
    dff = down.shape[1]

    def rowblk(b, t):
        return (b * tiles + t, 0)

    def wspec(a):
        nd = a.ndim - 1
        return pl.BlockSpec((None,) + a.shape[1:], lambda b, t: (l,) + (0,) * nd,
                            pipeline_mode=pl.Buffered(1))

    kern = functools.partial(_ffn_kernel, tm=tm, dff=dff, cw=cw, pad_rows=pad_rows)
    return pl.pallas_call(
        kern,
        out_shape=(jax.ShapeDtypeStruct((rows, d), F32),
                   jax.ShapeDtypeStruct(carry_in.shape, F32)),
        grid=(batch, tiles),
        in_specs=[
            pl.BlockSpec((tm, d), rowblk),
            pl.BlockSpec((tm, d), rowblk),
            wspec(wout), wspec(g), wspec(up), wspec(dww), wspec(dwb), wspec(down),
            _const_spec(carry_in),
        ],
        out_specs=(pl.BlockSpec((tm, d), rowblk), _const_spec(carry_in)),
        scratch_shapes=[
            pltpu.VMEM((2, FFN_TAIL, 2 * dff), F32),
            pltpu.VMEM((tm, dff), BF16),
        ],
        compiler_params=pltpu.CompilerParams(
            dimension_semantics=("arbitrary", "arbitrary"), vmem_limit_bytes=VMEM_LIMIT),
        name="ffn",
    )(h2, ycat, wout, g, up, dww, dwb, down, carry_in)


def _retention_tables():
    log_gamma = jnp.log1p(-jnp.exp2(-5.0 - jnp.arange(RET_HEADS, dtype=F32)))
    idx = jnp.arange(BLOCK, dtype=F32)
    diff = idx[:, None] - idx[None, :]
    decay = jnp.where(diff[None] >= 0,
                      jnp.exp(jnp.maximum(diff, 0.0)[None] * log_gamma[:, None, None]), 0.0)
    zeta = jnp.exp((BLOCK - 1 - idx)[None, :] * log_gamma[:, None])
    xi = jnp.exp((idx + 1.0)[None, :] * log_gamma[:, None])
    chunk_decay = jnp.exp(BLOCK * log_gamma)
    scale = RET_DK ** -0.5
    dec_t = jnp.transpose(decay, (1, 0, 2)).reshape(BLOCK, RET_HEADS * BLOCK) * scale
    zeta_t = jnp.repeat(zeta.T, RET_DK, axis=1) * scale
    xi_t = jnp.repeat(xi.T, RET_DK, axis=1)
    cdec_t = jnp.repeat(chunk_decay, RET_DK)[None, :]
    return dec_t, zeta_t, xi_t, cdec_t


def _attention_bias():
    n_kv = N_Q_HEADS // GQA_GROUP
    slopes = jnp.exp2(-8.0 * jnp.arange(1, N_Q_HEADS + 1, dtype=F32) / N_Q_HEADS).reshape(n_kv, GQA_GROUP)
    n = jnp.arange(3)[:, None, None]
    i = jnp.arange(BLOCK)[None, :, None]
    c = jnp.arange(BLOCK)[None, None, :]

    def head_tiles(ok, dist):
        b = jnp.where(ok[:, :, None, None, :],
                      -slopes[None, None, :, :, None] * dist.astype(F32)[:, :, None, None, :], NEG)
        return b.reshape(3, BLOCK, n_kv, GQA_GROUP * BLOCK)

    prev = head_tiles(jnp.logical_and(c > i, n >= 2), jnp.broadcast_to(i - c + BLOCK, (3, BLOCK, BLOCK)))
    cur = head_tiles(jnp.logical_and(c <= i, n >= 1), jnp.broadcast_to(i - c, (3, BLOCK, BLOCK)))
    d_meta = n * BLOCK + i - PAD - (c % N_META)
    ok_meta = jnp.logical_and(d_meta >= 0, c < GQA_GROUP * N_META)
    sl_meta = slopes[:, jnp.minimum(jnp.arange(BLOCK) // N_META, GQA_GROUP - 1)]
    meta = jnp.where(ok_meta[:, :, None, :],
                     -sl_meta[None, None] * jnp.minimum(d_meta, WINDOW).astype(F32)[:, :, None, :], NEG)
    return jnp.concatenate([prev, cur, meta], axis=-1).reshape(3, BLOCK, -1)


def kernel(x, meta, norm_mix_g, w_in, q_norm_g, k_norm_g, attn_sinks, attn_out_g, cv_dw_w, cv_dw_b, cv_ln_g, cv_ln_b, cv_pw, cv_out_g, ret_gn_g, w_out, norm_ffn_g, ffn_up, ffn_dw_w, ffn_dw_b, ffn_down):
    batch, seq, d = x.shape
    depth = w_in.shape[0]
    dff = ffn_down.shape[1]

    dec_t, zeta_t, xi_t, cdec_t = _retention_tables()
    bias = _attention_bias()
    seg = jnp.arange(256) // HEAD_DIM
    hm = jnp.broadcast_to((seg[None, :] == jnp.arange(4)[:, None])[:, None, :], (4, BLOCK, 256)).astype(BF16)

    def vec(a):
        return a.reshape(depth, 1, a.shape[-1])

    w_in_b, w_out_b, up_b, down_b, pw_b = (a.astype(BF16) for a in (w_in, w_out, ffn_up, ffn_down, cv_pw))
    mix_g, ffn_g = vec(norm_mix_g), vec(norm_ffn_g)
    qg = vec(jnp.tile(q_norm_g, (1, N_Q_HEADS)) * (HEAD_DIM ** -0.5))
    kg = vec(jnp.tile(k_norm_g, (1, N_Q_HEADS // GQA_GROUP)))
    aog, dwb, lng, lnb, cvg, gng, fdwb = (vec(a) for a in (
        attn_out_g, cv_dw_b, cv_ln_g, cv_ln_b, cv_out_g, ret_gn_g, ffn_dw_b))

    h0 = jnp.concatenate([jnp.zeros((PAD, d), x.dtype), meta.astype(x.dtype)], axis=0)
    hr = x.reshape(batch * seq, d)
    nblk = seq // BLOCK
    in_tm = 2 * BLOCK
    ffn_tm = 512
    zeta_tile = jnp.tile(zeta_t, (in_tm // BLOCK, 1))

    def layer(h2, l, state_in, carry_in, nb, blocks, tm_in, tm_ffn, zt, first):
        pad_rows = PAD if first else 0
        if first:
            pb, pf, pt = _inproj(h2, l, mix_g, w_in_b, qg, kg, zt, tm_in, pad_rows)
            ycat, *state = _mixer(pb, pf, pt, l, attn_sinks, bias, hm, aog, cv_dw_w, dwb, lng, lnb,
                                  pw_b, cvg, dec_t, xi_t, cdec_t, gng, state_in, nb, blocks, first)
        else:
            ycat = _fused(h2, l, mix_g, w_in_b, qg, kg, zt, attn_sinks, bias, hm, aog, cv_dw_w, dwb, lng, lnb,
                          pw_b, cvg, dec_t, xi_t, cdec_t, gng, state_in, nb, blocks)
            state = ()
        h2, carry = _ffn(h2, ycat, l, w_out_b, ffn_g, up_b, ffn_dw_w, fdwb, down_b, carry_in,
                         nb, blocks * BLOCK // tm_ffn, tm_ffn, 256, pad_rows)
        return h2, tuple(state), carry

    for l in range(depth):
        h0, state0, carry0 = layer(h0, l, _mixer_state_zeros(), jnp.zeros((FFN_TAIL, 2 * dff), F32),
                                   1, 1, BLOCK, BLOCK, zeta_t, True)
        hr, _, _ = layer(hr, l, state0, carry0, batch, nblk, in_tm, ffn_tm, zeta_tile, False)
    return hr.reshape(batch, seq, d)
```

```python
import functools

import jax
import jax.numpy as jnp
from jax import lax
from jax.experimental import pallas as pl
from jax.experimental.pallas import tpu as pltpu

F32 = jnp.float32
BF16 = jnp.bfloat16

HEAD_DIM = 64
BLOCK = 128
WINDOW = 128
N_META = 16
PAD = BLOCK - N_META
N_Q_HEADS = 8
GQA_GROUP = 4
CONV_K = 31
FFN_CONV_K = 3
RET_DK = 64
RET_HEADS = 4
RMS_EPS = 1e-6
LN_EPS = 1e-5
NEG = -1e30

Q0, K0, V0, CA0, CB0, RQ0, RK0, RV0, RG0, IN_END = 0, 512, 640, 768, 1024, 1280, 1536, 1792, 2048, 2304
PB_Q, PB_V, PB_RQ, PB_RV, PB_RVZ, PB_END = 0, 512, 768, 1024, 1280, 1536

SUBLANES = 8
CONV_TAIL = 32
FFN_TAIL = SUBLANES
VMEM_LIMIT = 56 * 1024 * 1024


def _dot(a, b):
    return jnp.dot(a, b, preferred_element_type=F32)


def _sigmoid(x):
    return 1.0 / (1.0 + jnp.exp(-x))


def _rms(x, g):
    return x * lax.rsqrt(jnp.mean(x * x, axis=-1, keepdims=True) + RMS_EPS) * g


def _head_mean(x):
    low = lax.broadcasted_iota(jnp.int32, (x.shape[0], 2 * HEAD_DIM), 1) < HEAD_DIM
    cols = []
    for c in range(x.shape[1] // (2 * HEAD_DIM)):
        xs = x[:, 2 * HEAD_DIM * c:2 * HEAD_DIM * (c + 1)]
        lo = jnp.where(low, xs, 0.0)
        s_lo = jnp.sum(lo, axis=-1, keepdims=True)
        s_hi = jnp.sum(xs - lo, axis=-1, keepdims=True)
        cols.append(jnp.where(low, s_lo, s_hi))
    return jnp.concatenate(cols, axis=1) * (1.0 / HEAD_DIM)


def _layer_spec(a, l):
    nd = a.ndim - 1
    return pl.BlockSpec((None,) + a.shape[1:], lambda *_: (l,) + (0,) * nd)


def _const_spec(a):
    return pl.BlockSpec(a.shape, lambda *_: (0,) * a.ndim)


def _inproj_kernel(h_ref, g_ref, w_ref, qg_ref, kg_ref, zeta_ref,
                   pb_ref, pf_ref, pt_ref, p0_ref, p1_ref, *, tm, pad_rows):
    i = pl.program_id(0)

    @pl.when(i == 0)
    def _init():
        p1_ref[...] = jnp.zeros_like(p1_ref)

    @pl.when(lax.rem(i, 2) == 0)
    def _even():
        _inproj_step(h_ref, g_ref, w_ref, qg_ref, kg_ref, zeta_ref,
                     pb_ref, pf_ref, pt_ref, p0_ref, p1_ref, tm=tm, pad_rows=pad_rows)

    @pl.when(lax.rem(i, 2) == 1)
    def _odd():
        _inproj_step(h_ref, g_ref, w_ref, qg_ref, kg_ref, zeta_ref,
                     pb_ref, pf_ref, pt_ref, p1_ref, p0_ref, tm=tm, pad_rows=pad_rows)


def _inproj_step(h_ref, g_ref, w_ref, qg_ref, kg_ref, zeta_ref,
                 pb_ref, pf_ref, pt_ref, pw_ref, pr_ref, *, tm, pad_rows):
    u = _rms(h_ref[...], g_ref[...]).astype(BF16)
    pw_ref[...] = _dot(u, w_ref[...])
    for _ in _proj_groups(lambda lo, hi: pr_ref[:, lo:hi], qg_ref, kg_ref, zeta_ref,
                          pb_ref, pf_ref, pt_ref, tm=tm, pad_rows=pad_rows):
        pass


def _proj_groups(proj, qg_ref, kg_ref, zeta_ref, pb_ref, pf_ref, pt_ref, *, tm, pad_rows):
    for half in range(2):
        lo = Q0 + 256 * half
        q = proj(lo, lo + 256)
        qn = q * lax.rsqrt(_head_mean(q * q) + RMS_EPS) * qg_ref[:, 256 * half:256 * (half + 1)]
        pb_ref[:, PB_Q + 256 * half:PB_Q + 256 * (half + 1)] = qn.astype(BF16)
        yield

    kv = proj(K0, CA0)
    k = kv[:, 0:128]
    v = kv[:, 128:256]
    kn = k * lax.rsqrt(_head_mean(k * k) + RMS_EPS) * kg_ref[...]
    pt_ref[0:128, :] = kn.T.astype(BF16)
    low = lax.broadcasted_iota(jnp.int32, v.shape, 1) < HEAD_DIM
    v_sw = pltpu.roll(v, HEAD_DIM, axis=1)
    pb_ref[:, PB_V:PB_V + 128] = jnp.where(low, v, v_sw).astype(BF16)
    pb_ref[:, PB_V + 128:PB_V + 256] = jnp.where(low, v_sw, v).astype(BF16)
    yield

    cab = proj(CA0, RQ0)
    pf_ref[:, 0:256] = cab[:, 0:256] * _sigmoid(cab[:, 256:512])
    yield

    pb_ref[:, PB_RQ:PB_RQ + 256] = proj(RQ0, RK0).astype(BF16)
    yield
    rk = proj(RK0, RV0)
    if pad_rows:
        row = lax.broadcasted_iota(jnp.int32, (tm, 1), 0)
        rk = jnp.where(row < pad_rows, 0.0, rk)
    pt_ref[128:384, :] = rk.T.astype(BF16)
    yield
    rv = proj(RV0, RG0)
    pb_ref[:, PB_RV:PB_RV + 256] = rv.astype(BF16)
    pb_ref[:, PB_RVZ:PB_RVZ + 256] = (rv * zeta_ref[...]).astype(BF16)
    yield
    rg = proj(RG0, IN_END)
    pf_ref[:, 256:512] = rg * _sigmoid(rg)
    yield


def _inproj(h2, l, g, w, qg, kg, zeta_tile, tm, pad_rows):
    rows, d = h2.shape
    ntiles = rows // tm
    nout = w.shape[2]
    kern = functools.partial(_inproj_kernel, tm=tm, pad_rows=pad_rows)
    return pl.pallas_call(
        kern,
        out_shape=(jax.ShapeDtypeStruct((rows, PB_END), BF16),
                   jax.ShapeDtypeStruct((rows, 512), F32),
                   jax.ShapeDtypeStruct((384, rows), BF16)),
        grid=(ntiles + 1,),
        in_specs=[
            pl.BlockSpec((tm, d), lambda i: (jnp.minimum(i, ntiles - 1), 0)),
            _layer_spec(g, l), _layer_spec(w, l), _layer_spec(qg, l), _layer_spec(kg, l),
            _const_spec(zeta_tile),
        ],
        out_specs=(pl.BlockSpec((tm, PB_END), lambda i: (jnp.maximum(i - 1, 0), 0)),
                   pl.BlockSpec((tm, 512), lambda i: (jnp.maximum(i - 1, 0), 0)),
                   pl.BlockSpec((384, tm), lambda i: (0, jnp.maximum(i - 1, 0)))),
        scratch_shapes=[pltpu.VMEM((tm, nout), F32)] * 2,
        compiler_params=pltpu.CompilerParams(
            dimension_semantics=("arbitrary",), vmem_limit_bytes=VMEM_LIMIT),
        name="in_proj",
    )(h2, g, w, qg, kg, zeta_tile)


def _row_blocks(x, hm_ref):
    r = x.shape[0]
    return jnp.concatenate([x * hm_ref[j, 0:r, :] for j in range(4)], axis=0)


def _diag_blocks(xt, heads):
    z = jnp.zeros_like(xt[0])
    return jnp.concatenate(
        [jnp.concatenate([xt[j] if i == j else z for i in range(heads)], axis=1)
         for j in range(heads)], axis=0)


def _mixer_kernel(sink_ref, pb_ref, pf_ref, pt_ref, bias_a_ref, bias_b_ref, hm_ref, aog_ref,
                  dww_ref, dwb_ref, lng_ref, lnb_ref, pw_ref, cvg_ref,
                  dec_ref, xi_ref, cdec_ref, gng_ref,
                  kbm_in_ref, vbm_in_ref, tail_in_ref, state_in_ref,
                  o_ref, kbm_out_ref, vbm_out_ref, tail_out_ref, state_out_ref,
                  kbp_ref, vbp_ref, kbm_ref, vbm_ref, tail_ref, state_ref, *, layer, first_block, bps):
    n = pl.program_id(1)

    @pl.when(n == 0)
    def _init():
        kbp_ref[...] = jnp.zeros_like(kbp_ref)
        vbp_ref[...] = jnp.zeros_like(vbp_ref)
        if first_block:
            tail_ref[1] = jnp.zeros(tail_ref.shape[1:], F32)
            state_ref[1] = jnp.zeros(state_ref.shape[1:], F32)
            for g in range(2):
                ktf = pt_ref[HEAD_DIM * g:HEAD_DIM * (g + 1), :].astype(F32)
                tok = lax.broadcasted_iota(jnp.int32, ktf.shape, 1)
                ktm = jnp.where(tok >= PAD, ktf, 0.0)
                kbm_ref[g] = jnp.concatenate(
                    [pltpu.roll(ktm, N_META * (j + 1), axis=1) for j in range(4)], axis=0).astype(BF16)
                vm = pb_ref[PAD:BLOCK, PB_V + 128 * g:PB_V + 128 * (g + 1)]
                z = jnp.zeros((BLOCK - 4 * N_META, 256), BF16)
                vbm_ref[g] = jnp.concatenate(
                    [_row_blocks(jnp.concatenate([vm, vm], axis=1), hm_ref), z], axis=0)
        else:
            tail_ref[1] = tail_in_ref[...]
            state_ref[1] = state_in_ref[...]
            kbm_ref[...] = kbm_in_ref[...]
            vbm_ref[...] = vbm_in_ref[...]

    for wr in range(2):
        rd = 1 - wr

        @pl.when(lax.rem(n, 2) == wr)
        def _step():
            prev = dict(kb=[lambda g=g: kbp_ref[2 * rd + g] for g in range(2)],
                        vb=[lambda g=g: vbp_ref[2 * rd + g] for g in range(2)],
                        tail=lambda: tail_ref[rd], state=lambda: state_ref[rd])
            for k in range(bps):
                cur = _mixer_block(sink_ref, pb_ref, pf_ref, pt_ref, bias_a_ref if k == 0 else bias_b_ref, hm_ref,
                                   aog_ref, dww_ref, dwb_ref, lng_ref, lnb_ref, pw_ref, cvg_ref,
                                   dec_ref, xi_ref, cdec_ref, gng_ref, o_ref, kbm_ref, vbm_ref,
                                   prev, layer=layer, row0=k * BLOCK)
                prev = dict(kb=[lambda g=g, c=cur: c["kb"][g] for g in range(2)],
                            vb=[lambda g=g, c=cur: c["vb"][g] for g in range(2)],
                            tail=lambda c=cur: c["tail"], state=lambda c=cur: c["state"])
            for g in range(2):
                kbp_ref[2 * wr + g] = cur["kb"][g]
                vbp_ref[2 * wr + g] = cur["vb"][g]
            tail_ref[wr] = cur["tail"]
            state_ref[wr] = cur["state"]

    @pl.when(n == pl.num_programs(1) - 1)
    def _hand_over():
        kbm_out_ref[...] = kbm_ref[...]
        vbm_out_ref[...] = vbm_ref[...]
        last = (pl.num_programs(1) - 1) % 2
        tail_out_ref[...] = tail_ref[last]
        state_out_ref[...] = state_ref[last]


def _mixer_block(sink_ref, pb_ref, pf_ref, pt_ref, bias_ref, hm_ref, aog_ref,
                 dww_ref, dwb_ref, lng_ref, lnb_ref, pw_ref, cvg_ref,
                 dec_ref, xi_ref, cdec_ref, gng_ref, o_ref, kbm_ref, vbm_ref,
                 prev, *, layer, row0, tick=lambda: None):
    pb_ref = pb_ref.at[row0:row0 + BLOCK, :]
    pf_ref = pf_ref.at[row0:row0 + BLOCK, :]
    pt_ref = pt_ref.at[:, row0:row0 + BLOCK]
    o_ref = o_ref.at[row0:row0 + BLOCK, :]
    m_head = lax.shift_right_logical(lax.broadcasted_iota(jnp.int32, (BLOCK, BLOCK), 1), 4)

    att, kb_out, vb_out = [], [], []
    for g in range(2):
        tick()
        kt = pt_ref[HEAD_DIM * g:HEAD_DIM * (g + 1), :]
        vg = pb_ref[:, PB_V + 128 * g:PB_V + 128 * (g + 1)]
        vrep = jnp.concatenate([vg, vg], axis=1)
        kb_cur = _diag_blocks([kt] * 4, 4)
        vb_cur = _row_blocks(vrep, hm_ref)
        kb_out.append(kb_cur)
        vb_out.append(vb_cur)

        qg = pb_ref[:, PB_Q + 256 * g:PB_Q + 256 * (g + 1)]
        gb = 1152 * g
        s_prev = _dot(qg, prev["kb"][g]()) + bias_ref[0, :, gb:gb + 512]
        s_cur = _dot(qg, kb_cur) + bias_ref[0, :, gb + 512:gb + 1024]
        sm = _dot(qg, kbm_ref[g]) + bias_ref[0, :, gb + 1024:gb + 1152]

        mx = []
        for j in range(4):
            t = jnp.maximum(jnp.maximum(s_prev[:, 128 * j:128 * (j + 1)], s_cur[:, 128 * j:128 * (j + 1)]),
                            jnp.where(m_head == j, sm, NEG))
            mx.append(jnp.maximum(jnp.max(t, axis=-1, keepdims=True), sink_ref[layer, 4 * g + j]))
        m_meta = jnp.where(m_head == 0, mx[0],
                           jnp.where(m_head == 1, mx[1],
                                     jnp.where(m_head == 2, mx[2], mx[3])))
        pm = jnp.exp(sm - m_meta)
        pp, pc, linv = [], [], []
        for j in range(4):
            tick()
            ppj = jnp.exp(s_prev[:, 128 * j:128 * (j + 1)] - mx[j])
            pcj = jnp.exp(s_cur[:, 128 * j:128 * (j + 1)] - mx[j])
            lj = jnp.sum(ppj + pcj + jnp.where(m_head == j, pm, 0.0), axis=-1, keepdims=True)
            lj = lj + jnp.exp(sink_ref[layer, 4 * g + j] - mx[j])
            pp.append(ppj.astype(BF16))
            pc.append(pcj.astype(BF16))
            linv.append(1.0 / lj)
        o = (_dot(jnp.concatenate(pp, axis=1), prev["vb"][g]())
             + _dot(jnp.concatenate(pc, axis=1), vb_cur)
             + _dot(pm.astype(BF16), vbm_ref[g]))
        seg = lax.shift_right_logical(lax.broadcasted_iota(jnp.int32, (BLOCK, 256), 1), 6)
        lfull = jnp.where(seg == 0, linv[0],
                          jnp.where(seg == 1, linv[1],
                                    jnp.where(seg == 2, linv[2], linv[3])))
        att.append(o * lfull)
    y_att = _rms(jnp.concatenate(att, axis=1), aog_ref[...])

    tick()
    u = pf_ref[:, 0:256]
    ub = jnp.concatenate([prev["tail"](), u], axis=0)
    base = CONV_TAIL - (CONV_K - 1)
    nrow = CONV_TAIL + BLOCK
    acc = None
    for r in range(SUBLANES):
        taps = [t for t in range(CONV_K) if (base + t) % SUBLANES == r]
        xr = ub if r == 0 else pltpu.roll(ub, nrow - r, axis=0)
        for t in taps:
            a = base + t - r
            term = dww_ref[t:t + 1, :] * xr[a:a + BLOCK, :]
            acc = term if acc is None else acc + term
    acc = acc + dwb_ref[...]
    mu = jnp.mean(acc, axis=-1, keepdims=True)
    dc = acc - mu
    var = jnp.mean(dc * dc, axis=-1, keepdims=True)
    cn = dc * lax.rsqrt(var + LN_EPS) * lng_ref[...] + lnb_ref[...]
    cs = cn * _sigmoid(cn)
    y_cv = _rms(_dot(cs.astype(BF16), pw_ref[...]), cvg_ref[...])

    tick()
    rq = pb_ref[:, PB_RQ:PB_RQ + 256]
    rkt = pt_ref[128:384, :]
    kb_r = _diag_blocks([rkt[HEAD_DIM * hh:HEAD_DIM * (hh + 1), :] for hh in range(RET_HEADS)],
                        RET_HEADS)
    vb_r = _row_blocks(pb_ref[:, PB_RV:PB_RV + 256], hm_ref)
    scores = _dot(rq, kb_r) * dec_ref[...]
    state = prev["state"]()
    y_r = _dot(scores.astype(BF16), vb_r) + _dot(rq, state.astype(BF16)) * xi_ref[...]
    kv = _dot(rkt, pb_ref[:, PB_RVZ:PB_RVZ + 256])
    r2 = lax.shift_right_logical(lax.broadcasted_iota(jnp.int32, (256, 256), 0), 6)
    c2 = lax.shift_right_logical(lax.broadcasted_iota(jnp.int32, (256, 256), 1), 6)
    state_new = cdec_ref[...] * state + jnp.where(r2 == c2, kv, 0.0)
    dr = y_r - _head_mean(y_r)
    var_r = _head_mean(dr * dr)
    yn = dr * lax.rsqrt(var_r + LN_EPS) * gng_ref[...]
    y_ret = pf_ref[:, 256:512] * yn

    tick()
    o_ref[...] = jnp.concatenate([y_att, y_cv, y_ret], axis=1).astype(BF16)
    return dict(kb=kb_out, vb=vb_out, tail=u[BLOCK - CONV_TAIL:BLOCK, :], state=state_new)


def _mixer(pb, pf, pt, l, sinks, bias, hm, aog, dww, dwb, lng, lnb, pw, cvg,
           dec, xi, cdec, gng, state_in, batch, nblk, first_block):
    rows = pb.shape[0]
    d = aog.shape[2] + cvg.shape[2] + gng.shape[2]
    pos0 = 0 if first_block else 1
    bps = 2 if nblk % 2 == 0 else 1
    steps = nblk // bps
    tb = bps * BLOCK

    def rowblk(b, n):
        return (b * steps + n, 0)

    def colblk(b, n):
        return (0, b * steps + n)

    def lspec(a):
        return _layer_spec(a, l)

    def bias_spec(k):
        return pl.BlockSpec((1,) + bias.shape[1:], lambda b, n: (jnp.minimum(pos0 + n * bps + k, 2), 0, 0))

    kern = functools.partial(_mixer_kernel, layer=l, first_block=first_block, bps=bps)
    state_shapes = tuple(jax.ShapeDtypeStruct(a.shape, a.dtype) for a in state_in)
    return pl.pallas_call(
        kern,
        out_shape=(jax.ShapeDtypeStruct((rows, d), BF16),) + state_shapes,
        grid=(batch, steps),
        in_specs=[
            pl.BlockSpec(memory_space=pltpu.SMEM),
            pl.BlockSpec((tb, pb.shape[1]), rowblk),
            pl.BlockSpec((tb, pf.shape[1]), rowblk),
            pl.BlockSpec((pt.shape[0], tb), colblk),
            bias_spec(0), bias_spec(bps - 1),
            _const_spec(hm), lspec(aog),
            lspec(dww), lspec(dwb), lspec(lng), lspec(lnb), lspec(pw), lspec(cvg),
            _const_spec(dec), _const_spec(xi), _const_spec(cdec), lspec(gng),
        ] + [_const_spec(a) for a in state_in],
        out_specs=(pl.BlockSpec((tb, d), rowblk),) + tuple(_const_spec(a) for a in state_in),
        scratch_shapes=[
            pltpu.VMEM((4, 256, 4 * BLOCK), BF16),
            pltpu.VMEM((4, 4 * BLOCK, 256), BF16),
            pltpu.VMEM((2, 256, BLOCK), BF16),
            pltpu.VMEM((2, BLOCK, 256), BF16),
            pltpu.VMEM((2, CONV_TAIL, 256), F32),
            pltpu.VMEM((2, 256, 256), F32),
        ],
        compiler_params=pltpu.CompilerParams(
            dimension_semantics=("arbitrary", "arbitrary"), vmem_limit_bytes=VMEM_LIMIT),
        name="mixer",
    )(sinks, pb, pf, pt, bias, bias, hm, aog, dww, dwb, lng, lnb, pw, cvg,
      dec, xi, cdec, gng, *state_in)


def _mixer_state_zeros():
    return (jnp.zeros((2, 256, BLOCK), BF16), jnp.zeros((2, BLOCK, 256), BF16),
            jnp.zeros((CONV_TAIL, 256), F32), jnp.zeros((256, 256), F32))


def _fused_kernel(sink_ref, h_ref, h0_ref, g_ref, w_ref, qg_ref, kg_ref, zeta_ref,
                  bias_a_ref, bias_b_ref, hm_ref, aog_ref,
                  dww_ref, dwb_ref, lng_ref, lnb_ref, pw_ref, cvg_ref,
                  dec_ref, xi_ref, cdec_ref, gng_ref,
                  kbm_in_ref, vbm_in_ref, tail_in_ref, state_in_ref,
                  o_ref,
                  kbp_ref, vbp_ref, kbm_ref, vbm_ref, tail_ref, state_ref,
                  pbs_ref, pfs_ref, pts_ref, *, layer, bps):
    b = pl.program_id(0)
    n = pl.program_id(1)
    tb = bps * BLOCK

    def project(rows_ref, slot):
        u = _rms(rows_ref[...], g_ref[...]).astype(BF16)
        yield
        bounds = (Q0, Q0 + 256, K0, CA0, RQ0, RK0, RV0, RG0, IN_END)
        ahead = {bounds[0]: _dot(u, w_ref[:, bounds[0]:bounds[1]])}
        yield

        def proj(lo, hi):
            i = bounds.index(hi)
            if i + 1 < len(bounds):
                ahead[hi] = _dot(u, w_ref[:, hi:bounds[i + 1]])
            return ahead.pop(lo)

        yield from _proj_groups(proj, qg_ref, kg_ref, zeta_ref,
                                pbs_ref.at[slot], pfs_ref.at[slot], pts_ref.at[slot], tm=tb, pad_rows=0)

    @pl.when(jnp.logical_and(b == 0, n == 0))
    def _first_operands():
        for _ in project(h0_ref, 0):
            pass

    @pl.when(n == 0)
    def _init():
        kbp_ref[...] = jnp.zeros_like(kbp_ref)
        vbp_ref[...] = jnp.zeros_like(vbp_ref)
        tail_ref[1] = tail_in_ref[...]
        state_ref[1] = state_in_ref[...]
        kbm_ref[...] = kbm_in_ref[...]
        vbm_ref[...] = vbm_in_ref[...]

    for cur_slot in range(2):
        nxt_slot = 1 - cur_slot

        @pl.when(lax.rem(n, 2) == cur_slot)
        def _step():
            pieces = project(h_ref, nxt_slot)

            ticks = [0]

            def tick():
                if ticks[0] % 2 == 0:
                    next(pieces, None)
                ticks[0] += 1

            prev = dict(kb=[lambda g=g: kbp_ref[2 * nxt_slot + g] for g in range(2)],
                        vb=[lambda g=g: vbp_ref[2 * nxt_slot + g] for g in range(2)],
                        tail=lambda: tail_ref[nxt_slot], state=lambda: state_ref[nxt_slot])
            for k in range(bps):
                cur = _mixer_block(sink_ref, pbs_ref.at[cur_slot], pfs_ref.at[cur_slot], pts_ref.at[cur_slot],
                                   bias_a_ref if k == 0 else bias_b_ref, hm_ref,
                                   aog_ref, dww_ref, dwb_ref, lng_ref, lnb_ref, pw_ref, cvg_ref,
                                   dec_ref, xi_ref, cdec_ref, gng_ref, o_ref, kbm_ref, vbm_ref,
                                   prev, layer=layer, row0=k * BLOCK, tick=tick)
                prev = dict(kb=[lambda g=g, c=cur: c["kb"][g] for g in range(2)],
                            vb=[lambda g=g, c=cur: c["vb"][g] for g in range(2)],
                            tail=lambda c=cur: c["tail"], state=lambda c=cur: c["state"])
            for _ in pieces:
                pass
            for g in range(2):
                kbp_ref[2 * cur_slot + g] = cur["kb"][g]
                vbp_ref[2 * cur_slot + g] = cur["vb"][g]
            tail_ref[cur_slot] = cur["tail"]
            state_ref[cur_slot] = cur["state"]


def _fused(h2, l, g, w, qg, kg, zeta_tile, sinks, bias, hm, aog, dww, dwb, lng, lnb, pw, cvg,
           dec, xi, cdec, gng, state_in, batch, nblk):
    rows, d = h2.shape
    dout = aog.shape[2] + cvg.shape[2] + gng.shape[2]
    bps = 2
    steps = nblk // bps
    assert steps % 2 == 0
    tb = bps * BLOCK
    last = batch * steps - 1

    def rowblk(b, n):
        return (b * steps + n, 0)

    def lspec(a):
        nd = a.ndim - 1
        return pl.BlockSpec((None,) + a.shape[1:], lambda *_: (l,) + (0,) * nd, pipeline_mode=pl.Buffered(1))

    def bias_spec(k):
        return pl.BlockSpec((1,) + bias.shape[1:], lambda b, n: (jnp.minimum(1 + n * bps + k, 2), 0, 0))

    kern = functools.partial(_fused_kernel, layer=l, bps=bps)
    return pl.pallas_call(
        kern,
        out_shape=jax.ShapeDtypeStruct((rows, dout), BF16),
        grid=(batch, steps),
        in_specs=[
            pl.BlockSpec(memory_space=pltpu.SMEM),
            pl.BlockSpec((tb, d), lambda b, n: (jnp.minimum(b * steps + n + 1, last), 0)),
            pl.BlockSpec((tb, d), lambda b, n: (0, 0)),
            lspec(g), lspec(w), lspec(qg), lspec(kg), _const_spec(zeta_tile),
            bias_spec(0), bias_spec(1),
            _const_spec(hm), lspec(aog),
            lspec(dww), lspec(dwb), lspec(lng), lspec(lnb), lspec(pw), lspec(cvg),
            _const_spec(dec), _const_spec(xi), _const_spec(cdec), lspec(gng),
        ] + [_const_spec(a) for a in state_in],
        out_specs=pl.BlockSpec((tb, dout), rowblk),
        scratch_shapes=[
            pltpu.VMEM((4, 256, 4 * BLOCK), BF16),
            pltpu.VMEM((4, 4 * BLOCK, 256), BF16),
            pltpu.VMEM((2, 256, BLOCK), BF16),
            pltpu.VMEM((2, BLOCK, 256), BF16),
            pltpu.VMEM((2, CONV_TAIL, 256), F32),
            pltpu.VMEM((2, 256, 256), F32),
            pltpu.VMEM((2, tb, PB_END), BF16),
            pltpu.VMEM((2, tb, 512), F32),
            pltpu.VMEM((2, 384, tb), BF16),
        ],
        compiler_params=pltpu.CompilerParams(
            dimension_semantics=("arbitrary", "arbitrary"), vmem_limit_bytes=VMEM_LIMIT),
        name="proj_mixer",
    )(sinks, h2, h2, g, w, qg, kg, zeta_tile, bias, bias, hm, aog, dww, dwb, lng, lnb, pw, cvg,
      dec, xi, cdec, gng, *state_in)


def _ffn_kernel(h_ref, y_ref, wout_ref, g_ref, up_ref, dww_ref, dwb_ref, down_ref, carry_in_ref,
                o_ref, carry_out_ref, carry_ref, act_ref, *, tm, dff, cw, pad_rows):
    t = pl.program_id(1)
    nchunk = dff // cw
    wr = lax.rem(t, 2)
    rd = 1 - wr

    @pl.when(t == 0)
    def _init():
        carry_ref[1] = carry_in_ref[...]

    def keep(y):
        if not pad_rows:
            return y
        row = t * tm + lax.broadcasted_iota(jnp.int32, (tm, 1), 0)
        return jnp.where(row >= pad_rows, y, 0.0)

    h = h_ref[...] + keep(_dot(y_ref[...], wout_ref[...]))
    u = _rms(h, g_ref[...]).astype(BF16)
    for c in range(nchunk):
        halves = []
        for off in (c * cw, dff + c * cw):
            f = _dot(u, up_ref[:, off:off + cw])
            fe = jnp.concatenate([carry_ref[rd, :, off:off + cw], f], axis=0)
            carry_ref[wr, :, off:off + cw] = f[tm - FFN_TAIL:tm, :]
            fc = dwb_ref[:, off:off + cw] + dww_ref[FFN_CONV_K - 1:FFN_CONV_K, off:off + cw] * f
            for s in range(1, FFN_CONV_K):
                kk = FFN_CONV_K - 1 - s
                shifted = pltpu.roll(fe, s, axis=0)[FFN_TAIL:FFN_TAIL + tm, :]
                fc = fc + dww_ref[kk:kk + 1, off:off + cw] * shifted
            halves.append(fc)
        act_ref[:, c * cw:(c + 1) * cw] = (halves[0] * _sigmoid(halves[0]) * halves[1]).astype(BF16)
    o_ref[...] = h + keep(_dot(act_ref[...], down_ref[...]))

    @pl.when(t == pl.num_programs(1) - 1)
    def _hand_over():
        carry_out_ref[...] = carry_ref[wr]


def _ffn(h2, ycat, l, wout, g, up, dww, dwb, down, carry_in, batch, tiles, tm, cw, pad_rows):
    rows, d = h2.shape
    dff = down.shape[1]

    def rowblk(b, t):
        return (b * tiles + t, 0)

    def wspec(a):
        nd = a.ndim - 1
        return pl.BlockSpec((None,) + a.shape[1:], lambda b, t: (l,) + (0,) * nd,
                            pipeline_mode=pl.Buffered(1))

    kern = functools.partial(_ffn_kernel, tm=tm, dff=dff, cw=cw, pad_rows=pad_rows)
    return pl.pallas_call(
        kern,
        out_shape=(jax.ShapeDtypeStruct((rows, d), F32),
                   jax.ShapeDtypeStruct(carry_in.shape, F32)),
        grid=(batch, tiles),
        in_specs=[
            pl.BlockSpec((tm, d), rowblk),
            pl.BlockSpec((tm, d), rowblk),
            wspec(wout), wspec(g), wspec(up), wspec(dww), wspec(dwb), wspec(down),
            _const_spec(carry_in),
        ],
        out_specs=(pl.BlockSpec((tm, d), rowblk), _const_spec(carry_in)),
        scratch_shapes=[
            pltpu.VMEM((2, FFN_TAIL, 2 * dff), F32),
            pltpu.VMEM((tm, dff), BF16),
        ],
        compiler_params=pltpu.CompilerParams(
            dimension_semantics=("arbitrary", "arbitrary"), vmem_limit_bytes=VMEM_LIMIT),
        name="ffn",
    )(h2, ycat, wout, g, up, dww, dwb, down, carry_in)


def _retention_tables():
    log_gamma = jnp.log1p(-jnp.exp2(-5.0 - jnp.arange(RET_HEADS, dtype=F32)))
    idx = jnp.arange(BLOCK, dtype=F32)
    diff = idx[:, None] - idx[None, :]
    decay = jnp.where(diff[None] >= 0,
                      jnp.exp(jnp.maximum(diff, 0.0)[None] * log_gamma[:, None, None]), 0.0)
    zeta = jnp.exp((BLOCK - 1 - idx)[None, :] * log_gamma[:, None])
    xi = jnp.exp((idx + 1.0)[None, :] * log_gamma[:, None])
    chunk_decay = jnp.exp(BLOCK * log_gamma)
    scale = RET_DK ** -0.5
    dec_t = jnp.transpose(decay, (1, 0, 2)).reshape(BLOCK, RET_HEADS * BLOCK) * scale
    zeta_t = jnp.repeat(zeta.T, RET_DK, axis=1) * scale
    xi_t = jnp.repeat(xi.T, RET_DK, axis=1)
    cdec_t = jnp.repeat(chunk_decay, RET_DK)[None, :]
    return dec_t, zeta_t, xi_t, cdec_t


def _attention_bias():
    n_kv = N_Q_HEADS // GQA_GROUP
    slopes = jnp.exp2(-8.0 * jnp.arange(1, N_Q_HEADS + 1, dtype=F32) / N_Q_HEADS).reshape(n_kv, GQA_GROUP)
    n = jnp.arange(3)[:, None, None]
    i = jnp.arange(BLOCK)[None, :, None]
    c = jnp.arange(BLOCK)[None, None, :]

    def head_tiles(ok, dist):
        b = jnp.where(ok[:, :, None, None, :],
                      -slopes[None, None, :, :, None] * dist.astype(F32)[:, :, None, None, :], NEG)
        return b.reshape(3, BLOCK, n_kv, GQA_GROUP * BLOCK)

    prev = head_tiles(jnp.logical_and(c > i, n >= 2), jnp.broadcast_to(i - c + BLOCK, (3, BLOCK, BLOCK)))
    cur = head_tiles(jnp.logical_and(c <= i, n >= 1), jnp.broadcast_to(i - c, (3, BLOCK, BLOCK)))
    d_meta = n * BLOCK + i - PAD - (c % N_META)
    ok_meta = jnp.logical_and(d_meta >= 0, c < GQA_GROUP * N_META)
    sl_meta = slopes[:, jnp.minimum(jnp.arange(BLOCK) // N_META, GQA_GROUP - 1)]
    meta = jnp.where(ok_meta[:, :, None, :],
                     -sl_meta[None, None] * jnp.minimum(d_meta, WINDOW).astype(F32)[:, :, None, :], NEG)
    return jnp.concatenate([prev, cur, meta], axis=-1).reshape(3, BLOCK, -1)


def kernel(x, meta, norm_mix_g, w_in, q_norm_g, k_norm_g, attn_sinks, attn_out_g, cv_dw_w, cv_dw_b, cv_ln_g, cv_ln_b, cv_pw, cv_out_g, ret_gn_g, w_out, norm_ffn_g, ffn_up, ffn_dw_w, ffn_dw_b, ffn_down):
    batch, seq, d = x.shape
    depth = w_in.shape[0]
    dff = ffn_down.shape[1]

    dec_t, zeta_t, xi_t, cdec_t = _retention_tables()
    bias = _attention_bias()
    seg = jnp.arange(256) // HEAD_DIM
    hm = jnp.broadcast_to((seg[None, :] == jnp.arange(4)[:, None])[:, None, :], (4, BLOCK, 256)).astype(BF16)

    def vec(a):
        return a.reshape(depth, 1, a.shape[-1])

    w_in_b, w_out_b, up_b, down_b, pw_b = (a.astype(BF16) for a in (w_in, w_out, ffn_up, ffn_down, cv_pw))
    mix_g, ffn_g = vec(norm_mix_g), vec(norm_ffn_g)
    qg = vec(jnp.tile(q_norm_g, (1, N_Q_HEADS)) * (HEAD_DIM ** -0.5))
    kg = vec(jnp.tile(k_norm_g, (1, N_Q_HEADS // GQA_GROUP)))
    aog, dwb, lng, lnb, cvg, gng, fdwb = (vec(a) for a in (
        attn_out_g, cv_dw_b, cv_ln_g, cv_ln_b, cv_out_g, ret_gn_g, ffn_dw_b))

    h0 = jnp.concatenate([jnp.zeros((PAD, d), x.dtype), meta.astype(x.dtype)], axis=0)
    hr = x.reshape(batch * seq, d)
    nblk = seq // BLOCK
    in_tm = 2 * BLOCK
    ffn_tm = 512
    zeta_tile = jnp.tile(zeta_t, (in_tm // BLOCK, 1))

    def layer(h2, l, state_in, carry_in, nb, blocks, tm_in, tm_ffn, zt, first):
        pad_rows = PAD if first else 0
        if first:
            pb, pf, pt = _inproj(h2, l, mix_g, w_in_b, qg, kg, zt, tm_in, pad_rows)
            ycat, *state = _mixer(pb, pf, pt, l, attn_sinks, bias, hm, aog, cv_dw_w, dwb, lng, lnb,
                                  pw_b, cvg, dec_t, xi_t, cdec_t, gng, state_in, nb, blocks, first)
        else:
            ycat = _fused(h2, l, mix_g, w_in_b, qg, kg, zt, attn_sinks, bias, hm, aog, cv_dw_w, dwb, lng, lnb,
                          pw_b, cvg, dec_t, xi_t, cdec_t, gng, state_in, nb, blocks)
            state = ()
        h2, carry = _ffn(h2, ycat, l, w_out_b, ffn_g, up_b, ffn_dw_w, fdwb, down_b, carry_in,
                         nb, blocks * BLOCK // tm_ffn, tm_ffn, 256, pad_rows)
        return h2, tuple(state), carry

    for l in range(depth):
        h0, state0, carry0 = layer(h0, l, _mixer_state_zeros(), jnp.zeros((FFN_TAIL, 2 * dff), F32),
                                   1, 1, BLOCK, BLOCK, zeta_t, True)
        hr, _, _ = layer(hr, l, state0, carry0, batch, nblk, in_tm, ffn_tm, zeta_tile, False)
    return hr.reshape(batch, seq, d)
```
